```python
import math
import jax
import jax.numpy as jnp
from jax import lax
import numpy as np

D_MODEL = 1024
BATCH = 2
SEQ = 16384
DEPTH = 2
DEC_BATCH = 8
DEC_SEQ = 64
PAST_LEN = 1024

CHUNK = 64
N_MEM = 256
EPS = 1e-6
MASK_VALUE = -1e30
POOL_WINDOWS = (2, 4, 8, 16)
POOL_GROUPS = 4
POOL_GROUP_W = 128
POOL_WIDTH = 512
POOL_STATE = 15
ATT_HEADS = 4
ATT_DK = 64
ATT_DV = 128
ROT_DIMS = 16
ROPE_THETA = 500000.0
Q_BLOCK = 128
HG_HEADS = 4
HG_DK = 64
HG_DV = 128
X_HEADS = 4
X_HD = D_MODEL // X_HEADS
D_FF = 2816
N_BRANCH = 3
IN_SIZES = (POOL_WIDTH, ATT_HEADS * 2 * ATT_DK, ATT_HEADS * 2 * ATT_DK, ATT_HEADS * ATT_DV,
            HG_HEADS * HG_DK, HG_HEADS * HG_DK, HG_HEADS * HG_DV, HG_HEADS * HG_DV,
            N_BRANCH * D_MODEL)
IN_COLS = sum(IN_SIZES)

kernel_name = 'hybrid_streaming_encoder_step'


def _split_points():
    pts, acc = [], 0
    for s in IN_SIZES[:-1]:
        acc += s
        pts.append(acc)
    return pts


def rms_norm(x, g):
    xf = x.astype(jnp.float32)
    y = xf * lax.rsqrt(jnp.mean(xf * xf, axis=-1, keepdims=True) + EPS)
    return (y * g.astype(jnp.float32)).astype(x.dtype)


def partial_rope(x, pos):
    half = ROT_DIMS // 2
    inv = ROPE_THETA ** (-jnp.arange(half, dtype=jnp.float32) / half)
    ang = pos.astype(jnp.float32)[:, None] * inv[None, :]
    cos = jnp.cos(ang)[:, None, :]
    sin = jnp.sin(ang)[:, None, :]
    xr = x[..., :ROT_DIMS].astype(jnp.float32)
    x1, x2 = xr[..., :half], xr[..., half:]
    rot = jnp.concatenate([x1 * cos - x2 * sin, x2 * cos + x1 * sin], axis=-1).astype(x.dtype)
    return jnp.concatenate([rot, x[..., ROT_DIMS:]], axis=-1)


def ffn_half(x, g, w_i, w_o):
    a, b = jnp.split(rms_norm(x, g) @ w_i, 2, axis=-1)
    return x + 0.5 * ((jax.nn.silu(a) * b) @ w_o)


def pool_mix(u, prev, p0, pool_w, pool_scale):
    B, T, C = u.shape
    ext = jnp.concatenate([prev, u], axis=1).astype(jnp.float32)
    cs = jnp.pad(jnp.cumsum(ext, axis=1), ((0, 0), (1, 0), (0, 0)))
    cnt_pos = p0 + 1 + jnp.arange(T, dtype=jnp.int32)
    means = []
    for g, w in enumerate(POOL_WINDOWS):
        sl = slice(g * POOL_GROUP_W, (g + 1) * POOL_GROUP_W)
        win = (cs[:, POOL_STATE + 1:POOL_STATE + 1 + T, sl]
               - cs[:, POOL_STATE + 1 - w:POOL_STATE + 1 - w + T, sl])
        cnt = jnp.minimum(cnt_pos, w).astype(jnp.float32)[None, :, None]
        means.append(win / cnt)
    d = (jnp.concatenate(means, axis=-1) - ext[:, POOL_STATE:]).astype(u.dtype)
    y = jnp.einsum('btgc,gcd->btgd', d.reshape(B, T, POOL_GROUPS, POOL_GROUP_W), pool_w)
    return y.reshape(B, T, C) * pool_scale


def diff_attention(q, k, v, q_pos, k_pos, lam):
    B, Tq = q.shape[:2]
    k_chunk = k_pos // CHUNK
    scale = ATT_DK ** -0.5

    def block(qb, qpos):
        s = jnp.einsum('bqhmd,bkhmd->bhmqk', qb, k, preferred_element_type=jnp.float32) * scale
        mask = k_chunk[None, :] <= (qpos // CHUNK)[:, None]
        p = jax.nn.softmax(jnp.where(mask, s, MASK_VALUE), axis=-1)
        a = p[:, :, 0] - lam * p[:, :, 1]
        return jnp.einsum('bhqk,bkhd->bqhd', a.astype(v.dtype), v)

    if Tq > Q_BLOCK and Tq % Q_BLOCK == 0:
        nb = Tq // Q_BLOCK
        qs = jnp.swapaxes(q.reshape((B, nb, Q_BLOCK) + q.shape[2:]), 0, 1)
        ps = q_pos.reshape(nb, Q_BLOCK)
        out = lax.map(lambda args: block(args[0], args[1]), (qs, ps))
        return jnp.swapaxes(out, 0, 1).reshape(B, Tq, ATT_HEADS, ATT_DV)
    return block(q, q_pos)


def hgrn2_scan(q, k, logf, v, s0):
    B, T, H, DK = q.shape
    DV = v.shape[-1]
    f32 = jnp.float32
    pad = (-T) % CHUNK
    q, k, logf, v = (a.astype(f32) for a in (q, k, logf, v))
    if pad:
        widths = ((0, 0), (0, pad), (0, 0), (0, 0))
        q, k, logf, v = (jnp.pad(a, widths) for a in (q, k, logf, v))
    n = (T + pad) // CHUNK

    def to_chunks(a):
        return a.reshape(B, n, CHUNK, H, a.shape[-1]).transpose(1, 0, 3, 2, 4)

    causal = jnp.tril(jnp.ones((CHUNK, CHUNK), dtype=bool))[None, None, :, :, None]

    def step(S, xs):
        qi, ki, gi, vi = xs
        b = jnp.cumsum(gi, axis=2)
        diff = jnp.where(causal, b[:, :, :, None, :] - b[:, :, None, :, :], 0.0)
        decay = jnp.where(causal, jnp.exp(diff), 0.0)
        A = jnp.einsum('bhtd,bhsd,bhtsd->bhts', qi, ki, decay)
        o = (jnp.einsum('bhts,bhse->bhte', A, vi)
             + jnp.einsum('bhtd,bhde->bhte', qi * jnp.exp(b), S))
        b_last = b[:, :, -1:, :]
        S = (jnp.exp(b_last[:, :, 0, :])[..., None] * S
             + jnp.einsum('bhsd,bhse->bhde', ki * jnp.exp(b_last - b), vi))
        return S, o

    S, o = lax.scan(step, s0.astype(f32), tuple(to_chunks(a) for a in (q, k, logf, v)))
    o = o.transpose(1, 0, 3, 2, 4).reshape(B, n * CHUNK, H, DV)[:, :T]
    return o, S.astype(s0.dtype)


def memory_kv(mem, g, w_ckv, kn):
    B, N, _ = mem.shape
    mk, mv = jnp.split(rms_norm(mem, g) @ w_ckv, 2, axis=-1)
    mk = rms_norm(mk.reshape(B, N, X_HEADS, X_HD), kn)
    return mk, mv.reshape(B, N, X_HEADS, X_HD)


def cross_attend(x, mk, mv, g, w_cq, qn, w_co):
    B, T, _ = x.shape
    q = rms_norm((rms_norm(x, g) @ w_cq).reshape(B, T, X_HEADS, X_HD), qn)
    s = jnp.einsum('bqhd,bkhd->bhqk', q, mk, preferred_element_type=jnp.float32) * (X_HD ** -0.5)
    p = jax.nn.softmax(s, axis=-1).astype(mv.dtype)
    o = jnp.einsum('bhqk,bkhd->bqhd', p, mv).reshape(B, T, D_MODEL)
    return x + o @ w_co


def token_mixers(x, layer, p0, past_k, past_v, pool_prev, hg_state, lb,
                 g_mix, w_in, pool_w, pool_scale, att_qn, att_kn, lq1, lk1, lq2, lk2,
                 att_subln, hg_outn, w_up_pool, w_up_att, w_up_hgrn, w_out):
    B, T, _ = x.shape
    f32 = jnp.float32
    pos = p0 + jnp.arange(T, dtype=jnp.int32)
    h = rms_norm(x, g_mix)
    u_pool, aq, ak, av, hq, hf, hi, hg, gates = jnp.split(h @ w_in, _split_points(), axis=-1)

    y_pool = pool_mix(u_pool, pool_prev, p0, pool_w, pool_scale)
    pool_new = jnp.concatenate([pool_prev, u_pool], axis=1)[:, -POOL_STATE:]

    aq = partial_rope(rms_norm(aq.reshape(B, T, 2 * ATT_HEADS, ATT_DK), att_qn), pos)
    ak = partial_rope(rms_norm(ak.reshape(B, T, 2 * ATT_HEADS, ATT_DK), att_kn), pos)
    k_rows = ak.reshape(B, T, ATT_HEADS, 2 * ATT_DK)
    v_rows = av.reshape(B, T, ATT_HEADS, ATT_DV)
    if past_k is None:
        k_all, v_all, k_pos = k_rows, v_rows, pos
    else:
        k_all = jnp.concatenate([past_k, k_rows], axis=1)
        v_all = jnp.concatenate([past_v, v_rows], axis=1)
        k_pos = jnp.arange(k_all.shape[1], dtype=jnp.int32)
    lam_init = 0.8 - 0.6 * math.exp(-0.3 * layer)
    lam = (jnp.exp(jnp.sum(lq1.astype(f32) * lk1.astype(f32)))
           - jnp.exp(jnp.sum(lq2.astype(f32) * lk2.astype(f32))) + lam_init)
    o = diff_attention(aq.reshape(B, T, ATT_HEADS, 2, ATT_DK),
                       k_all.reshape(B, k_all.shape[1], ATT_HEADS, 2, ATT_DK),
                       v_all, pos, k_pos, lam)
    y_att = (rms_norm(o, att_subln) * (1.0 - lam_init)).reshape(B, T, ATT_HEADS * ATT_DV)

    hq = jax.nn.silu(hq).reshape(B, T, HG_HEADS, HG_DK) * (HG_DK ** -0.5)
    z = hf.astype(f32).reshape(B, T, HG_HEADS, HG_DK)
    lbh = lb.reshape(HG_HEADS, HG_DK)
    f_gate = lbh + (1.0 - lbh) * jax.nn.sigmoid(z)
    logf = jnp.log(f_gate)
    kk = 1.0 - f_gate
    o_h, hg_new = hgrn2_scan(hq, kk, logf, hi.reshape(B, T, HG_HEADS, HG_DV), hg_state)
    o_h = rms_norm(o_h.astype(x.dtype), hg_outn) * jax.nn.silu(hg.reshape(B, T, HG_HEADS, HG_DV))
    y_hg = o_h.reshape(B, T, HG_HEADS * HG_DV)

    g_pool, g_att, g_hg = jnp.split(jax.nn.sigmoid(gates), N_BRANCH, axis=-1)
    merged = (g_pool * (y_pool @ w_up_pool) + g_att * (y_att @ w_up_att)
              + g_hg * (y_hg @ w_up_hgrn))
    return x + merged @ w_out, k_rows, v_rows, pool_new, hg_new


def setup_inputs(seed: int = 0) -> dict:
    key = jax.random.key(seed)
    ks = iter(jax.random.split(key, 48))
    L = DEPTH

    def nrm(shape, scale):
        return jax.random.normal(next(ks), shape, jnp.float32) * scale

    def gain(shape):
        return 1.0 + 0.02 * jax.random.normal(next(ks), shape, jnp.float32)

    return {
        'x_prompt': nrm((BATCH, SEQ, D_MODEL), 1.0),
        'x_sample': nrm((DEC_BATCH, DEC_SEQ, D_MODEL), 1.0),
        'cache_attn_k': nrm((L, DEC_BATCH, PAST_LEN, ATT_HEADS, 2 * ATT_DK), 1.0),
        'cache_attn_v': nrm((L, DEC_BATCH, PAST_LEN, ATT_HEADS, ATT_DV), 1.0),
        'cache_mem_k': nrm((L, DEC_BATCH, N_MEM, X_HEADS, X_HD), 1.0),
        'cache_mem_v': nrm((L, DEC_BATCH, N_MEM, X_HEADS, X_HD), 1.0),
        'state_pool': nrm((L, DEC_BATCH, POOL_STATE, POOL_WIDTH), 1.0),
        'state_hgrn': nrm((L, DEC_BATCH, HG_HEADS, HG_DK, HG_DV), 0.5),
        'mem_prompt': nrm((BATCH, N_MEM, D_MODEL), 1.0),
        'norm_ffn1': gain((L, D_MODEL)),
        'w_ffn1_in': nrm((L, D_MODEL, 2 * D_FF), D_MODEL ** -0.5),
        'w_ffn1_out': nrm((L, D_FF, D_MODEL), D_FF ** -0.5),
        'norm_mix': gain((L, D_MODEL)),
        'w_in': nrm((L, D_MODEL, IN_COLS), D_MODEL ** -0.5),
        'pool_w': nrm((L, POOL_GROUPS, POOL_GROUP_W, POOL_GROUP_W), POOL_GROUP_W ** -0.5),
        'pool_scale': gain((L, POOL_WIDTH)),
        'att_q_norm': gain((L, ATT_DK)),
        'att_k_norm': gain((L, ATT_DK)),
        'lambda_q1': nrm((L, ATT_DK), 0.1),
        'lambda_k1': nrm((L, ATT_DK), 0.1),
        'lambda_q2': nrm((L, ATT_DK), 0.1),
        'lambda_k2': nrm((L, ATT_DK), 0.1),
        'att_subln': gain((L, ATT_DV)),
        'hgrn_lower': nrm((L, HG_HEADS * HG_DK), 1.0),
        'hgrn_out_norm': gain((L, HG_DV)),
        'w_up_pool': nrm((L, POOL_WIDTH, D_MODEL), POOL_WIDTH ** -0.5),
        'w_up_att': nrm((L, ATT_HEADS * ATT_DV, D_MODEL), (ATT_HEADS * ATT_DV) ** -0.5),
        'w_up_hgrn': nrm((L, HG_HEADS * HG_DV, D_MODEL), (HG_HEADS * HG_DV) ** -0.5),
        'w_out': nrm((L, D_MODEL, D_MODEL), D_MODEL ** -0.5),
        'norm_cross': gain((L, D_MODEL)),
        'norm_mem': gain((L, D_MODEL)),
        'w_cq': nrm((L, D_MODEL, D_MODEL), D_MODEL ** -0.5),
        'w_ckv': nrm((L, D_MODEL, 2 * D_MODEL), D_MODEL ** -0.5),
        'cross_q_norm': gain((L, X_HD)),
        'cross_k_norm': gain((L, X_HD)),
        'w_co': nrm((L, D_MODEL, D_MODEL), D_MODEL ** -0.5),
        'norm_ffn2': gain((L, D_MODEL)),
        'w_ffn2_in': nrm((L, D_MODEL, 2 * D_FF), D_MODEL ** -0.5),
        'w_ffn2_out': nrm((L, D_FF, D_MODEL), D_FF ** -0.5),
    }


def reference(x_prompt, x_sample, cache_attn_k, cache_attn_v, cache_mem_k, cache_mem_v,
              state_pool, state_hgrn, mem_prompt,
              norm_ffn1, w_ffn1_in, w_ffn1_out,
              norm_mix, w_in, pool_w, pool_scale, att_q_norm, att_k_norm,
              lambda_q1, lambda_k1, lambda_q2, lambda_k2, att_subln,
              hgrn_lower, hgrn_out_norm, w_up_pool, w_up_att, w_up_hgrn, w_out,
              norm_cross, norm_mem, w_cq, w_ckv, cross_q_norm, cross_k_norm, w_co,
              norm_ffn2, w_ffn2_in, w_ffn2_out):
    f32 = jnp.float32
    lp = jax.nn.softmax(hgrn_lower.astype(f32), axis=0)
    lbs = jnp.cumsum(lp, axis=0) - lp[0:1]

    def layer(x, l, p0, past_k, past_v, pool_prev, hg_state, mk, mv):
        x = ffn_half(x, norm_ffn1[l], w_ffn1_in[l], w_ffn1_out[l])
        x, k_rows, v_rows, pool_new, hg_new = token_mixers(
            x, l, p0, past_k, past_v, pool_prev, hg_state, lbs[l],
            norm_mix[l], w_in[l], pool_w[l], pool_scale[l], att_q_norm[l], att_k_norm[l],
            lambda_q1[l], lambda_k1[l], lambda_q2[l], lambda_k2[l], att_subln[l],
            hgrn_out_norm[l], w_up_pool[l], w_up_att[l], w_up_hgrn[l], w_out[l])
        x = cross_attend(x, mk, mv, norm_cross[l], w_cq[l], cross_q_norm[l], w_co[l])
        x = ffn_half(x, norm_ffn2[l], w_ffn2_in[l], w_ffn2_out[l])
        return x, k_rows, v_rows, pool_new, hg_new

    bp = x_prompt.shape[0]
    y_prompt = x_prompt
    pk, pv, pmk, pmv, ppool, phg = [], [], [], [], [], []
    for l in range(DEPTH):
        mk, mv = memory_kv(mem_prompt, norm_mem[l], w_ckv[l], cross_k_norm[l])
        pool0 = jnp.zeros((bp, POOL_STATE, POOL_WIDTH), x_prompt.dtype)
        hg0 = jnp.zeros((bp, HG_HEADS, HG_DK, HG_DV), f32)
        y_prompt, kr, vr, pn, hn = layer(y_prompt, l, 0, None, None, pool0, hg0, mk, mv)
        pk.append(kr); pv.append(vr); pmk.append(mk); pmv.append(mv)
        ppool.append(pn); phg.append(hn)

    p0 = cache_attn_k.shape[2]
    y_sample = x_sample
    sk, sv, spool, shg = [], [], [], []
    for l in range(DEPTH):
        y_sample, kr, vr, pn, hn = layer(y_sample, l, p0, cache_attn_k[l], cache_attn_v[l],
                                         state_pool[l], state_hgrn[l],
                                         cache_mem_k[l], cache_mem_v[l])
        sk.append(kr); sv.append(vr); spool.append(pn); shg.append(hn)

    return (y_prompt, y_sample,
            jnp.stack(pk), jnp.stack(pv), jnp.stack(pmk), jnp.stack(pmv),
            jnp.stack(ppool), jnp.stack(phg),
            jnp.stack(sk), jnp.stack(sv), jnp.stack(spool), jnp.stack(shg))
```

```python
import functools
import math

import jax
import jax.numpy as jnp
from jax import lax
from jax.experimental import pallas as pl
from jax.experimental.pallas import tpu as pltpu

F32 = jnp.float32
BF16 = jnp.bfloat16

D_MODEL = 1024
CHUNK = 64
EPS = 1e-6
MASK_VALUE = -1e30
POOL_WINDOWS = (2, 4, 8, 16)
POOL_STATE = 15
POOL_HIST = 16
ATT_HEADS = 4
ATT_DK = 64
ATT_DV = 128
ROT_DIMS = 16
ROPE_THETA = 500000.0
HG_HEADS = 4
HG_DK = 64
HG_DV = 128
X_HEADS = 4
X_HD = 256
D_FF = 2816
SEG = 512
N_SEG_A = 7
N_SEG_G = 6
LANES = 128
VMEM_LIMIT = 56 * 1024 * 1024


def _cparams(sem):
    return pltpu.CompilerParams(dimension_semantics=sem, vmem_limit_bytes=VMEM_LIMIT)


def _rms(x, g):
    ms = jnp.mean(x * x, axis=-1, keepdims=True)
    return x * lax.rsqrt(ms + EPS) * g


def _silu(x):
    return x * jax.nn.sigmoid(x)


def _dot(a, b):
    return jnp.dot(a, b, preferred_element_type=F32)


def _dot_nt(a, b):
    return lax.dot_general(a, b, (((1,), (1,)), ((), ())), preferred_element_type=F32)


def _dot_tn(a, b):
    return lax.dot_general(a, b, (((0,), (0,)), ((), ())), preferred_element_type=F32)


def _shr(x, pow2):
    return lax.shift_right_logical(x, jnp.int32(int(math.log2(pow2))))


def _ffn_kernel(x_ref, g_ref, wa_ref, wb_ref, wo_ref, o_ref, h_sc, acc_sc):
    j = pl.program_id(1)

    @pl.when(j == 0)
    def _():
        h_sc[...] = _rms(x_ref[...], g_ref[...]).astype(BF16)
        acc_sc[...] = jnp.zeros_like(acc_sc)

    h = h_sc[...]
    a = _dot(h, wa_ref[...])
    b = _dot(h, wb_ref[...])
    acc_sc[...] += _dot((_silu(a) * b).astype(BF16), wo_ref[...])

    @pl.when(j == pl.num_programs(1) - 1)
    def _():
        o_ref[...] = x_ref[...] + 0.5 * acc_sc[...]


def _ffn_half(x, g, w_i, w_o, *, tm, tf):
    n = x.shape[0]
    nj = D_FF // tf
    return pl.pallas_call(
        _ffn_kernel,
        grid=(n // tm, nj),
        in_specs=[
            pl.BlockSpec((tm, D_MODEL), lambda i, j: (i, 0)),
            pl.BlockSpec((1, D_MODEL), lambda i, j: (0, 0)),
            pl.BlockSpec((D_MODEL, tf), lambda i, j: (0, j)),
            pl.BlockSpec((D_MODEL, tf), lambda i, j: (0, j + nj)),
            pl.BlockSpec((tf, D_MODEL), lambda i, j: (j, 0)),
        ],
        out_specs=pl.BlockSpec((tm, D_MODEL), lambda i, j: (i, 0)),
        out_shape=jax.ShapeDtypeStruct((n, D_MODEL), F32),
        scratch_shapes=[pltpu.VMEM((tm, D_MODEL), BF16), pltpu.VMEM((tm, D_MODEL), F32)],
        compiler_params=_cparams(("parallel", "arbitrary")),
        name="ffn_half",
    )(x, g.reshape(1, D_MODEL), w_i, w_i, w_o)


def _inproj_kernel(x_ref, g_ref, w_ref, gm_ref, qn_ref, kn_ref, cos_ref, sa_ref, sb_ref,
                   pa_ref, pg_ref, qkv_ref, h_sc):
    j = pl.program_id(1)

    @pl.when(j == 0)
    def _():
        h_sc[...] = _rms(x_ref[...], g_ref[...]).astype(BF16)

    y = _dot(h_sc[...], w_ref[...])

    @pl.when(jnp.logical_or(j == 1, j == 2))
    def _():
        sq = y * y
        hi = sq.astype(BF16)
        lo = (sq - hi.astype(F32)).astype(BF16)
        ms = _dot(hi, gm_ref[...]) + _dot(lo, gm_ref[...])
        gain = jnp.where(j == 1, qn_ref[...], kn_ref[...])
        yn = y * lax.rsqrt(ms + EPS) * gain
        rep = SEG // LANES
        cos = jnp.concatenate([cos_ref[...]] * rep, axis=1)
        sa = jnp.concatenate([sa_ref[...]] * rep, axis=1)
        sb = jnp.concatenate([sb_ref[...]] * rep, axis=1)
        half = ROT_DIMS // 2
        rot = (yn * cos + pltpu.roll(yn, half, 1) * sa + pltpu.roll(yn, SEG - half, 1) * sb)
        pa_ref[...] = rot
        scale = jnp.where(j == 1, ATT_DK ** -0.5, 1.0).astype(F32)
        qkv_ref[...] = (rot * scale).astype(BF16)

    @pl.when(j == 3)
    def _():
        pa_ref[...] = y
        qkv_ref[...] = y.astype(BF16)

    @pl.when(jnp.logical_or(j == 0, jnp.logical_and(j > 3, j < N_SEG_A)))
    def _():
        pa_ref[...] = y

    @pl.when(j >= N_SEG_A)
    def _():
        pg_ref[...] = y


def _in_proj(x, g, w, gm, qn, kn, cos, sa, sb, *, tm):
    n = x.shape[0]
    tab_blocks = cos.shape[0] // tm
    nseg = N_SEG_A + N_SEG_G
    tab_spec = pl.BlockSpec((tm, LANES), lambda i, j: (i % tab_blocks, 0))
    vec_spec = pl.BlockSpec((1, SEG), lambda i, j: (0, 0))
    return pl.pallas_call(
        _inproj_kernel,
        grid=(n // tm, nseg),
        in_specs=[
            pl.BlockSpec((tm, D_MODEL), lambda i, j: (i, 0)),
            pl.BlockSpec((1, D_MODEL), lambda i, j: (0, 0)),
            pl.BlockSpec((D_MODEL, SEG), lambda i, j: (0, j)),
            pl.BlockSpec((SEG, SEG), lambda i, j: (0, 0)),
            vec_spec, vec_spec, tab_spec, tab_spec, tab_spec,
        ],
        out_specs=[
            pl.BlockSpec((tm, SEG), lambda i, j: (i, jnp.minimum(j, N_SEG_A - 1))),
            pl.BlockSpec((tm, SEG), lambda i, j: (i, jnp.maximum(j - N_SEG_A, 0))),
            pl.BlockSpec((tm, SEG), lambda i, j: (i, jnp.clip(j - 1, 0, 2))),
        ],
        out_shape=[
            jax.ShapeDtypeStruct((n, N_SEG_A * SEG), F32),
            jax.ShapeDtypeStruct((n, N_SEG_G * SEG), F32),
            jax.ShapeDtypeStruct((n, 3 * SEG), BF16),
        ],
        scratch_shapes=[pltpu.VMEM((tm, D_MODEL), BF16)],
        compiler_params=_cparams(("parallel", "arbitrary")),
        name="in_proj",
    )(x, g.reshape(1, D_MODEL), w, gm, qn, kn, cos, sa, sb)


def _rope_tables(p0, t):
    half = ROT_DIMS // 2
    inv = ROPE_THETA ** (-jnp.arange(half, dtype=F32) / half)
    pos = p0 + jnp.arange(t, dtype=jnp.int32)
    ang = pos.astype(F32)[:, None] * inv[None, :]
    cos, sin = jnp.cos(ang), jnp.sin(ang)
    ones = jnp.ones((t, ATT_DK - ROT_DIMS), F32)
    zeros = jnp.zeros((t, ATT_DK - ROT_DIMS), F32)
    zh = jnp.zeros((t, half), F32)
    c64 = jnp.concatenate([cos, cos, ones], axis=1)
    sa64 = jnp.concatenate([zh, sin, zeros], axis=1)
    sb64 = jnp.concatenate([-sin, zh, zeros], axis=1)
    rep = LANES // ATT_DK
    return tuple(jnp.concatenate([a] * rep, axis=1) for a in (c64, sa64, sb64))


def _stack_maps(q):
    lane = lax.broadcasted_iota(jnp.int32, q.shape, 1)
    zero = jnp.zeros_like(q)
    return jnp.concatenate([jnp.where(lane < ATT_DK, q, zero), jnp.where(lane >= ATT_DK, q, zero)], axis=0)


def _attn_finish(acc, l, lam, gsub, post_scale, tq):
    o = acc[:tq] / l[:tq] - lam * (acc[tq:] / l[tq:])
    return _rms(o, gsub) * post_scale


def _attn_prompt_kernel(lam_ref, gsub_ref, q_ref, k_ref, v_ref, o_ref, acc_sc, m_sc, l_sc, *, tq, post_scale):
    i = pl.program_id(2)
    qbd = _stack_maps(q_ref[...])
    m_sc[...] = jnp.full_like(m_sc, MASK_VALUE)
    l_sc[...] = jnp.zeros_like(l_sc)
    acc_sc[...] = jnp.zeros_like(acc_sc)

    def step(j, masked):
        off = pl.multiple_of(j * tq, tq)
        kj = k_ref[pl.ds(off, tq), :]
        vj = v_ref[pl.ds(off, tq), :]
        s = _dot_nt(qbd, kj)
        if masked:
            qi = lax.broadcasted_iota(jnp.int32, s.shape, 0) & (tq - 1)
            ki = lax.broadcasted_iota(jnp.int32, s.shape, 1)
            s = jnp.where(_shr(ki, CHUNK) <= _shr(qi, CHUNK), s, MASK_VALUE)
        m_prev = m_sc[...]
        m_new = jnp.maximum(m_prev, jnp.max(s, axis=-1, keepdims=True))
        alpha = jnp.exp(m_prev - m_new)
        p = jnp.exp(s - m_new)
        l_sc[...] = alpha * l_sc[...] + jnp.sum(p, axis=-1, keepdims=True)
        acc_sc[...] = alpha * acc_sc[...] + _dot(p.astype(BF16), vj)
        m_sc[...] = m_new

    def body(j, carry):
        step(j, False)
        return carry

    lax.fori_loop(0, i, body, 0)
    step(i, True)
    o_ref[...] = _attn_finish(acc_sc[...], l_sc[...], lam_ref[...], gsub_ref[...], post_scale, tq)


def _attn_prompt(qkv, lam, gsub, post_scale, *, tq):
    b, t, _ = qkv.shape
    kern = functools.partial(_attn_prompt_kernel, tq=tq, post_scale=post_scale)
    vec_spec = pl.BlockSpec((1, ATT_DV), lambda b_, h, i: (0, 0))
    return pl.pallas_call(
        kern,
        grid=(b, ATT_HEADS, t // tq),
        in_specs=[
            vec_spec, vec_spec,
            pl.BlockSpec((None, tq, ATT_DV), lambda b_, h, i: (b_, i, h)),
            pl.BlockSpec((None, t, ATT_DV), lambda b_, h, i: (b_, 0, ATT_HEADS + h)),
            pl.BlockSpec((None, t, ATT_DV), lambda b_, h, i: (b_, 0, 2 * ATT_HEADS + h)),
        ],
        out_specs=pl.BlockSpec((None, tq, ATT_DV), lambda b_, h, i: (b_, i, h)),
        out_shape=jax.ShapeDtypeStruct((b, t, ATT_HEADS * ATT_DV), F32),
        scratch_shapes=[pltpu.VMEM((2 * tq, ATT_DV), F32), pltpu.VMEM((2 * tq, 1), F32),
                        pltpu.VMEM((2 * tq, 1), F32)],
        compiler_params=_cparams(("parallel", "parallel", "arbitrary")),
        name="diff_attn_prompt",
    )(lam, gsub, qkv, qkv, qkv)


def _attn_sample_kernel(lam_ref, gsub_ref, q_ref, kn_ref, vn_ref, kp_ref, vp_ref, o_ref, *, tq, post_scale):
    qbd = _stack_maps(q_ref[...])
    sp = _dot_nt(qbd, kp_ref[...])
    sn = _dot_nt(qbd, kn_ref[...])
    m = jnp.maximum(jnp.max(sp, axis=-1, keepdims=True), jnp.max(sn, axis=-1, keepdims=True))
    pp = jnp.exp(sp - m)
    pn = jnp.exp(sn - m)
    l = jnp.sum(pp, axis=-1, keepdims=True) + jnp.sum(pn, axis=-1, keepdims=True)
    acc = _dot(pp.astype(BF16), vp_ref[...]) + _dot(pn.astype(BF16), vn_ref[...])
    o_ref[...] = _attn_finish(acc, l, lam_ref[...], gsub_ref[...], post_scale, tq)


def _attn_sample(qkv, kp, vp, lam, gsub, post_scale):
    b, t, _ = qkv.shape
    tp = kp.shape[1]
    kern = functools.partial(_attn_sample_kernel, tq=t, post_scale=post_scale)
    vec_spec = pl.BlockSpec((1, ATT_DV), lambda b_, h: (0, 0))
    return pl.pallas_call(
        kern,
        grid=(b, ATT_HEADS),
        in_specs=[
            vec_spec, vec_spec,
            pl.BlockSpec((None, t, ATT_DV), lambda b_, h: (b_, 0, h)),
            pl.BlockSpec((None, t, ATT_DV), lambda b_, h: (b_, 0, ATT_HEADS + h)),
            pl.BlockSpec((None, t, ATT_DV), lambda b_, h: (b_, 0, 2 * ATT_HEADS + h)),
            pl.BlockSpec((None, tp, ATT_DV), lambda b_, h: (b_, 0, h)),
            pl.BlockSpec((None, tp, ATT_DV), lambda b_, h: (b_, 0, h)),
        ],
        out_specs=pl.BlockSpec((None, t, ATT_DV), lambda b_, h: (b_, 0, h)),
        out_shape=jax.ShapeDtypeStruct((b, t, ATT_HEADS * ATT_DV), F32),
        compiler_params=_cparams(("parallel", "parallel")),
        name="diff_attn_sample",
    )(lam, gsub, qkv, qkv, qkv, kp, vp)


HG_W = HG_HEADS * HG_DK
HG_V = HG_HEADS * HG_DV
B_PAD = 8


def _level_reference(b_sc, w):
    if w >= 4:
        pieces = []
        for p in range(CHUNK // (2 * w)):
            row = b_sc[pl.ds(B_PAD + p * 2 * w + w, 1), :]
            pieces.append(jnp.broadcast_to(row, (2 * w, HG_W)))
        return jnp.concatenate(pieces, axis=0)
    t = lax.broadcasted_iota(jnp.int32, (CHUNK, HG_W), 0)
    phase = t & (2 * w - 1)
    r = None
    for ph in range(2 * w):
        shifted = b_sc[pl.ds(B_PAD + w - ph, CHUNK), :]
        r = shifted if r is None else jnp.where(phase == ph, shifted, r)
    return r


def _hgrn_chunk(hq, hf, hi, lb, st_sc, b_sc, masks):
    tril, bd_k, bd_v, bd_s = masks
    f = lb + (1.0 - lb) * jax.nn.sigmoid(hf)
    g = jnp.log(f)
    kk = 1.0 - f
    q = _silu(hq) * (HG_DK ** -0.5)
    g0 = g.astype(BF16)
    r1 = g - g0.astype(F32)
    g1 = r1.astype(BF16)
    g2 = (r1 - g1.astype(F32)).astype(BF16)
    b = _dot(tril, g0) + _dot(tril, g1) + _dot(tril, g2)
    b_sc[pl.ds(B_PAD, CHUNK), :] = b

    t_idx = lax.broadcasted_iota(jnp.int32, (CHUNK, HG_W), 0)
    s_idx = lax.broadcasted_iota(jnp.int32, (CHUNK, HG_W), 1) & (CHUNK - 1)
    zero = jnp.zeros((CHUNK, HG_W), F32)

    def block_diag_k(x):
        return jnp.where(bd_k, jnp.concatenate([x] * HG_HEADS, axis=0), jnp.zeros((), F32)).astype(BF16)

    a = jnp.where(t_idx == s_idx, _dot_nt(q.astype(BF16), block_diag_k(kk)), zero)
    w = CHUNK // 2
    while w >= 1:
        r = _level_reference(b_sc, w)
        upper = (t_idx & w) != 0
        e = jnp.exp(jnp.where(upper, b - r, r - b))
        ql = jnp.where(upper, q * e, zero)
        kl = jnp.where(upper, zero, kk * e)
        same_pair = _shr(t_idx, 2 * w) == _shr(s_idx, 2 * w)
        a = a + jnp.where(same_pair, _dot_nt(ql.astype(BF16), block_diag_k(kl)), zero)
        w //= 2

    b_last = jnp.broadcast_to(b_sc[pl.ds(B_PAD + CHUNK - 1, 1), :], (CHUNK, HG_W))
    qb = q * jnp.exp(b)
    kd = kk * jnp.exp(b_last - b)
    v16 = hi.astype(BF16)
    vbd = jnp.where(bd_v, jnp.concatenate([hi] * HG_HEADS, axis=0), jnp.zeros((), F32)).astype(BF16)
    st = st_sc[...]
    o = _dot(a.astype(BF16), vbd) + _dot_nt(qb.astype(BF16), st.astype(BF16))
    upd = _dot_tn(v16, kd.astype(BF16))
    decay = jnp.exp(b_sc[pl.ds(B_PAD + CHUNK - 1, 1), :])
    st_sc[...] = st * decay + jnp.where(bd_s, upd, jnp.zeros((), F32))
    return o


def _mix_kernel(x_ref, u_ref, hqf_ref, hi_ref, hg_ref, g_ref, ya_ref, st0_ref, pool0_ref,
                lb_ref, pw_ref, ps_ref, hgn_ref, wup_ref, wua_ref, wuh_ref, wo_ref,
                xo_ref, stn_ref, pooln_ref,
                st_sc, b_sc, ext_sc, yh_sc, *, tm, p0):
    it = pl.program_id(1)

    @pl.when(it == 0)
    def _():
        st_sc[...] = st0_ref[...]
        ext_sc[pl.ds(0, POOL_HIST), :] = pool0_ref[...]
        b_sc[...] = jnp.zeros_like(b_sc)

    u = u_ref[...]
    ext_sc[pl.ds(POOL_HIST, tm), :] = u
    row = lax.broadcasted_iota(jnp.int32, (tm, LANES), 0)
    seen = (p0 + 1 + it * tm + row).astype(F32)
    cols = []
    for gi, w in enumerate(POOL_WINDOWS):
        sl = pl.ds(gi * LANES, LANES)
        win = u[:, gi * LANES:(gi + 1) * LANES]
        for d in range(1, w):
            win = win + ext_sc[pl.ds(POOL_HIST - d, tm), sl]
        dlt = win / jnp.minimum(seen, float(w)) - u[:, gi * LANES:(gi + 1) * LANES]
        cols.append(_dot(dlt.astype(BF16), pw_ref[gi]))
    y_pool = jnp.concatenate(cols, axis=1) * ps_ref[...]
    ext_sc[pl.ds(0, POOL_HIST), :] = ext_sc[pl.ds(tm, POOL_HIST), :]

    tri_r = lax.broadcasted_iota(jnp.int32, (CHUNK, CHUNK), 0)
    tri_c = lax.broadcasted_iota(jnp.int32, (CHUNK, CHUNK), 1)
    tril = (tri_c <= tri_r).astype(BF16)
    rk = lax.broadcasted_iota(jnp.int32, (HG_W, HG_W), 0)
    ck = lax.broadcasted_iota(jnp.int32, (HG_W, HG_W), 1)
    bd_k = _shr(rk, CHUNK) == _shr(ck, HG_DK)
    rv = lax.broadcasted_iota(jnp.int32, (HG_W, HG_V), 0)
    cv = lax.broadcasted_iota(jnp.int32, (HG_W, HG_V), 1)
    bd_v = _shr(rv, CHUNK) == _shr(cv, HG_DV)
    rs = lax.broadcasted_iota(jnp.int32, (HG_V, HG_W), 0)
    cs = lax.broadcasted_iota(jnp.int32, (HG_V, HG_W), 1)
    bd_s = _shr(rs, HG_DV) == _shr(cs, HG_DK)
    masks = (tril, bd_k, bd_v, bd_s)
    lb = lb_ref[...]
    for c in range(tm // CHUNK):
        rows = pl.ds(c * CHUNK, CHUNK)
        o = _hgrn_chunk(hqf_ref[rows, pl.ds(0, HG_W)], hqf_ref[rows, pl.ds(HG_W, HG_W)],
                        hi_ref[rows, :], lb, st_sc, b_sc, masks)
        yh_sc[rows, :] = o
    oh = yh_sc[...]
    hg = hg_ref[...]
    heads = []
    for h in range(HG_HEADS):
        sl = slice(h * HG_DV, (h + 1) * HG_DV)
        heads.append(_rms(oh[:, sl], hgn_ref[...]) * _silu(hg[:, sl]))
    y_hg = jnp.concatenate(heads, axis=1)

    gts = g_ref[...]
    merged = (jax.nn.sigmoid(gts[:, :D_MODEL]) * _dot(y_pool.astype(BF16), wup_ref[...])
              + jax.nn.sigmoid(gts[:, D_MODEL:2 * D_MODEL]) * _dot(ya_ref[...].astype(BF16), wua_ref[...])
              + jax.nn.sigmoid(gts[:, 2 * D_MODEL:]) * _dot(y_hg.astype(BF16), wuh_ref[...]))
    xo_ref[...] = x_ref[...] + _dot(merged.astype(BF16), wo_ref[...])

    @pl.when(it == pl.num_programs(1) - 1)
    def _():
        stn_ref[...] = st_sc[...]
        pooln_ref[...] = ext_sc[pl.ds(0, POOL_HIST), :]


def _mix_out(x, proj_a, gates, y_att, st0, pool0, lb, pool_w, pool_scale, hg_outn,
             w_up_pool, w_up_att, w_up_hgrn, w_out, *, tm, p0):
    b, t, _ = x.shape
    kern = functools.partial(_mix_kernel, tm=tm, p0=p0)

    def rows(width, col):
        return pl.BlockSpec((None, tm, width), lambda b_, i: (b_, i, col))

    def const(shape):
        return pl.BlockSpec(shape, lambda b_, i: (0,) * len(shape))

    def per_batch(shape):
        return pl.BlockSpec((None,) + shape, lambda b_, i: (b_, 0, 0))

    return pl.pallas_call(
        kern,
        grid=(b, t // tm),
        in_specs=[
            rows(D_MODEL, 0),
            rows(SEG, 0),
            rows(SEG, 4),
            rows(SEG, 5),
            rows(SEG, 6),
            rows(3 * D_MODEL, 0),
            rows(SEG, 0),
            per_batch((HG_V, HG_W)),
            per_batch((POOL_HIST, SEG)),
            const((1, HG_W)), const((4, LANES, LANES)), const((1, SEG)), const((1, HG_DV)),
            const((SEG, D_MODEL)), const((SEG, D_MODEL)), const((SEG, D_MODEL)), const((D_MODEL, D_MODEL)),
        ],
        out_specs=[
            rows(D_MODEL, 0),
            per_batch((HG_V, HG_W)),
            per_batch((POOL_HIST, SEG)),
        ],
        out_shape=[
            jax.ShapeDtypeStruct((b, t, D_MODEL), F32),
            jax.ShapeDtypeStruct((b, HG_V, HG_W), F32),
            jax.ShapeDtypeStruct((b, POOL_HIST, SEG), F32),
        ],
        scratch_shapes=[
            pltpu.VMEM((HG_V, HG_W), F32),
            pltpu.VMEM((CHUNK + 2 * B_PAD, HG_W), F32),
            pltpu.VMEM((POOL_HIST + tm, SEG), F32),
            pltpu.VMEM((tm, HG_V), F32),
        ],
        compiler_params=_cparams(("parallel", "arbitrary")),
        name="mix_out",
    )(x, proj_a, proj_a, proj_a, proj_a, gates, y_att, st0, pool0,
      lb, pool_w, pool_scale, hg_outn, w_up_pool, w_up_att, w_up_hgrn, w_out)


def _state_to_block_diag(s):
    b = s.shape[0]
    st = jnp.swapaxes(s, 2, 3)
    eye = jnp.eye(HG_HEADS, dtype=s.dtype)
    return jnp.einsum('bhed,hg->bhegd', st, eye).reshape(b, HG_V, HG_W)


def _block_diag_to_state(st):
    b = st.shape[0]
    s5 = st.reshape(b, HG_HEADS, HG_DV, HG_HEADS, HG_DK)
    diag = jnp.stack([s5[:, h, :, h, :] for h in range(HG_HEADS)], axis=1)
    return jnp.swapaxes(diag, 2, 3)


def _head_rms(x, gain):
    return jnp.concatenate(
        [_rms(x[:, h * X_HD:(h + 1) * X_HD], gain) for h in range(X_HEADS)], axis=1)


def _memkv_kernel(m_ref, g_ref, w_ref, kn_ref, k_ref, v_ref, k16_ref, v16_ref):
    h = _rms(m_ref[...], g_ref[...]).astype(BF16)
    kv = _dot(h, w_ref[...])
    mk = _head_rms(kv[:, :D_MODEL], kn_ref[...])
    mv = kv[:, D_MODEL:]
    k_ref[...] = mk
    v_ref[...] = mv
    k16_ref[...] = mk.astype(BF16)
    v16_ref[...] = mv.astype(BF16)


def _memory_kv(mem, g, w_ckv, kn):
    b, n, _ = mem.shape
    blk = pl.BlockSpec((None, n, D_MODEL), lambda b_: (b_, 0, 0))
    return pl.pallas_call(
        _memkv_kernel,
        grid=(b,),
        in_specs=[
            blk,
            pl.BlockSpec((1, D_MODEL), lambda b_: (0, 0)),
            pl.BlockSpec((D_MODEL, 2 * D_MODEL), lambda b_: (0, 0)),
            pl.BlockSpec((1, X_HD), lambda b_: (0, 0)),
        ],
        out_specs=[blk, blk, blk, blk],
        out_shape=[jax.ShapeDtypeStruct((b, n, D_MODEL), F32)] * 2
        + [jax.ShapeDtypeStruct((b, n, D_MODEL), BF16)] * 2,
        compiler_params=_cparams(("parallel",)),
        name="memory_kv",
    )(mem, g.reshape(1, D_MODEL), w_ckv, kn.reshape(1, X_HD))


def _cross_kernel(x_ref, g_ref, wq_ref, qn_ref, mk_ref, mv_ref, wo_ref, o_ref):
    x = x_ref[...]
    q = _head_rms(_dot(_rms(x, g_ref[...]).astype(BF16), wq_ref[...]), qn_ref[...])
    q = (q * (X_HD ** -0.5)).astype(BF16)
    outs = []
    for h in range(X_HEADS):
        sl = slice(h * X_HD, (h + 1) * X_HD)
        s = _dot_nt(q[:, sl], mk_ref[:, sl])
        p = jnp.exp(s - jnp.max(s, axis=-1, keepdims=True))
        p = p / jnp.sum(p, axis=-1, keepdims=True)
        outs.append(_dot(p.astype(BF16), mv_ref[:, sl]))
    o = jnp.concatenate(outs, axis=1)
    o_ref[...] = x + _dot(o.astype(BF16), wo_ref[...])


def _cross_attend(x, mk, mv, g, w_cq, qn, w_co, *, tm):
    b, t, _ = x.shape
    n = mk.shape[1]
    rows = pl.BlockSpec((None, tm, D_MODEL), lambda b_, i: (b_, i, 0))
    mem = pl.BlockSpec((None, n, D_MODEL), lambda b_, i: (b_, 0, 0))
    wsq = pl.BlockSpec((D_MODEL, D_MODEL), lambda b_, i: (0, 0))
    return pl.pallas_call(
        _cross_kernel,
        grid=(b, t // tm),
        in_specs=[
            rows,
            pl.BlockSpec((1, D_MODEL), lambda b_, i: (0, 0)),
            wsq,
            pl.BlockSpec((1, X_HD), lambda b_, i: (0, 0)),
            mem, mem, wsq,
        ],
        out_specs=rows,
        out_shape=jax.ShapeDtypeStruct((b, t, D_MODEL), F32),
        compiler_params=_cparams(("parallel", "parallel")),
        name="cross_attn",
    )(x, g.reshape(1, D_MODEL), w_cq, qn.reshape(1, X_HD), mk, mv, w_co)


def _layer(x, l, p0, past_kv, pool_prev, hg_state, lb, mk16, mv16, w, *, tiles):
    b, t, _ = x.shape
    n = b * t
    x2 = _ffn_half(x.reshape(n, D_MODEL), w['norm_ffn1'], w['w_ffn1_in'], w['w_ffn1_out'],
                   tm=tiles['ffn'], tf=tiles['tf'])

    cos, sa, sb = _rope_tables(p0, t)
    if t < tiles['proj']:
        rep = tiles['proj'] // t
        cos, sa, sb = (jnp.concatenate([a] * rep, axis=0) for a in (cos, sa, sb))
    proj_a, gates, qkv = _in_proj(x2, w['norm_mix'], w['w_in'], w['group_mean'], w['att_qn'], w['att_kn'],
                                  cos, sa, sb, tm=tiles['proj'])
    k_rows = proj_a[:, 2 * SEG:3 * SEG].reshape(b, t, ATT_HEADS, 2 * ATT_DK)
    v_rows = proj_a[:, 3 * SEG:4 * SEG].reshape(b, t, ATT_HEADS, ATT_DV)

    lam_init = 0.8 - 0.6 * math.exp(-0.3 * l)
    lam = (jnp.exp(jnp.sum(w['lq1'] * w['lk1'])) - jnp.exp(jnp.sum(w['lq2'] * w['lk2'])) + lam_init)
    lam = jnp.full((1, ATT_DV), lam, F32)
    gsub = w['att_subln'].reshape(1, ATT_DV)
    qkv3 = qkv.reshape(b, t, 3 * SEG)
    if past_kv is None:
        y_att = _attn_prompt(qkv3, lam, gsub, 1.0 - lam_init, tq=tiles['attn'])
    else:
        y_att = _attn_sample(qkv3, past_kv[0], past_kv[1], lam, gsub, 1.0 - lam_init)

    x3, st_new, pool_new = _mix_out(
        x2.reshape(b, t, D_MODEL), proj_a.reshape(b, t, N_SEG_A * SEG), gates.reshape(b, t, N_SEG_G * SEG),
        y_att, _state_to_block_diag(hg_state), pool_prev, lb,
        w['pool_w'], w['pool_scale'], w['hg_outn'], w['w_up_pool'], w['w_up_att'], w['w_up_hgrn'], w['w_out'],
        tm=tiles['mix'], p0=p0)

    x4 = _cross_attend(x3, mk16, mv16, w['norm_cross'], w['w_cq'], w['cross_qn'], w['w_co'], tm=tiles['cross'])
    x5 = _ffn_half(x4.reshape(n, D_MODEL), w['norm_ffn2'], w['w_ffn2_in'], w['w_ffn2_out'],
                   tm=tiles['ffn'], tf=tiles['tf'])
    return (x5.reshape(b, t, D_MODEL), k_rows, v_rows, pool_new[:, POOL_HIST - POOL_STATE:],
            _block_diag_to_state(st_new))


PROMPT_TILES = dict(ffn=512, tf=1408, proj=1024, attn=256, mix=256, cross=512)
SAMPLE_TILES = dict(ffn=512, tf=1408, proj=512, attn=64, mix=64, cross=64)


def kernel(x_prompt, x_sample, cache_attn_k, cache_attn_v, cache_mem_k, cache_mem_v, state_pool, state_hgrn, mem_prompt, norm_ffn1, w_ffn1_in, w_ffn1_out, norm_mix, w_in, pool_w, pool_scale, att_q_norm, att_k_norm, lambda_q1, lambda_k1, lambda_q2, lambda_k2, att_subln, hgrn_lower, hgrn_out_norm, w_up_pool, w_up_att, w_up_hgrn, w_out, norm_cross, norm_mem, w_cq, w_ckv, cross_q_norm, cross_k_norm, w_co, norm_ffn2, w_ffn2_in, w_ffn2_out):
    depth = w_in.shape[0]
    bp = x_prompt.shape[0]
    bs = x_sample.shape[0]
    p0_sample = cache_attn_k.shape[2]

    lp = jax.nn.softmax(hgrn_lower.astype(F32), axis=0)
    lbs = jnp.cumsum(lp, axis=0) - lp[0:1]

    gidx = jnp.arange(SEG) // ATT_DK
    group_mean = ((gidx[:, None] == gidx[None, :]).astype(F32) / ATT_DK).astype(BF16)

    def layer_weights(l):
        return dict(
            norm_ffn1=norm_ffn1[l], w_ffn1_in=w_ffn1_in[l].astype(BF16), w_ffn1_out=w_ffn1_out[l].astype(BF16),
            norm_mix=norm_mix[l], w_in=w_in[l].astype(BF16), group_mean=group_mean,
            att_qn=jnp.tile(att_q_norm[l], SEG // ATT_DK).reshape(1, SEG),
            att_kn=jnp.tile(att_k_norm[l], SEG // ATT_DK).reshape(1, SEG),
            lq1=lambda_q1[l].astype(F32), lk1=lambda_k1[l].astype(F32),
            lq2=lambda_q2[l].astype(F32), lk2=lambda_k2[l].astype(F32),
            att_subln=att_subln[l],
            pool_w=pool_w[l].astype(BF16), pool_scale=pool_scale[l].reshape(1, SEG),
            hg_outn=hgrn_out_norm[l].reshape(1, HG_DV),
            w_up_pool=w_up_pool[l].astype(BF16), w_up_att=w_up_att[l].astype(BF16),
            w_up_hgrn=w_up_hgrn[l].astype(BF16), w_out=w_out[l].astype(BF16),
            norm_cross=norm_cross[l], w_cq=w_cq[l].astype(BF16), cross_qn=cross_q_norm[l],
            w_co=w_co[l].astype(BF16),
            norm_ffn2=norm_ffn2[l], w_ffn2_in=w_ffn2_in[l].astype(BF16), w_ffn2_out=w_ffn2_out[l].astype(BF16),
        )

    weights = [layer_weights(l) for l in range(depth)]

    y = x_prompt
    pk, pv, pmk, pmv, ppool, phg = [], [], [], [], [], []
    for l in range(depth):
        mk, mv, mk16, mv16 = _memory_kv(mem_prompt, norm_mem[l], w_ckv[l].astype(BF16), cross_k_norm[l])
        pool0 = jnp.zeros((bp, POOL_HIST, SEG), F32)
        hg0 = jnp.zeros((bp, HG_HEADS, HG_DK, HG_DV), F32)
        y, kr, vr, pn, hn = _layer(y, l, 0, None, pool0, hg0, lbs[l].reshape(1, HG_W), mk16, mv16,
                                   weights[l], tiles=PROMPT_TILES)
        pk.append(kr); pv.append(vr)
        pmk.append(mk.reshape(bp, -1, X_HEADS, X_HD)); pmv.append(mv.reshape(bp, -1, X_HEADS, X_HD))
        ppool.append(pn); phg.append(hn)
    y_prompt = y

    y = x_sample
    sk, sv, spool, shg = [], [], [], []
    for l in range(depth):
        past = (cache_attn_k[l].reshape(bs, p0_sample, ATT_HEADS * 2 * ATT_DK).astype(BF16),
                cache_attn_v[l].reshape(bs, p0_sample, ATT_HEADS * ATT_DV).astype(BF16))
        pool0 = jnp.pad(state_pool[l], ((0, 0), (POOL_HIST - POOL_STATE, 0), (0, 0)))
        mk16 = cache_mem_k[l].reshape(bs, -1, D_MODEL).astype(BF16)
        mv16 = cache_mem_v[l].reshape(bs, -1, D_MODEL).astype(BF16)
        y, kr, vr, pn, hn = _layer(y, l, p0_sample, past, pool0, state_hgrn[l], lbs[l].reshape(1, HG_W),
                                   mk16, mv16, weights[l], tiles=SAMPLE_TILES)
        sk.append(kr); sv.append(vr); spool.append(pn); shg.append(hn)
    y_sample = y

    return (y_prompt, y_sample,
            jnp.stack(pk), jnp.stack(pv), jnp.stack(pmk), jnp.stack(pmv),
            jnp.stack(ppool), jnp.stack(phg),
            jnp.stack(sk), jnp.stack(sv), jnp.stack(spool), jnp.stack(shg))
```

```python
import functools
import math

import jax
import jax.numpy as jnp
from jax import lax
from jax.experimental import pallas as pl
from jax.experimental.pallas import tpu as pltpu

F32 = jnp.float32
BF16 = jnp.bfloat16

D_MODEL = 1024
CHUNK = 64
EPS = 1e-6
MASK_VALUE = -1e30
POOL_WINDOWS = (2, 4, 8, 16)
POOL_STATE = 15
POOL_HIST = 16
ATT_HEADS = 4
ATT_DK = 64
ATT_DV = 128
ROT_DIMS = 16
ROPE_THETA = 500000.0
HG_HEADS = 4
HG_DK = 64
HG_DV = 128
X_HEADS = 4
X_HD = 256
D_FF = 2816
SEG = 512
N_SEG_A = 7
N_SEG_G = 6
LANES = 128
VMEM_LIMIT = 56 * 1024 * 1024
SCORE_BOUND_MAX = 20.0


def _cparams(sem):
    return pltpu.CompilerParams(dimension_semantics=sem, vmem_limit_bytes=VMEM_LIMIT)


def _rms(x, g):
    ms = jnp.mean(x * x, axis=-1, keepdims=True)
    return x * lax.rsqrt(ms + EPS) * g


def _silu(x):
    return x * jax.nn.sigmoid(x)


def _dot(a, b):
    return jnp.dot(a, b, preferred_element_type=F32)


def _dot_nt(a, b):
    return lax.dot_general(a, b, (((1,), (1,)), ((), ())), preferred_element_type=F32)


def _dot_tn(a, b):
    return lax.dot_general(a, b, (((0,), (0,)), ((), ())), preferred_element_type=F32)


def _shr(x, pow2):
    return lax.shift_right_logical(x, jnp.int32(int(math.log2(pow2))))


def _ffn_kernel(x_ref, g_ref, wa_ref, wb_ref, wo_ref, o_ref, h_sc, acc_sc):
    j = pl.program_id(1)

    @pl.when(j == 0)
    def _():
        h_sc[...] = _rms(x_ref[...], g_ref[...]).astype(BF16)
        acc_sc[...] = jnp.zeros_like(acc_sc)

    h = h_sc[...]
    a = _dot(h, wa_ref[...])
    b = _dot(h, wb_ref[...])
    acc_sc[...] += _dot((_silu(a) * b).astype(BF16), wo_ref[...])

    @pl.when(j == pl.num_programs(1) - 1)
    def _():
        o_ref[...] = x_ref[...] + 0.5 * acc_sc[...]


def _ffn_half(x, g, w_i, w_o, *, tm, tf):
    n = x.shape[0]
    nj = D_FF // tf
    return pl.pallas_call(
        _ffn_kernel,
        grid=(n // tm, nj),
        in_specs=[
            pl.BlockSpec((tm, D_MODEL), lambda i, j: (i, 0)),
            pl.BlockSpec((1, D_MODEL), lambda i, j: (0, 0)),
            pl.BlockSpec((D_MODEL, tf), lambda i, j: (0, j)),
            pl.BlockSpec((D_MODEL, tf), lambda i, j: (0, j + nj)),
            pl.BlockSpec((tf, D_MODEL), lambda i, j: (j, 0)),
        ],
        out_specs=pl.BlockSpec((tm, D_MODEL), lambda i, j: (i, 0)),
        out_shape=jax.ShapeDtypeStruct((n, D_MODEL), F32),
        scratch_shapes=[pltpu.VMEM((tm, D_MODEL), BF16), pltpu.VMEM((tm, D_MODEL), F32)],
        compiler_params=_cparams(("parallel", "arbitrary")),
        name="ffn_half",
    )(x, g.reshape(1, D_MODEL), w_i, w_i, w_o)


def _inproj_kernel(x_ref, g_ref, w_ref, gm_ref, qn_ref, kn_ref, cos_ref, sa_ref, sb_ref,
                   pa_ref, pg_ref, qkv_ref, h_sc):
    j = pl.program_id(1)

    @pl.when(j == 0)
    def _():
        h_sc[...] = _rms(x_ref[...], g_ref[...]).astype(BF16)

    y = _dot(h_sc[...], w_ref[...])

    @pl.when(jnp.logical_or(j == 1, j == 2))
    def _():
        sq = y * y
        hi = sq.astype(BF16)
        lo = (sq - hi.astype(F32)).astype(BF16)
        ms = _dot(hi, gm_ref[...]) + _dot(lo, gm_ref[...])
        gain = jnp.where(j == 1, qn_ref[...], kn_ref[...])
        yn = y * lax.rsqrt(ms + EPS) * gain
        rep = SEG // LANES
        cos = jnp.concatenate([cos_ref[...]] * rep, axis=1)
        sa = jnp.concatenate([sa_ref[...]] * rep, axis=1)
        sb = jnp.concatenate([sb_ref[...]] * rep, axis=1)
        half = ROT_DIMS // 2
        rot = (yn * cos + pltpu.roll(yn, half, 1) * sa + pltpu.roll(yn, SEG - half, 1) * sb)
        pa_ref[...] = rot
        scale = jnp.where(j == 1, ATT_DK ** -0.5, 1.0).astype(F32)
        qkv_ref[...] = (rot * scale).astype(BF16)

    @pl.when(j == 3)
    def _():
        pa_ref[...] = y
        qkv_ref[...] = y.astype(BF16)

    @pl.when(jnp.logical_or(j == 0, jnp.logical_and(j > 3, j < N_SEG_A)))
    def _():
        pa_ref[...] = y

    @pl.when(j >= N_SEG_A)
    def _():
        pg_ref[...] = y


def _in_proj(x, g, w, gm, qn, kn, cos, sa, sb, *, tm):
    n = x.shape[0]
    tab_blocks = cos.shape[0] // tm
    nseg = N_SEG_A + N_SEG_G
    tab_spec = pl.BlockSpec((tm, LANES), lambda i, j: (i % tab_blocks, 0))
    vec_spec = pl.BlockSpec((1, SEG), lambda i, j: (0, 0))
    return pl.pallas_call(
        _inproj_kernel,
        grid=(n // tm, nseg),
        in_specs=[
            pl.BlockSpec((tm, D_MODEL), lambda i, j: (i, 0)),
            pl.BlockSpec((1, D_MODEL), lambda i, j: (0, 0)),
            pl.BlockSpec((D_MODEL, SEG), lambda i, j: (0, j)),
            pl.BlockSpec((SEG, SEG), lambda i, j: (0, 0)),
            vec_spec, vec_spec, tab_spec, tab_spec, tab_spec,
        ],
        out_specs=[
            pl.BlockSpec((tm, SEG), lambda i, j: (i, jnp.minimum(j, N_SEG_A - 1))),
            pl.BlockSpec((tm, SEG), lambda i, j: (i, jnp.maximum(j - N_SEG_A, 0))),
            pl.BlockSpec((tm, SEG), lambda i, j: (i, jnp.clip(j - 1, 0, 2))),
        ],
        out_shape=[
            jax.ShapeDtypeStruct((n, N_SEG_A * SEG), F32),
            jax.ShapeDtypeStruct((n, N_SEG_G * SEG), F32),
            jax.ShapeDtypeStruct((n, 3 * SEG), BF16),
        ],
        scratch_shapes=[pltpu.VMEM((tm, D_MODEL), BF16)],
        compiler_params=_cparams(("parallel", "arbitrary")),
        name="in_proj",
    )(x, g.reshape(1, D_MODEL), w, gm, qn, kn, cos, sa, sb)


def _rope_tables(p0, t):
    half = ROT_DIMS // 2
    inv = ROPE_THETA ** (-jnp.arange(half, dtype=F32) / half)
    pos = p0 + jnp.arange(t, dtype=jnp.int32)
    ang = pos.astype(F32)[:, None] * inv[None, :]
    cos, sin = jnp.cos(ang), jnp.sin(ang)
    ones = jnp.ones((t, ATT_DK - ROT_DIMS), F32)
    zeros = jnp.zeros((t, ATT_DK - ROT_DIMS), F32)
    zh = jnp.zeros((t, half), F32)
    c64 = jnp.concatenate([cos, cos, ones], axis=1)
    sa64 = jnp.concatenate([zh, sin, zeros], axis=1)
    sb64 = jnp.concatenate([-sin, zh, zeros], axis=1)
    rep = LANES // ATT_DK
    return tuple(jnp.concatenate([a] * rep, axis=1) for a in (c64, sa64, sb64))


def _stack_maps(q):
    lane = lax.broadcasted_iota(jnp.int32, q.shape, 1)
    zero = jnp.zeros_like(q)
    return jnp.concatenate([jnp.where(lane < ATT_DK, q, zero), jnp.where(lane >= ATT_DK, q, zero)], axis=0)


def _attn_finish(acc, l, lam, gsub, post_scale, tq):
    o = acc[:tq] / l[:tq] - lam * (acc[tq:] / l[tq:])
    return _rms(o, gsub) * post_scale


def _attn_prompt_kernel(lam_ref, gsub_ref, q_ref, k_ref, v_ref, o_ref, acc_sc, m_sc, l_sc, *, tq, post_scale):
    i = pl.program_id(2)
    qbd = _stack_maps(q_ref[...])
    m_sc[...] = jnp.full_like(m_sc, MASK_VALUE)
    l_sc[...] = jnp.zeros_like(l_sc)
    acc_sc[...] = jnp.zeros_like(acc_sc)

    def step(j, masked):
        off = pl.multiple_of(j * tq, tq)
        kj = k_ref[pl.ds(off, tq), :]
        vj = v_ref[pl.ds(off, tq), :]
        s = _dot_nt(qbd, kj)
        if masked:
            qi = lax.broadcasted_iota(jnp.int32, s.shape, 0) & (tq - 1)
            ki = lax.broadcasted_iota(jnp.int32, s.shape, 1)
            s = jnp.where(_shr(ki, CHUNK) <= _shr(qi, CHUNK), s, MASK_VALUE)
        m_prev = m_sc[...]
        m_new = jnp.maximum(m_prev, jnp.max(s, axis=-1, keepdims=True))
        alpha = jnp.exp(m_prev - m_new)
        p = jnp.exp(s - m_new)
        l_sc[...] = alpha * l_sc[...] + jnp.sum(p, axis=-1, keepdims=True)
        acc_sc[...] = alpha * acc_sc[...] + _dot(p.astype(BF16), vj)
        m_sc[...] = m_new

    def body(j, carry):
        step(j, False)
        return carry

    lax.fori_loop(0, i, body, 0)
    step(i, True)
    o_ref[...] = _attn_finish(acc_sc[...], l_sc[...], lam_ref[...], gsub_ref[...], post_scale, tq)


def _attn_prompt(qkv, lam, gsub, post_scale, *, tq):
    b, t, _ = qkv.shape
    kern = functools.partial(_attn_prompt_kernel, tq=tq, post_scale=post_scale)
    vec_spec = pl.BlockSpec((1, ATT_DV), lambda b_, h, i: (0, 0))
    return pl.pallas_call(
        kern,
        grid=(b, ATT_HEADS, t // tq),
        in_specs=[
            vec_spec, vec_spec,
            pl.BlockSpec((None, tq, ATT_DV), lambda b_, h, i: (b_, i, h)),
            pl.BlockSpec((None, t, ATT_DV), lambda b_, h, i: (b_, 0, ATT_HEADS + h)),
            pl.BlockSpec((None, t, ATT_DV), lambda b_, h, i: (b_, 0, 2 * ATT_HEADS + h)),
        ],
        out_specs=pl.BlockSpec((None, tq, ATT_DV), lambda b_, h, i: (b_, i, h)),
        out_shape=jax.ShapeDtypeStruct((b, t, ATT_HEADS * ATT_DV), F32),
        scratch_shapes=[pltpu.VMEM((2 * tq, ATT_DV), F32), pltpu.VMEM((2 * tq, 1), F32),
                        pltpu.VMEM((2 * tq, 1), F32)],
        compiler_params=_cparams(("parallel", "parallel", "arbitrary")),
        name="diff_attn_prompt",
    )(lam, gsub, qkv, qkv, qkv)


def _attn_bounded_kernel(lam_ref, gsub_ref, qt_ref, k_ref, vt_ref, o_ref, acc_sc, l_sc, *, tq, post_scale):
    i = pl.program_id(2)
    qt = qt_ref[...].astype(F32)
    row = lax.broadcasted_iota(jnp.int32, qt.shape, 0)
    qbd = jnp.concatenate([jnp.where(row < ATT_DK, qt, 0.0), jnp.where(row >= ATT_DK, qt, 0.0)],
                          axis=1).astype(BF16)
    acc_sc[...] = jnp.zeros_like(acc_sc)
    l_sc[...] = jnp.zeros_like(l_sc)

    def step(j, masked):
        off = pl.multiple_of(j * tq, tq)
        s = _dot(k_ref[pl.ds(off, tq), :], qbd)
        if masked:
            ki = lax.broadcasted_iota(jnp.int32, s.shape, 0)
            qi = lax.broadcasted_iota(jnp.int32, s.shape, 1) & (tq - 1)
            s = jnp.where(_shr(ki, CHUNK) <= _shr(qi, CHUNK), s, MASK_VALUE)
        p = jnp.exp(s)
        l_sc[...] += jnp.sum(p.reshape(tq // 8, 8, 2 * tq), axis=0)
        acc_sc[...] += _dot(vt_ref[j], p.astype(BF16))

    def body(j, carry):
        step(j, False)
        return carry

    lax.fori_loop(0, i, body, 0)
    step(i, True)
    l = jnp.sum(l_sc[...], axis=0, keepdims=True)
    acc = acc_sc[...]
    ot = acc[:, :tq] / l[:, :tq] - lam_ref[...] * (acc[:, tq:] / l[:, tq:])
    ms = jnp.mean(ot * ot, axis=0, keepdims=True)
    yt = ot * lax.rsqrt(ms + EPS) * (gsub_ref[...] * post_scale)
    o_ref[...] = yt.T


def _attn_bounded(qkv, lam, gsub, post_scale, *, tq):
    b, t, _ = qkv.shape
    nt = t // tq
    q5 = qkv[:, :, :SEG].reshape(b, nt, tq, ATT_HEADS, ATT_DV).transpose(0, 3, 1, 4, 2)
    v5 = qkv[:, :, 2 * SEG:].reshape(b, nt, tq, ATT_HEADS, ATT_DV).transpose(0, 3, 1, 4, 2)
    kern = functools.partial(_attn_bounded_kernel, tq=tq, post_scale=post_scale)
    return pl.pallas_call(
        kern,
        grid=(b, ATT_HEADS, nt),
        in_specs=[
            pl.BlockSpec((1, 1), lambda b_, h, i: (0, 0)),
            pl.BlockSpec((ATT_DV, 1), lambda b_, h, i: (0, 0)),
            pl.BlockSpec((None, None, None, ATT_DV, tq), lambda b_, h, i: (b_, h, i, 0, 0)),
            pl.BlockSpec((None, t, ATT_DV), lambda b_, h, i: (b_, 0, ATT_HEADS + h)),
            pl.BlockSpec((None, None, nt, ATT_DV, tq), lambda b_, h, i: (b_, h, 0, 0, 0)),
        ],
        out_specs=pl.BlockSpec((None, tq, ATT_DV), lambda b_, h, i: (b_, i, h)),
        out_shape=jax.ShapeDtypeStruct((b, t, ATT_HEADS * ATT_DV), F32),
        scratch_shapes=[pltpu.VMEM((ATT_DV, 2 * tq), F32), pltpu.VMEM((8, 2 * tq), F32)],
        compiler_params=_cparams(("parallel", "parallel", "arbitrary")),
        name="diff_attn_bounded",
    )(lam[:, :1], gsub.reshape(ATT_DV, 1), q5, qkv, v5)


def _attn_sample_kernel(lam_ref, gsub_ref, q_ref, kn_ref, vn_ref, kp_ref, vp_ref, o_ref, *, tq, post_scale):
    qbd = _stack_maps(q_ref[...])
    sp = _dot_nt(qbd, kp_ref[...])
    sn = _dot_nt(qbd, kn_ref[...])
    m = jnp.maximum(jnp.max(sp, axis=-1, keepdims=True), jnp.max(sn, axis=-1, keepdims=True))
    pp = jnp.exp(sp - m)
    pn = jnp.exp(sn - m)
    l = jnp.sum(pp, axis=-1, keepdims=True) + jnp.sum(pn, axis=-1, keepdims=True)
    acc = _dot(pp.astype(BF16), vp_ref[...]) + _dot(pn.astype(BF16), vn_ref[...])
    o_ref[...] = _attn_finish(acc, l, lam_ref[...], gsub_ref[...], post_scale, tq)


def _attn_sample(qkv, kp, vp, lam, gsub, post_scale):
    b, t, _ = qkv.shape
    tp = kp.shape[1]
    kern = functools.partial(_attn_sample_kernel, tq=t, post_scale=post_scale)
    vec_spec = pl.BlockSpec((1, ATT_DV), lambda b_, h: (0, 0))
    return pl.pallas_call(
        kern,
        grid=(b, ATT_HEADS),
        in_specs=[
            vec_spec, vec_spec,
            pl.BlockSpec((None, t, ATT_DV), lambda b_, h: (b_, 0, h)),
            pl.BlockSpec((None, t, ATT_DV), lambda b_, h: (b_, 0, ATT_HEADS + h)),
            pl.BlockSpec((None, t, ATT_DV), lambda b_, h: (b_, 0, 2 * ATT_HEADS + h)),
            pl.BlockSpec((None, tp, ATT_DV), lambda b_, h: (b_, 0, h)),
            pl.BlockSpec((None, tp, ATT_DV), lambda b_, h: (b_, 0, h)),
        ],
        out_specs=pl.BlockSpec((None, t, ATT_DV), lambda b_, h: (b_, 0, h)),
        out_shape=jax.ShapeDtypeStruct((b, t, ATT_HEADS * ATT_DV), F32),
        compiler_params=_cparams(("parallel", "parallel")),
        name="diff_attn_sample",
    )(lam, gsub, qkv, qkv, qkv, kp, vp)


HG_W = HG_HEADS * HG_DK
HG_V = HG_HEADS * HG_DV
B_PAD = 8


def _level_reference(b_sc, w):
    if w >= 4:
        pieces = []
        for p in range(CHUNK // (2 * w)):
            row = b_sc[pl.ds(B_PAD + p * 2 * w + w, 1), :]
            pieces.append(jnp.broadcast_to(row, (2 * w, HG_W)))
        return jnp.concatenate(pieces, axis=0)
    t = lax.broadcasted_iota(jnp.int32, (CHUNK, HG_W), 0)
    phase = t & (2 * w - 1)
    r = None
    for ph in range(2 * w):
        shifted = b_sc[pl.ds(B_PAD + w - ph, CHUNK), :]
        r = shifted if r is None else jnp.where(phase == ph, shifted, r)
    return r


def _hgrn_chunk(hq, hf, hi, lb, st_sc, b_sc, masks):
    tril, bd_k, bd_v, bd_s = masks
    f = lb + (1.0 - lb) * jax.nn.sigmoid(hf)
    g = jnp.log(f)
    kk = 1.0 - f
    q = _silu(hq) * (HG_DK ** -0.5)
    g0 = g.astype(BF16)
    r1 = g - g0.astype(F32)
    g1 = r1.astype(BF16)
    g2 = (r1 - g1.astype(F32)).astype(BF16)
    b = _dot(tril, g0) + _dot(tril, g1) + _dot(tril, g2)
    b_sc[pl.ds(B_PAD, CHUNK), :] = b

    t_idx = lax.broadcasted_iota(jnp.int32, (CHUNK, HG_W), 0)
    s_idx = lax.broadcasted_iota(jnp.int32, (CHUNK, HG_W), 1) & (CHUNK - 1)
    zero = jnp.zeros((CHUNK, HG_W), F32)

    def block_diag_k(x):
        return jnp.where(bd_k, jnp.concatenate([x] * HG_HEADS, axis=0), jnp.zeros((), F32)).astype(BF16)

    a = jnp.where(t_idx == s_idx, _dot_nt(q.astype(BF16), block_diag_k(kk)), zero)
    w = CHUNK // 2
    while w >= 1:
        r = _level_reference(b_sc, w)
        upper = (t_idx & w) != 0
        e = jnp.exp(jnp.where(upper, b - r, r - b))
        ql = jnp.where(upper, q * e, zero)
        kl = jnp.where(upper, zero, kk * e)
        same_pair = _shr(t_idx, 2 * w) == _shr(s_idx, 2 * w)
        a = a + jnp.where(same_pair, _dot_nt(ql.astype(BF16), block_diag_k(kl)), zero)
        w //= 2

    b_last = jnp.broadcast_to(b_sc[pl.ds(B_PAD + CHUNK - 1, 1), :], (CHUNK, HG_W))
    qb = q * jnp.exp(b)
    kd = kk * jnp.exp(b_last - b)
    v16 = hi.astype(BF16)
    vbd = jnp.where(bd_v, jnp.concatenate([hi] * HG_HEADS, axis=0), jnp.zeros((), F32)).astype(BF16)
    st = st_sc[...]
    o = _dot(a.astype(BF16), vbd) + _dot_nt(qb.astype(BF16), st.astype(BF16))
    upd = _dot_tn(v16, kd.astype(BF16))
    decay = jnp.exp(b_sc[pl.ds(B_PAD + CHUNK - 1, 1), :])
    st_sc[...] = st * decay + jnp.where(bd_s, upd, jnp.zeros((), F32))
    return o


def _mix_kernel(x_ref, u_ref, hqf_ref, hi_ref, hg_ref, g_ref, ya_ref, st0_ref, pool0_ref,
                lb_ref, pw_ref, ps_ref, hgn_ref, wup_ref, wua_ref, wuh_ref, wo_ref,
                xo_ref, stn_ref, pooln_ref,
                st_sc, b_sc, ext_sc, yh_sc, *, tm, p0):
    it = pl.program_id(1)

    @pl.when(it == 0)
    def _():
        st_sc[...] = st0_ref[...]
        ext_sc[pl.ds(0, POOL_HIST), :] = pool0_ref[...]
        b_sc[...] = jnp.zeros_like(b_sc)

    u = u_ref[...]
    ext_sc[pl.ds(POOL_HIST, tm), :] = u
    row = lax.broadcasted_iota(jnp.int32, (tm, LANES), 0)
    seen = (p0 + 1 + it * tm + row).astype(F32)
    cols = []
    for gi, w in enumerate(POOL_WINDOWS):
        sl = pl.ds(gi * LANES, LANES)
        win = u[:, gi * LANES:(gi + 1) * LANES]
        for d in range(1, w):
            win = win + ext_sc[pl.ds(POOL_HIST - d, tm), sl]
        dlt = win / jnp.minimum(seen, float(w)) - u[:, gi * LANES:(gi + 1) * LANES]
        cols.append(_dot(dlt.astype(BF16), pw_ref[gi]))
    y_pool = jnp.concatenate(cols, axis=1) * ps_ref[...]
    ext_sc[pl.ds(0, POOL_HIST), :] = ext_sc[pl.ds(tm, POOL_HIST), :]

    tri_r = lax.broadcasted_iota(jnp.int32, (CHUNK, CHUNK), 0)
    tri_c = lax.broadcasted_iota(jnp.int32, (CHUNK, CHUNK), 1)
    tril = (tri_c <= tri_r).astype(BF16)
    rk = lax.broadcasted_iota(jnp.int32, (HG_W, HG_W), 0)
    ck = lax.broadcasted_iota(jnp.int32, (HG_W, HG_W), 1)
    bd_k = _shr(rk, CHUNK) == _shr(ck, HG_DK)
    rv = lax.broadcasted_iota(jnp.int32, (HG_W, HG_V), 0)
    cv = lax.broadcasted_iota(jnp.int32, (HG_W, HG_V), 1)
    bd_v = _shr(rv, CHUNK) == _shr(cv, HG_DV)
    rs = lax.broadcasted_iota(jnp.int32, (HG_V, HG_W), 0)
    cs = lax.broadcasted_iota(jnp.int32, (HG_V, HG_W), 1)
    bd_s = _shr(rs, HG_DV) == _shr(cs, HG_DK)
    masks = (tril, bd_k, bd_v, bd_s)
    lb = lb_ref[...]
    for c in range(tm // CHUNK):
        rows = pl.ds(c * CHUNK, CHUNK)
        o = _hgrn_chunk(hqf_ref[rows, pl.ds(0, HG_W)], hqf_ref[rows, pl.ds(HG_W, HG_W)],
                        hi_ref[rows, :], lb, st_sc, b_sc, masks)
        yh_sc[rows, :] = o
    oh = yh_sc[...]
    hg = hg_ref[...]
    heads = []
    for h in range(HG_HEADS):
        sl = slice(h * HG_DV, (h + 1) * HG_DV)
        heads.append(_rms(oh[:, sl], hgn_ref[...]) * _silu(hg[:, sl]))
    y_hg = jnp.concatenate(heads, axis=1)

    gts = g_ref[...]
    merged = (jax.nn.sigmoid(gts[:, :D_MODEL]) * _dot(y_pool.astype(BF16), wup_ref[...])
              + jax.nn.sigmoid(gts[:, D_MODEL:2 * D_MODEL]) * _dot(ya_ref[...].astype(BF16), wua_ref[...])
              + jax.nn.sigmoid(gts[:, 2 * D_MODEL:]) * _dot(y_hg.astype(BF16), wuh_ref[...]))
    xo_ref[...] = x_ref[...] + _dot(merged.astype(BF16), wo_ref[...])

    @pl.when(it == pl.num_programs(1) - 1)
    def _():
        stn_ref[...] = st_sc[...]
        pooln_ref[...] = ext_sc[pl.ds(0, POOL_HIST), :]


def _mix_out(x, proj_a, gates, y_att, st0, pool0, lb, pool_w, pool_scale, hg_outn,
             w_up_pool, w_up_att, w_up_hgrn, w_out, *, tm, p0):
    b, t, _ = x.shape
    kern = functools.partial(_mix_kernel, tm=tm, p0=p0)

    def rows(width, col):
        return pl.BlockSpec((None, tm, width), lambda b_, i: (b_, i, col))

    def const(shape):
        return pl.BlockSpec(shape, lambda b_, i: (0,) * len(shape))

    def per_batch(shape):
        return pl.BlockSpec((None,) + shape, lambda b_, i: (b_, 0, 0))

    return pl.pallas_call(
        kern,
        grid=(b, t // tm),
        in_specs=[
            rows(D_MODEL, 0),
            rows(SEG, 0),
            rows(SEG, 4),
            rows(SEG, 5),
            rows(SEG, 6),
            rows(3 * D_MODEL, 0),
            rows(SEG, 0),
            per_batch((HG_V, HG_W)),
            per_batch((POOL_HIST, SEG)),
            const((1, HG_W)), const((4, LANES, LANES)), const((1, SEG)), const((1, HG_DV)),
            const((SEG, D_MODEL)), const((SEG, D_MODEL)), const((SEG, D_MODEL)), const((D_MODEL, D_MODEL)),
        ],
        out_specs=[
            rows(D_MODEL, 0),
            per_batch((HG_V, HG_W)),
            per_batch((POOL_HIST, SEG)),
        ],
        out_shape=[
            jax.ShapeDtypeStruct((b, t, D_MODEL), F32),
            jax.ShapeDtypeStruct((b, HG_V, HG_W), F32),
            jax.ShapeDtypeStruct((b, POOL_HIST, SEG), F32),
        ],
        scratch_shapes=[
            pltpu.VMEM((HG_V, HG_W), F32),
            pltpu.VMEM((CHUNK + 2 * B_PAD, HG_W), F32),
            pltpu.VMEM((POOL_HIST + tm, SEG), F32),
            pltpu.VMEM((tm, HG_V), F32),
        ],
        compiler_params=_cparams(("parallel", "arbitrary")),
        name="mix_out",
    )(x, proj_a, proj_a, proj_a, proj_a, gates, y_att, st0, pool0,
      lb, pool_w, pool_scale, hg_outn, w_up_pool, w_up_att, w_up_hgrn, w_out)


def _state_to_block_diag(s):
    b = s.shape[0]
    st = jnp.swapaxes(s, 2, 3)
    eye = jnp.eye(HG_HEADS, dtype=s.dtype)
    return jnp.einsum('bhed,hg->bhegd', st, eye).reshape(b, HG_V, HG_W)


def _block_diag_to_state(st):
    b = st.shape[0]
    s5 = st.reshape(b, HG_HEADS, HG_DV, HG_HEADS, HG_DK)
    diag = jnp.stack([s5[:, h, :, h, :] for h in range(HG_HEADS)], axis=1)
    return jnp.swapaxes(diag, 2, 3)


def _head_rms(x, gain):
    return jnp.concatenate(
        [_rms(x[:, h * X_HD:(h + 1) * X_HD], gain) for h in range(X_HEADS)], axis=1)


def _memkv_kernel(m_ref, g_ref, w_ref, kn_ref, k_ref, v_ref, k16_ref, v16_ref):
    h = _rms(m_ref[...], g_ref[...]).astype(BF16)
    kv = _dot(h, w_ref[...])
    mk = _head_rms(kv[:, :D_MODEL], kn_ref[...])
    mv = kv[:, D_MODEL:]
    k_ref[...] = mk
    v_ref[...] = mv
    k16_ref[...] = mk.astype(BF16)
    v16_ref[...] = mv.astype(BF16)


def _memory_kv(mem, g, w_ckv, kn):
    b, n, _ = mem.shape
    blk = pl.BlockSpec((None, n, D_MODEL), lambda b_: (b_, 0, 0))
    return pl.pallas_call(
        _memkv_kernel,
        grid=(b,),
        in_specs=[
            blk,
            pl.BlockSpec((1, D_MODEL), lambda b_: (0, 0)),
            pl.BlockSpec((D_MODEL, 2 * D_MODEL), lambda b_: (0, 0)),
            pl.BlockSpec((1, X_HD), lambda b_: (0, 0)),
        ],
        out_specs=[blk, blk, blk, blk],
        out_shape=[jax.ShapeDtypeStruct((b, n, D_MODEL), F32)] * 2
        + [jax.ShapeDtypeStruct((b, n, D_MODEL), BF16)] * 2,
        compiler_params=_cparams(("parallel",)),
        name="memory_kv",
    )(mem, g.reshape(1, D_MODEL), w_ckv, kn.reshape(1, X_HD))


def _cross_kernel(x_ref, g_ref, wq_ref, qn_ref, mk_ref, mv_ref, wo_ref, o_ref):
    x = x_ref[...]
    q = _head_rms(_dot(_rms(x, g_ref[...]).astype(BF16), wq_ref[...]), qn_ref[...])
    q = (q * (X_HD ** -0.5)).astype(BF16)
    outs = []
    for h in range(X_HEADS):
        sl = slice(h * X_HD, (h + 1) * X_HD)
        s = _dot_nt(q[:, sl], mk_ref[:, sl])
        p = jnp.exp(s - jnp.max(s, axis=-1, keepdims=True))
        p = p / jnp.sum(p, axis=-1, keepdims=True)
        outs.append(_dot(p.astype(BF16), mv_ref[:, sl]))
    o = jnp.concatenate(outs, axis=1)
    o_ref[...] = x + _dot(o.astype(BF16), wo_ref[...])


def _cross_attend(x, mk, mv, g, w_cq, qn, w_co, *, tm):
    b, t, _ = x.shape
    n = mk.shape[1]
    rows = pl.BlockSpec((None, tm, D_MODEL), lambda b_, i: (b_, i, 0))
    mem = pl.BlockSpec((None, n, D_MODEL), lambda b_, i: (b_, 0, 0))
    wsq = pl.BlockSpec((D_MODEL, D_MODEL), lambda b_, i: (0, 0))
    return pl.pallas_call(
        _cross_kernel,
        grid=(b, t // tm),
        in_specs=[
            rows,
            pl.BlockSpec((1, D_MODEL), lambda b_, i: (0, 0)),
            wsq,
            pl.BlockSpec((1, X_HD), lambda b_, i: (0, 0)),
            mem, mem, wsq,
        ],
        out_specs=rows,
        out_shape=jax.ShapeDtypeStruct((b, t, D_MODEL), F32),
        compiler_params=_cparams(("parallel", "parallel")),
        name="cross_attn",
    )(x, g.reshape(1, D_MODEL), w_cq, qn.reshape(1, X_HD), mk, mv, w_co)


def _layer(x, l, p0, past_kv, pool_prev, hg_state, lb, mk16, mv16, w, *, tiles):
    b, t, _ = x.shape
    n = b * t
    x2 = _ffn_half(x.reshape(n, D_MODEL), w['norm_ffn1'], w['w_ffn1_in'], w['w_ffn1_out'],
                   tm=tiles['ffn'], tf=tiles['tf'])

    cos, sa, sb = _rope_tables(p0, t)
    if t < tiles['proj']:
        rep = tiles['proj'] // t
        cos, sa, sb = (jnp.concatenate([a] * rep, axis=0) for a in (cos, sa, sb))
    proj_a, gates, qkv = _in_proj(x2, w['norm_mix'], w['w_in'], w['group_mean'], w['att_qn'], w['att_kn'],
                                  cos, sa, sb, tm=tiles['proj'])
    k_rows = proj_a[:, 2 * SEG:3 * SEG].reshape(b, t, ATT_HEADS, 2 * ATT_DK)
    v_rows = proj_a[:, 3 * SEG:4 * SEG].reshape(b, t, ATT_HEADS, ATT_DV)

    lam_init = 0.8 - 0.6 * math.exp(-0.3 * l)
    lam = (jnp.exp(jnp.sum(w['lq1'] * w['lk1'])) - jnp.exp(jnp.sum(w['lq2'] * w['lk2'])) + lam_init)
    lam = jnp.full((1, ATT_DV), lam, F32)
    gsub = w['att_subln'].reshape(1, ATT_DV)
    qkv3 = qkv.reshape(b, t, 3 * SEG)
    if past_kv is None:
        bound = (ATT_DK ** 0.5) * jnp.max(jnp.abs(w['att_qn'])) * jnp.max(jnp.abs(w['att_kn']))
        y_att = lax.cond(
            bound <= SCORE_BOUND_MAX,
            lambda a: _attn_bounded(a, lam, gsub, 1.0 - lam_init, tq=tiles['attn_bounded']),
            lambda a: _attn_prompt(a, lam, gsub, 1.0 - lam_init, tq=tiles['attn']),
            qkv3)
    else:
        y_att = _attn_sample(qkv3, past_kv[0], past_kv[1], lam, gsub, 1.0 - lam_init)

    x3, st_new, pool_new = _mix_out(
        x2.reshape(b, t, D_MODEL), proj_a.reshape(b, t, N_SEG_A * SEG), gates.reshape(b, t, N_SEG_G * SEG),
        y_att, _state_to_block_diag(hg_state), pool_prev, lb,
        w['pool_w'], w['pool_scale'], w['hg_outn'], w['w_up_pool'], w['w_up_att'], w['w_up_hgrn'], w['w_out'],
        tm=tiles['mix'], p0=p0)

    x4 = _cross_attend(x3, mk16, mv16, w['norm_cross'], w['w_cq'], w['cross_qn'], w['w_co'], tm=tiles['cross'])
    x5 = _ffn_half(x4.reshape(n, D_MODEL), w['norm_ffn2'], w['w_ffn2_in'], w['w_ffn2_out'],
                   tm=tiles['ffn'], tf=tiles['tf'])
    return (x5.reshape(b, t, D_MODEL), k_rows, v_rows, pool_new[:, POOL_HIST - POOL_STATE:],
            _block_diag_to_state(st_new))


PROMPT_TILES = dict(ffn=512, tf=1408, proj=1024, attn=256, attn_bounded=512, mix=256, cross=512)
SAMPLE_TILES = dict(ffn=512, tf=1408, proj=512, attn=64, mix=64, cross=64)


def kernel(x_prompt, x_sample, cache_attn_k, cache_attn_v, cache_mem_k, cache_mem_v, state_pool, state_hgrn, mem_prompt, norm_ffn1, w_ffn1_in, w_ffn1_out, norm_mix, w_in, pool_w, pool_scale, att_q_norm, att_k_norm, lambda_q1, lambda_k1, lambda_q2, lambda_k2, att_subln, hgrn_lower, hgrn_out_norm, w_up_pool, w_up_att, w_up_hgrn, w_out, norm_cross, norm_mem, w_cq, w_ckv, cross_q_norm, cross_k_norm, w_co, norm_ffn2, w_ffn2_in, w_ffn2_out):
    depth = w_in.shape[0]
    bp = x_prompt.shape[0]
    bs = x_sample.shape[0]
    p0_sample = cache_attn_k.shape[2]

    lp = jax.nn.softmax(hgrn_lower.astype(F32), axis=0)
    lbs = jnp.cumsum(lp, axis=0) - lp[0:1]

    gidx = jnp.arange(SEG) // ATT_DK
    group_mean = ((gidx[:, None] == gidx[None, :]).astype(F32) / ATT_DK).astype(BF16)

    def layer_weights(l):
        return dict(
            norm_ffn1=norm_ffn1[l], w_ffn1_in=w_ffn1_in[l].astype(BF16), w_ffn1_out=w_ffn1_out[l].astype(BF16),
            norm_mix=norm_mix[l], w_in=w_in[l].astype(BF16), group_mean=group_mean,
            att_qn=jnp.tile(att_q_norm[l], SEG // ATT_DK).reshape(1, SEG),
            att_kn=jnp.tile(att_k_norm[l], SEG // ATT_DK).reshape(1, SEG),
            lq1=lambda_q1[l].astype(F32), lk1=lambda_k1[l].astype(F32),
            lq2=lambda_q2[l].astype(F32), lk2=lambda_k2[l].astype(F32),
            att_subln=att_subln[l],
            pool_w=pool_w[l].astype(BF16), pool_scale=pool_scale[l].reshape(1, SEG),
            hg_outn=hgrn_out_norm[l].reshape(1, HG_DV),
            w_up_pool=w_up_pool[l].astype(BF16), w_up_att=w_up_att[l].astype(BF16),
            w_up_hgrn=w_up_hgrn[l].astype(BF16), w_out=w_out[l].astype(BF16),
            norm_cross=norm_cross[l], w_cq=w_cq[l].astype(BF16), cross_qn=cross_q_norm[l],
            w_co=w_co[l].astype(BF16),
            norm_ffn2=norm_ffn2[l], w_ffn2_in=w_ffn2_in[l].astype(BF16), w_ffn2_out=w_ffn2_out[l].astype(BF16),
        )

    weights = [layer_weights(l) for l in range(depth)]

    y = x_prompt
    pk, pv, pmk, pmv, ppool, phg = [], [], [], [], [], []
    for l in range(depth):
        mk, mv, mk16, mv16 = _memory_kv(mem_prompt, norm_mem[l], w_ckv[l].astype(BF16), cross_k_norm[l])
        pool0 = jnp.zeros((bp, POOL_HIST, SEG), F32)
        hg0 = jnp.zeros((bp, HG_HEADS, HG_DK, HG_DV), F32)
        y, kr, vr, pn, hn = _layer(y, l, 0, None, pool0, hg0, lbs[l].reshape(1, HG_W), mk16, mv16,
                                   weights[l], tiles=PROMPT_TILES)
        pk.append(kr); pv.append(vr)
        pmk.append(mk.reshape(bp, -1, X_HEADS, X_HD)); pmv.append(mv.reshape(bp, -1, X_HEADS, X_HD))
        ppool.append(pn); phg.append(hn)
    y_prompt = y

    y = x_sample
    sk, sv, spool, shg = [], [], [], []
    for l in range(depth):
        past = (cache_attn_k[l].reshape(bs, p0_sample, ATT_HEADS * 2 * ATT_DK).astype(BF16),
                cache_attn_v[l].reshape(bs, p0_sample, ATT_HEADS * ATT_DV).astype(BF16))
        pool0 = jnp.pad(state_pool[l], ((0, 0), (POOL_HIST - POOL_STATE, 0), (0, 0)))
        mk16 = cache_mem_k[l].reshape(bs, -1, D_MODEL).astype(BF16)
        mv16 = cache_mem_v[l].reshape(bs, -1, D_MODEL).astype(BF16)
        y, kr, vr, pn, hn = _layer(y, l, p0_sample, past, pool0, state_hgrn[l], lbs[l].reshape(1, HG_W),
                                   mk16, mv16, weights[l], tiles=SAMPLE_TILES)
        sk.append(kr); sv.append(vr); spool.append(pn); shg.append(hn)
    y_sample = y

    return (y_prompt, y_sample,
            jnp.stack(pk), jnp.stack(pv), jnp.stack(pmk), jnp.stack(pmv),
            jnp.stack(ppool), jnp.stack(phg),
            jnp.stack(sk), jnp.stack(sv), jnp.stack(spool), jnp.stack(shg))
```

```python
import functools
import math

import jax
import jax.numpy as jnp
from jax import lax
from jax.experimental import pallas as pl
from jax.experimental.pallas import tpu as pltpu

F32 = jnp.float32
BF16 = jnp.bfloat16

D_MODEL = 1024
CHUNK = 64
EPS = 1e-6
MASK_VALUE = -1e30
POOL_WINDOWS = (2, 4, 8, 16)
POOL_STATE = 15
POOL_HIST = 16
ATT_HEADS = 4
ATT_DK = 64
ATT_DV = 128
ROT_DIMS = 16
ROPE_THETA = 500000.0
HG_HEADS = 4
HG_DK = 64
HG_DV = 128
X_HEADS = 4
X_HD = 256
D_FF = 2816
SEG = 512
N_SEG_A = 7
N_SEG_P = 4
N_SEG_G = 6
LANES = 128
VMEM_LIMIT = 56 * 1024 * 1024
SCORE_BOUND_MAX = 20.0


def _cparams(sem):
    return pltpu.CompilerParams(dimension_semantics=sem, vmem_limit_bytes=VMEM_LIMIT)


def _rms(x, g):
    ms = jnp.mean(x * x, axis=-1, keepdims=True)
    return x * lax.rsqrt(ms + EPS) * g


def _silu(x):
    return x * jax.nn.sigmoid(x)


def _dot(a, b):
    return jnp.dot(a, b, preferred_element_type=F32)


def _dot_nt(a, b):
    return lax.dot_general(a, b, (((1,), (1,)), ((), ())), preferred_element_type=F32)


def _dot_tn(a, b):
    return lax.dot_general(a, b, (((0,), (0,)), ((), ())), preferred_element_type=F32)


def _shr(x, pow2):
    return lax.shift_right_logical(x, jnp.int32(int(math.log2(pow2))))


def _ffn_kernel(x_ref, g_ref, wi_ref, wo_ref, o_ref, *, tf):
    x = x_ref[...]
    h = _rms(x, g_ref[...]).astype(BF16)
    acc = None
    for c in range(D_FF // tf):
        a = _dot(h, wi_ref[:, pl.ds(c * tf, tf)])
        b = _dot(h, wi_ref[:, pl.ds(D_FF + c * tf, tf)])
        part = _dot((_silu(a) * b).astype(BF16), wo_ref[pl.ds(c * tf, tf), :])
        acc = part if acc is None else acc + part
    o_ref[...] = x + 0.5 * acc


def _resident(shape):
    return pl.BlockSpec(shape, lambda *_: (0,) * len(shape), pipeline_mode=pl.Buffered(1))


def _ffn_half(x, g, w_i, w_o, *, tm, tf):
    n = x.shape[0]
    return pl.pallas_call(
        functools.partial(_ffn_kernel, tf=tf),
        grid=(n // tm,),
        in_specs=[
            pl.BlockSpec((tm, D_MODEL), lambda i: (i, 0)),
            _resident((1, D_MODEL)),
            _resident((D_MODEL, 2 * D_FF)),
            _resident((D_FF, D_MODEL)),
        ],
        out_specs=pl.BlockSpec((tm, D_MODEL), lambda i: (i, 0)),
        out_shape=jax.ShapeDtypeStruct((n, D_MODEL), F32),
        compiler_params=_cparams(("parallel",)),
        name="ffn_half",
    )(x, g.reshape(1, D_MODEL), w_i, w_o)


def _inproj_kernel(x_ref, g_ref, w_ref, gm_ref, qn_ref, kn_ref, cos_ref, sa_ref, sb_ref,
                   pa_ref, kf_ref, vf_ref, pg_ref, qkv_ref, h_sc):
    j = pl.program_id(1)

    @pl.when(j == 0)
    def _():
        h_sc[...] = _rms(x_ref[...], g_ref[...]).astype(BF16)

    def proj():
        return _dot(h_sc[...], w_ref[...])

    @pl.when(jnp.logical_or(j == 1, j == 2))
    def _():
        y = proj()
        ms = _dot((y * y).astype(BF16), gm_ref[...])
        gain = jnp.where(j == 1, qn_ref[...], kn_ref[...])
        yn = y * lax.rsqrt(ms + EPS) * gain
        rep = SEG // LANES
        cos = jnp.concatenate([cos_ref[...]] * rep, axis=1)
        sa = jnp.concatenate([sa_ref[...]] * rep, axis=1)
        sb = jnp.concatenate([sb_ref[...]] * rep, axis=1)
        half = ROT_DIMS // 2
        rot = (yn * cos + pltpu.roll(yn, half, 1) * sa + pltpu.roll(yn, SEG - half, 1) * sb)

        @pl.when(j == 1)
        def _():
            qkv_ref[...] = (rot * (ATT_DK ** -0.5)).astype(BF16)

        @pl.when(j == 2)
        def _():
            kf_ref[...] = rot
            qkv_ref[...] = rot.astype(BF16)

    @pl.when(j == 3)
    def _():
        y = proj()
        vf_ref[...] = y
        qkv_ref[...] = y.astype(BF16)

    @pl.when(jnp.logical_or(j == 0, jnp.logical_and(j > 3, j < N_SEG_A)))
    def _():
        pa_ref[...] = proj()

    @pl.when(j >= N_SEG_A)
    def _():
        pg_ref[...] = proj()


def _in_proj(x, g, w, gm, qn, kn, cos, sa, sb, *, tm):
    n = x.shape[0]
    tab_blocks = cos.shape[0] // tm
    nseg = N_SEG_A + N_SEG_G
    tab_spec = pl.BlockSpec((tm, LANES), lambda i, j: (i % tab_blocks, 0))
    vec_spec = pl.BlockSpec((1, SEG), lambda i, j: (0, 0))
    row_spec = pl.BlockSpec((tm, SEG), lambda i, j: (i, 0))
    return pl.pallas_call(
        _inproj_kernel,
        grid=(n // tm, nseg),
        in_specs=[
            pl.BlockSpec((tm, D_MODEL), lambda i, j: (i, 0)),
            pl.BlockSpec((1, D_MODEL), lambda i, j: (0, 0)),
            pl.BlockSpec((D_MODEL, SEG), lambda i, j: (0, j)),
            pl.BlockSpec((SEG, SEG), lambda i, j: (0, 0)),
            vec_spec, vec_spec, tab_spec, tab_spec, tab_spec,
        ],
        out_specs=[
            pl.BlockSpec((tm, SEG), lambda i, j: (i, jnp.clip(j - 3, 0, N_SEG_P - 1))),
            row_spec, row_spec,
            pl.BlockSpec((tm, SEG), lambda i, j: (i, jnp.maximum(j - N_SEG_A, 0))),
            pl.BlockSpec((tm, SEG), lambda i, j: (i, jnp.clip(j - 1, 0, 2))),
        ],
        out_shape=[
            jax.ShapeDtypeStruct((n, N_SEG_P * SEG), F32),
            jax.ShapeDtypeStruct((n, SEG), F32),
            jax.ShapeDtypeStruct((n, SEG), F32),
            jax.ShapeDtypeStruct((n, N_SEG_G * SEG), F32),
            jax.ShapeDtypeStruct((n, 3 * SEG), BF16),
        ],
        scratch_shapes=[pltpu.VMEM((tm, D_MODEL), BF16)],
        compiler_params=_cparams(("parallel", "arbitrary")),
        name="in_proj",
    )(x, g.reshape(1, D_MODEL), w, gm, qn, kn, cos, sa, sb)


def _rope_tables(p0, t):
    half = ROT_DIMS // 2
    inv = ROPE_THETA ** (-jnp.arange(half, dtype=F32) / half)
    pos = p0 + jnp.arange(t, dtype=jnp.int32)
    ang = pos.astype(F32)[:, None] * inv[None, :]
    cos, sin = jnp.cos(ang), jnp.sin(ang)
    ones = jnp.ones((t, ATT_DK - ROT_DIMS), F32)
    zeros = jnp.zeros((t, ATT_DK - ROT_DIMS), F32)
    zh = jnp.zeros((t, half), F32)
    c64 = jnp.concatenate([cos, cos, ones], axis=1)
    sa64 = jnp.concatenate([zh, sin, zeros], axis=1)
    sb64 = jnp.concatenate([-sin, zh, zeros], axis=1)
    rep = LANES // ATT_DK
    return tuple(jnp.concatenate([a] * rep, axis=1) for a in (c64, sa64, sb64))


def _stack_maps(q):
    lane = lax.broadcasted_iota(jnp.int32, q.shape, 1)
    zero = jnp.zeros_like(q)
    return jnp.concatenate([jnp.where(lane < ATT_DK, q, zero), jnp.where(lane >= ATT_DK, q, zero)], axis=0)


def _attn_finish(acc, l, lam, gsub, post_scale, tq):
    o = acc[:tq] / l[:tq] - lam * (acc[tq:] / l[tq:])
    return _rms(o, gsub) * post_scale


def _attn_prompt_kernel(lam_ref, gsub_ref, q_ref, k_ref, v_ref, o_ref, acc_sc, m_sc, l_sc, *, tq, post_scale):
    i = pl.program_id(2)
    qbd = _stack_maps(q_ref[...])
    m_sc[...] = jnp.full_like(m_sc, MASK_VALUE)
    l_sc[...] = jnp.zeros_like(l_sc)
    acc_sc[...] = jnp.zeros_like(acc_sc)

    def step(j, masked):
        off = pl.multiple_of(j * tq, tq)
        kj = k_ref[pl.ds(off, tq), :]
        vj = v_ref[pl.ds(off, tq), :]
        s = _dot_nt(qbd, kj)
        if masked:
            qi = lax.broadcasted_iota(jnp.int32, s.shape, 0) & (tq - 1)
            ki = lax.broadcasted_iota(jnp.int32, s.shape, 1)
            s = jnp.where(_shr(ki, CHUNK) <= _shr(qi, CHUNK), s, MASK_VALUE)
        m_prev = m_sc[...]
        m_new = jnp.maximum(m_prev, jnp.max(s, axis=-1, keepdims=True))
        alpha = jnp.exp(m_prev - m_new)
        p = jnp.exp(s - m_new)
        l_sc[...] = alpha * l_sc[...] + jnp.sum(p, axis=-1, keepdims=True)
        acc_sc[...] = alpha * acc_sc[...] + _dot(p.astype(BF16), vj)
        m_sc[...] = m_new

    def body(j, carry):
        step(j, False)
        return carry

    lax.fori_loop(0, i, body, 0)
    step(i, True)
    o_ref[...] = _attn_finish(acc_sc[...], l_sc[...], lam_ref[...], gsub_ref[...], post_scale, tq)


def _attn_prompt(qkv, lam, gsub, post_scale, *, tq):
    b, t, _ = qkv.shape
    kern = functools.partial(_attn_prompt_kernel, tq=tq, post_scale=post_scale)
    vec_spec = pl.BlockSpec((1, ATT_DV), lambda b_, h, i: (0, 0))
    return pl.pallas_call(
        kern,
        grid=(b, ATT_HEADS, t // tq),
        in_specs=[
            vec_spec, vec_spec,
            pl.BlockSpec((None, tq, ATT_DV), lambda b_, h, i: (b_, i, h)),
            pl.BlockSpec((None, t, ATT_DV), lambda b_, h, i: (b_, 0, ATT_HEADS + h)),
            pl.BlockSpec((None, t, ATT_DV), lambda b_, h, i: (b_, 0, 2 * ATT_HEADS + h)),
        ],
        out_specs=pl.BlockSpec((None, tq, ATT_DV), lambda b_, h, i: (b_, i, h)),
        out_shape=jax.ShapeDtypeStruct((b, t, ATT_HEADS * ATT_DV), F32),
        scratch_shapes=[pltpu.VMEM((2 * tq, ATT_DV), F32), pltpu.VMEM((2 * tq, 1), F32),
                        pltpu.VMEM((2 * tq, 1), F32)],
        compiler_params=_cparams(("parallel", "parallel", "arbitrary")),
        name="diff_attn_prompt",
    )(lam, gsub, qkv, qkv, qkv)


def _attn_bounded_kernel(lam_ref, gsub_ref, qt_ref, k_ref, vt_ref, o_ref, acc_sc, l_sc, *, tq, post_scale):
    i = pl.program_id(2)
    qt = qt_ref[...].astype(F32)
    row = lax.broadcasted_iota(jnp.int32, qt.shape, 0)
    qbd = jnp.concatenate([jnp.where(row < ATT_DK, qt, 0.0), jnp.where(row >= ATT_DK, qt, 0.0)],
                          axis=1).astype(BF16)
    acc_sc[...] = jnp.zeros_like(acc_sc)
    l_sc[...] = jnp.zeros_like(l_sc)

    def probs(j, masked):
        off = pl.multiple_of(j * tq, tq)
        s = _dot(k_ref[pl.ds(off, tq), :], qbd)
        if masked:
            ki = lax.broadcasted_iota(jnp.int32, s.shape, 0)
            qi = lax.broadcasted_iota(jnp.int32, s.shape, 1) & (tq - 1)
            s = jnp.where(_shr(ki, CHUNK) <= _shr(qi, CHUNK), s, MASK_VALUE)
        p = jnp.exp(s)
        return jnp.sum(p.reshape(tq // 8, 8, 2 * tq), axis=0), _dot(vt_ref[j], p.astype(BF16))

    def pair(jj, carry):
        l0, a0 = probs(2 * jj, False)
        l1, a1 = probs(2 * jj + 1, False)
        l_sc[...] += l0 + l1
        acc_sc[...] += a0 + a1
        return carry

    lax.fori_loop(0, i // 2, pair, 0)

    @pl.when(i % 2 == 1)
    def _():
        l0, a0 = probs(i - 1, False)
        l_sc[...] += l0
        acc_sc[...] += a0

    l0, a0 = probs(i, True)
    l_sc[...] += l0
    acc_sc[...] += a0
    l = jnp.sum(l_sc[...], axis=0, keepdims=True)
    acc = acc_sc[...]
    ot = acc[:, :tq] / l[:, :tq] - lam_ref[...] * (acc[:, tq:] / l[:, tq:])
    ms = jnp.mean(ot * ot, axis=0, keepdims=True)
    yt = ot * lax.rsqrt(ms + EPS) * (gsub_ref[...] * post_scale)
    o_ref[...] = yt.T


def _attn_bounded(qkv, lam, gsub, post_scale, *, tq):
    b, t, _ = qkv.shape
    nt = t // tq
    q5 = qkv[:, :, :SEG].reshape(b, nt, tq, ATT_HEADS, ATT_DV).transpose(0, 3, 1, 4, 2)
    v5 = qkv[:, :, 2 * SEG:].reshape(b, nt, tq, ATT_HEADS, ATT_DV).transpose(0, 3, 1, 4, 2)
    kern = functools.partial(_attn_bounded_kernel, tq=tq, post_scale=post_scale)
    return pl.pallas_call(
        kern,
        grid=(b, ATT_HEADS, nt),
        in_specs=[
            pl.BlockSpec((1, 1), lambda b_, h, i: (0, 0)),
            pl.BlockSpec((ATT_DV, 1), lambda b_, h, i: (0, 0)),
            pl.BlockSpec((None, None, None, ATT_DV, tq), lambda b_, h, i: (b_, h, i, 0, 0)),
            pl.BlockSpec((None, t, ATT_DV), lambda b_, h, i: (b_, 0, ATT_HEADS + h)),
            pl.BlockSpec((None, None, nt, ATT_DV, tq), lambda b_, h, i: (b_, h, 0, 0, 0)),
        ],
        out_specs=pl.BlockSpec((None, tq, ATT_DV), lambda b_, h, i: (b_, i, h)),
        out_shape=jax.ShapeDtypeStruct((b, t, ATT_HEADS * ATT_DV), F32),
        scratch_shapes=[pltpu.VMEM((ATT_DV, 2 * tq), F32), pltpu.VMEM((8, 2 * tq), F32)],
        compiler_params=_cparams(("parallel", "parallel", "arbitrary")),
        name="diff_attn_bounded",
    )(lam[:, :1], gsub.reshape(ATT_DV, 1), q5, qkv, v5)


def _attn_sample_kernel(lam_ref, gsub_ref, q_ref, kn_ref, vn_ref, kp_ref, vp_ref, o_ref, *, tq, post_scale):
    qbd = _stack_maps(q_ref[...])
    sp = _dot_nt(qbd, kp_ref[...])
    sn = _dot_nt(qbd, kn_ref[...])
    m = jnp.maximum(jnp.max(sp, axis=-1, keepdims=True), jnp.max(sn, axis=-1, keepdims=True))
    pp = jnp.exp(sp - m)
    pn = jnp.exp(sn - m)
    l = jnp.sum(pp, axis=-1, keepdims=True) + jnp.sum(pn, axis=-1, keepdims=True)
    acc = _dot(pp.astype(BF16), vp_ref[...]) + _dot(pn.astype(BF16), vn_ref[...])
    o_ref[...] = _attn_finish(acc, l, lam_ref[...], gsub_ref[...], post_scale, tq)


def _attn_sample(qkv, kp, vp, lam, gsub, post_scale):
    b, t, _ = qkv.shape
    tp = kp.shape[1]
    kern = functools.partial(_attn_sample_kernel, tq=t, post_scale=post_scale)
    vec_spec = pl.BlockSpec((1, ATT_DV), lambda b_, h: (0, 0))
    return pl.pallas_call(
        kern,
        grid=(b, ATT_HEADS),
        in_specs=[
            vec_spec, vec_spec,
            pl.BlockSpec((None, t, ATT_DV), lambda b_, h: (b_, 0, h)),
            pl.BlockSpec((None, t, ATT_DV), lambda b_, h: (b_, 0, ATT_HEADS + h)),
            pl.BlockSpec((None, t, ATT_DV), lambda b_, h: (b_, 0, 2 * ATT_HEADS + h)),
            pl.BlockSpec((None, tp, ATT_DV), lambda b_, h: (b_, 0, h)),
            pl.BlockSpec((None, tp, ATT_DV), lambda b_, h: (b_, 0, h)),
        ],
        out_specs=pl.BlockSpec((None, t, ATT_DV), lambda b_, h: (b_, 0, h)),
        out_shape=jax.ShapeDtypeStruct((b, t, ATT_HEADS * ATT_DV), F32),
        compiler_params=_cparams(("parallel", "parallel")),
        name="diff_attn_sample",
    )(lam, gsub, qkv, qkv, qkv, kp, vp)


HG_W = HG_HEADS * HG_DK
HG_V = HG_HEADS * HG_DV
B_PAD = 8


def _level_reference(b_sc, w):
    if w >= 4:
        pieces = []
        for p in range(CHUNK // (2 * w)):
            row = b_sc[pl.ds(B_PAD + p * 2 * w + w, 1), :]
            pieces.append(jnp.broadcast_to(row, (2 * w, HG_W)))
        return jnp.concatenate(pieces, axis=0)
    t = lax.broadcasted_iota(jnp.int32, (CHUNK, HG_W), 0)
    phase = t & (2 * w - 1)
    r = None
    for ph in range(2 * w):
        shifted = b_sc[pl.ds(B_PAD + w - ph, CHUNK), :]
        r = shifted if r is None else jnp.where(phase == ph, shifted, r)
    return r


def _hgrn_chunk(hq, hf, hi, lb, st_sc, b_sc, masks):
    tril, bd_k, bd_v, bd_s = masks
    f = lb + (1.0 - lb) * jax.nn.sigmoid(hf)
    g = jnp.log(f)
    kk = 1.0 - f
    q = _silu(hq) * (HG_DK ** -0.5)
    g0 = g.astype(BF16)
    r1 = g - g0.astype(F32)
    g1 = r1.astype(BF16)
    g2 = (r1 - g1.astype(F32)).astype(BF16)
    b = _dot(tril, g0) + _dot(tril, g1) + _dot(tril, g2)
    b_sc[pl.ds(B_PAD, CHUNK), :] = b

    t_idx = lax.broadcasted_iota(jnp.int32, (CHUNK, HG_W), 0)
    s_idx = lax.broadcasted_iota(jnp.int32, (CHUNK, HG_W), 1) & (CHUNK - 1)
    zero = jnp.zeros((CHUNK, HG_W), F32)

    def block_diag_k(x):
        return jnp.where(bd_k, jnp.concatenate([x] * HG_HEADS, axis=0), jnp.zeros((), F32)).astype(BF16)

    a = jnp.where(t_idx == s_idx, _dot_nt(q.astype(BF16), block_diag_k(kk)), zero)
    w = CHUNK // 2
    while w >= 1:
        r = _level_reference(b_sc, w)
        upper = (t_idx & w) != 0
        e = jnp.exp(jnp.where(upper, b - r, r - b))
        ql = jnp.where(upper, q * e, zero)
        kl = jnp.where(upper, zero, kk * e)
        same_pair = _shr(t_idx, 2 * w) == _shr(s_idx, 2 * w)
        a = a + jnp.where(same_pair, _dot_nt(ql.astype(BF16), block_diag_k(kl)), zero)
        w //= 2

    b_last = jnp.broadcast_to(b_sc[pl.ds(B_PAD + CHUNK - 1, 1), :], (CHUNK, HG_W))
    qb = q * jnp.exp(b)
    kd = kk * jnp.exp(b_last - b)
    v16 = hi.astype(BF16)
    vbd = jnp.where(bd_v, jnp.concatenate([hi] * HG_HEADS, axis=0), jnp.zeros((), F32)).astype(BF16)
    st = st_sc[...]
    o = _dot(a.astype(BF16), vbd) + _dot_nt(qb.astype(BF16), st.astype(BF16))
    upd = _dot_tn(v16, kd.astype(BF16))
    decay = jnp.exp(b_sc[pl.ds(B_PAD + CHUNK - 1, 1), :])
    st_sc[...] = st * decay + jnp.where(bd_s, upd, jnp.zeros((), F32))
    return o


def _mix_kernel(x_ref, u_ref, hqf_ref, hi_ref, hg_ref, g_ref, ya_ref, st0_ref, pool0_ref,
                lb_ref, pw_ref, ps_ref, hgn_ref, wup_ref, wua_ref, wuh_ref, wo_ref,
                xo_ref, stn_ref, pooln_ref,
                st_sc, b_sc, ext_sc, yh_sc, *, tm, p0):
    it = pl.program_id(1)

    @pl.when(it == 0)
    def _():
        st_sc[...] = st0_ref[...]
        ext_sc[pl.ds(0, POOL_HIST), :] = pool0_ref[...]
        b_sc[...] = jnp.zeros_like(b_sc)

    u = u_ref[...]
    ext_sc[pl.ds(POOL_HIST, tm), :] = u
    row = lax.broadcasted_iota(jnp.int32, (tm, LANES), 0)
    seen = (p0 + 1 + it * tm + row).astype(F32)
    cols = []
    for gi, w in enumerate(POOL_WINDOWS):
        sl = pl.ds(gi * LANES, LANES)
        win = u[:, gi * LANES:(gi + 1) * LANES]
        for d in range(1, w):
            win = win + ext_sc[pl.ds(POOL_HIST - d, tm), sl]
        dlt = win / jnp.minimum(seen, float(w)) - u[:, gi * LANES:(gi + 1) * LANES]
        cols.append(_dot(dlt.astype(BF16), pw_ref[gi]))
    y_pool = jnp.concatenate(cols, axis=1) * ps_ref[...]
    ext_sc[pl.ds(0, POOL_HIST), :] = ext_sc[pl.ds(tm, POOL_HIST), :]

    tri_r = lax.broadcasted_iota(jnp.int32, (CHUNK, CHUNK), 0)
    tri_c = lax.broadcasted_iota(jnp.int32, (CHUNK, CHUNK), 1)
    tril = (tri_c <= tri_r).astype(BF16)
    rk = lax.broadcasted_iota(jnp.int32, (HG_W, HG_W), 0)
    ck = lax.broadcasted_iota(jnp.int32, (HG_W, HG_W), 1)
    bd_k = _shr(rk, CHUNK) == _shr(ck, HG_DK)
    rv = lax.broadcasted_iota(jnp.int32, (HG_W, HG_V), 0)
    cv = lax.broadcasted_iota(jnp.int32, (HG_W, HG_V), 1)
    bd_v = _shr(rv, CHUNK) == _shr(cv, HG_DV)
    rs = lax.broadcasted_iota(jnp.int32, (HG_V, HG_W), 0)
    cs = lax.broadcasted_iota(jnp.int32, (HG_V, HG_W), 1)
    bd_s = _shr(rs, HG_DV) == _shr(cs, HG_DK)
    masks = (tril, bd_k, bd_v, bd_s)
    lb = lb_ref[...]
    for c in range(tm // CHUNK):
        rows = pl.ds(c * CHUNK, CHUNK)
        o = _hgrn_chunk(hqf_ref[rows, pl.ds(0, HG_W)], hqf_ref[rows, pl.ds(HG_W, HG_W)],
                        hi_ref[rows, :], lb, st_sc, b_sc, masks)
        yh_sc[rows, :] = o
    oh = yh_sc[...]
    hg = hg_ref[...]
    heads = []
    for h in range(HG_HEADS):
        sl = slice(h * HG_DV, (h + 1) * HG_DV)
        heads.append(_rms(oh[:, sl], hgn_ref[...]) * _silu(hg[:, sl]))
    y_hg = jnp.concatenate(heads, axis=1)

    gts = g_ref[...]
    merged = (jax.nn.sigmoid(gts[:, :D_MODEL]) * _dot(y_pool.astype(BF16), wup_ref[...])
              + jax.nn.sigmoid(gts[:, D_MODEL:2 * D_MODEL]) * _dot(ya_ref[...].astype(BF16), wua_ref[...])
              + jax.nn.sigmoid(gts[:, 2 * D_MODEL:]) * _dot(y_hg.astype(BF16), wuh_ref[...]))
    xo_ref[...] = x_ref[...] + _dot(merged.astype(BF16), wo_ref[...])

    @pl.when(it == pl.num_programs(1) - 1)
    def _():
        stn_ref[...] = st_sc[...]
        pooln_ref[...] = ext_sc[pl.ds(0, POOL_HIST), :]


def _mix_out(x, proj_a, gates, y_att, st0, pool0, lb, pool_w, pool_scale, hg_outn,
             w_up_pool, w_up_att, w_up_hgrn, w_out, *, tm, p0):
    b, t, _ = x.shape
    kern = functools.partial(_mix_kernel, tm=tm, p0=p0)

    def rows(width, col):
        return pl.BlockSpec((None, tm, width), lambda b_, i: (b_, i, col))

    def const(shape):
        return pl.BlockSpec(shape, lambda b_, i: (0,) * len(shape))

    def per_batch(shape):
        return pl.BlockSpec((None,) + shape, lambda b_, i: (b_, 0, 0))

    return pl.pallas_call(
        kern,
        grid=(b, t // tm),
        in_specs=[
            rows(D_MODEL, 0),
            rows(SEG, 0),
            rows(SEG, 1),
            rows(SEG, 2),
            rows(SEG, 3),
            rows(3 * D_MODEL, 0),
            rows(SEG, 0),
            per_batch((HG_V, HG_W)),
            per_batch((POOL_HIST, SEG)),
            const((1, HG_W)), const((4, LANES, LANES)), const((1, SEG)), const((1, HG_DV)),
            const((SEG, D_MODEL)), const((SEG, D_MODEL)), const((SEG, D_MODEL)), const((D_MODEL, D_MODEL)),
        ],
        out_specs=[
            rows(D_MODEL, 0),
            per_batch((HG_V, HG_W)),
            per_batch((POOL_HIST, SEG)),
        ],
        out_shape=[
            jax.ShapeDtypeStruct((b, t, D_MODEL), F32),
            jax.ShapeDtypeStruct((b, HG_V, HG_W), F32),
            jax.ShapeDtypeStruct((b, POOL_HIST, SEG), F32),
        ],
        scratch_shapes=[
            pltpu.VMEM((HG_V, HG_W), F32),
            pltpu.VMEM((CHUNK + 2 * B_PAD, HG_W), F32),
            pltpu.VMEM((POOL_HIST + tm, SEG), F32),
            pltpu.VMEM((tm, HG_V), F32),
        ],
        compiler_params=_cparams(("parallel", "arbitrary")),
        name="mix_out",
    )(x, proj_a, proj_a, proj_a, proj_a, gates, y_att, st0, pool0,
      lb, pool_w, pool_scale, hg_outn, w_up_pool, w_up_att, w_up_hgrn, w_out)


def _state_to_block_diag(s):
    b = s.shape[0]
    st = jnp.swapaxes(s, 2, 3)
    eye = jnp.eye(HG_HEADS, dtype=s.dtype)
    return jnp.einsum('bhed,hg->bhegd', st, eye).reshape(b, HG_V, HG_W)


def _block_diag_to_state(st):
    b = st.shape[0]
    s5 = st.reshape(b, HG_HEADS, HG_DV, HG_HEADS, HG_DK)
    diag = jnp.stack([s5[:, h, :, h, :] for h in range(HG_HEADS)], axis=1)
    return jnp.swapaxes(diag, 2, 3)


def _head_rms(x, gain):
    return jnp.concatenate(
        [_rms(x[:, h * X_HD:(h + 1) * X_HD], gain) for h in range(X_HEADS)], axis=1)


def _memkv_kernel(m_ref, g_ref, w_ref, kn_ref, k_ref, v_ref, k16_ref, v16_ref):
    h = _rms(m_ref[...], g_ref[...]).astype(BF16)
    kv = _dot(h, w_ref[...])
    mk = _head_rms(kv[:, :D_MODEL], kn_ref[...])
    mv = kv[:, D_MODEL:]
    k_ref[...] = mk
    v_ref[...] = mv
    k16_ref[...] = mk.astype(BF16)
    v16_ref[...] = mv.astype(BF16)


def _memory_kv(mem, g, w_ckv, kn):
    b, n, _ = mem.shape
    blk = pl.BlockSpec((None, n, D_MODEL), lambda b_: (b_, 0, 0))
    return pl.pallas_call(
        _memkv_kernel,
        grid=(b,),
        in_specs=[
            blk,
            pl.BlockSpec((1, D_MODEL), lambda b_: (0, 0)),
            pl.BlockSpec((D_MODEL, 2 * D_MODEL), lambda b_: (0, 0)),
            pl.BlockSpec((1, X_HD), lambda b_: (0, 0)),
        ],
        out_specs=[blk, blk, blk, blk],
        out_shape=[jax.ShapeDtypeStruct((b, n, D_MODEL), F32)] * 2
        + [jax.ShapeDtypeStruct((b, n, D_MODEL), BF16)] * 2,
        compiler_params=_cparams(("parallel",)),
        name="memory_kv",
    )(mem, g.reshape(1, D_MODEL), w_ckv, kn.reshape(1, X_HD))


def _cross_kernel(x_ref, g_ref, wq_ref, qn_ref, mk_ref, mv_ref, wo_ref, o_ref):
    x = x_ref[...]
    q = _head_rms(_dot(_rms(x, g_ref[...]).astype(BF16), wq_ref[...]), qn_ref[...])
    q = (q * (X_HD ** -0.5)).astype(BF16)
    outs = []
    for h in range(X_HEADS):
        sl = slice(h * X_HD, (h + 1) * X_HD)
        s = _dot_nt(q[:, sl], mk_ref[:, sl])
        p = jnp.exp(s - jnp.max(s, axis=-1, keepdims=True))
        p = p / jnp.sum(p, axis=-1, keepdims=True)
        outs.append(_dot(p.astype(BF16), mv_ref[:, sl]))
    o = jnp.concatenate(outs, axis=1)
    o_ref[...] = x + _dot(o.astype(BF16), wo_ref[...])


def _cross_attend(x, mk, mv, g, w_cq, qn, w_co, *, tm):
    b, t, _ = x.shape
    n = mk.shape[1]
    rows = pl.BlockSpec((None, tm, D_MODEL), lambda b_, i: (b_, i, 0))
    mem = pl.BlockSpec((None, n, D_MODEL), lambda b_, i: (b_, 0, 0))
    wsq = pl.BlockSpec((D_MODEL, D_MODEL), lambda b_, i: (0, 0))
    return pl.pallas_call(
        _cross_kernel,
        grid=(b, t // tm),
        in_specs=[
            rows,
            pl.BlockSpec((1, D_MODEL), lambda b_, i: (0, 0)),
            wsq,
            pl.BlockSpec((1, X_HD), lambda b_, i: (0, 0)),
            mem, mem, wsq,
        ],
        out_specs=rows,
        out_shape=jax.ShapeDtypeStruct((b, t, D_MODEL), F32),
        compiler_params=_cparams(("parallel", "parallel")),
        name="cross_attn",
    )(x, g.reshape(1, D_MODEL), w_cq, qn.reshape(1, X_HD), mk, mv, w_co)


def _layer(x, l, p0, rope, past_kv, pool_prev, hg_state, lb, mk16, mv16, w, *, tiles):
    b, t, _ = x.shape
    n = b * t
    x2 = _ffn_half(x.reshape(n, D_MODEL), w['norm_ffn1'], w['w_ffn1_in'], w['w_ffn1_out'],
                   tm=tiles['ffn'], tf=tiles['tf'])

    proj_a, kf, vf, gates, qkv = _in_proj(x2, w['norm_mix'], w['w_in'], w['group_mean'], w['att_qn'],
                                          w['att_kn'], *rope, tm=tiles['proj'])
    k_rows = kf.reshape(b, t, ATT_HEADS, 2 * ATT_DK)
    v_rows = vf.reshape(b, t, ATT_HEADS, ATT_DV)

    lam_init = 0.8 - 0.6 * math.exp(-0.3 * l)
    lam = (jnp.exp(jnp.sum(w['lq1'] * w['lk1'])) - jnp.exp(jnp.sum(w['lq2'] * w['lk2'])) + lam_init)
    lam = jnp.full((1, ATT_DV), lam, F32)
    gsub = w['att_subln'].reshape(1, ATT_DV)
    qkv3 = qkv.reshape(b, t, 3 * SEG)
    if past_kv is None:
        bound = (ATT_DK ** 0.5) * jnp.max(jnp.abs(w['att_qn'])) * jnp.max(jnp.abs(w['att_kn']))
        y_att = lax.cond(
            bound <= SCORE_BOUND_MAX,
            lambda a: _attn_bounded(a, lam, gsub, 1.0 - lam_init, tq=tiles['attn_bounded']),
            lambda a: _attn_prompt(a, lam, gsub, 1.0 - lam_init, tq=tiles['attn']),
            qkv3)
    else:
        y_att = _attn_sample(qkv3, past_kv[0], past_kv[1], lam, gsub, 1.0 - lam_init)

    x3, st_new, pool_new = _mix_out(
        x2.reshape(b, t, D_MODEL), proj_a.reshape(b, t, N_SEG_P * SEG), gates.reshape(b, t, N_SEG_G * SEG),
        y_att, _state_to_block_diag(hg_state), pool_prev, lb,
        w['pool_w'], w['pool_scale'], w['hg_outn'], w['w_up_pool'], w['w_up_att'], w['w_up_hgrn'], w['w_out'],
        tm=tiles['mix'], p0=p0)

    x4 = _cross_attend(x3, mk16, mv16, w['norm_cross'], w['w_cq'], w['cross_qn'], w['w_co'], tm=tiles['cross'])
    x5 = _ffn_half(x4.reshape(n, D_MODEL), w['norm_ffn2'], w['w_ffn2_in'], w['w_ffn2_out'],
                   tm=tiles['ffn'], tf=tiles['tf'])
    return (x5.reshape(b, t, D_MODEL), k_rows, v_rows, pool_new[:, POOL_HIST - POOL_STATE:],
            _block_diag_to_state(st_new))


PROMPT_TILES = dict(ffn=1024, tf=256, proj=1024, attn=256, attn_bounded=512, mix=256, cross=512)
SAMPLE_TILES = dict(ffn=512, tf=256, proj=512, attn=64, mix=64, cross=64)


def kernel(x_prompt, x_sample, cache_attn_k, cache_attn_v, cache_mem_k, cache_mem_v, state_pool, state_hgrn, mem_prompt, norm_ffn1, w_ffn1_in, w_ffn1_out, norm_mix, w_in, pool_w, pool_scale, att_q_norm, att_k_norm, lambda_q1, lambda_k1, lambda_q2, lambda_k2, att_subln, hgrn_lower, hgrn_out_norm, w_up_pool, w_up_att, w_up_hgrn, w_out, norm_cross, norm_mem, w_cq, w_ckv, cross_q_norm, cross_k_norm, w_co, norm_ffn2, w_ffn2_in, w_ffn2_out):
    depth = w_in.shape[0]
    bp = x_prompt.shape[0]
    bs = x_sample.shape[0]
    p0_sample = cache_attn_k.shape[2]

    lp = jax.nn.softmax(hgrn_lower.astype(F32), axis=0)
    lbs = jnp.cumsum(lp, axis=0) - lp[0:1]

    gidx = jnp.arange(SEG) // ATT_DK
    group_mean = ((gidx[:, None] == gidx[None, :]).astype(F32) / ATT_DK).astype(BF16)

    def layer_weights(l):
        return dict(
            norm_ffn1=norm_ffn1[l], w_ffn1_in=w_ffn1_in[l].astype(BF16), w_ffn1_out=w_ffn1_out[l].astype(BF16),
            norm_mix=norm_mix[l], w_in=w_in[l].astype(BF16), group_mean=group_mean,
            att_qn=jnp.tile(att_q_norm[l], SEG // ATT_DK).reshape(1, SEG),
            att_kn=jnp.tile(att_k_norm[l], SEG // ATT_DK).reshape(1, SEG),
            lq1=lambda_q1[l].astype(F32), lk1=lambda_k1[l].astype(F32),
            lq2=lambda_q2[l].astype(F32), lk2=lambda_k2[l].astype(F32),
            att_subln=att_subln[l],
            pool_w=pool_w[l].astype(BF16), pool_scale=pool_scale[l].reshape(1, SEG),
            hg_outn=hgrn_out_norm[l].reshape(1, HG_DV),
            w_up_pool=w_up_pool[l].astype(BF16), w_up_att=w_up_att[l].astype(BF16),
            w_up_hgrn=w_up_hgrn[l].astype(BF16), w_out=w_out[l].astype(BF16),
            norm_cross=norm_cross[l], w_cq=w_cq[l].astype(BF16), cross_qn=cross_q_norm[l],
            w_co=w_co[l].astype(BF16),
            norm_ffn2=norm_ffn2[l], w_ffn2_in=w_ffn2_in[l].astype(BF16), w_ffn2_out=w_ffn2_out[l].astype(BF16),
        )

    weights = [layer_weights(l) for l in range(depth)]

    def rope_for(p0, t, tm):
        tabs = _rope_tables(p0, t)
        if t < tm:
            tabs = tuple(jnp.concatenate([a] * (tm // t), axis=0) for a in tabs)
        return tabs

    rope_prompt = rope_for(0, x_prompt.shape[1], PROMPT_TILES['proj'])
    rope_sample = rope_for(p0_sample, x_sample.shape[1], SAMPLE_TILES['proj'])

    y = x_prompt
    pk, pv, pmk, pmv, ppool, phg = [], [], [], [], [], []
    for l in range(depth):
        mk, mv, mk16, mv16 = _memory_kv(mem_prompt, norm_mem[l], w_ckv[l].astype(BF16), cross_k_norm[l])
        pool0 = jnp.zeros((bp, POOL_HIST, SEG), F32)
        hg0 = jnp.zeros((bp, HG_HEADS, HG_DK, HG_DV), F32)
        y, kr, vr, pn, hn = _layer(y, l, 0, rope_prompt, None, pool0, hg0, lbs[l].reshape(1, HG_W), mk16, mv16,
                                   weights[l], tiles=PROMPT_TILES)
        pk.append(kr); pv.append(vr)
        pmk.append(mk.reshape(bp, -1, X_HEADS, X_HD)); pmv.append(mv.reshape(bp, -1, X_HEADS, X_HD))
        ppool.append(pn); phg.append(hn)
    y_prompt = y

    y = x_sample
    sk, sv, spool, shg = [], [], [], []
    for l in range(depth):
        past = (cache_attn_k[l].reshape(bs, p0_sample, ATT_HEADS * 2 * ATT_DK).astype(BF16),
                cache_attn_v[l].reshape(bs, p0_sample, ATT_HEADS * ATT_DV).astype(BF16))
        pool0 = jnp.pad(state_pool[l], ((0, 0), (POOL_HIST - POOL_STATE, 0), (0, 0)))
        mk16 = cache_mem_k[l].reshape(bs, -1, D_MODEL).astype(BF16)
        mv16 = cache_mem_v[l].reshape(bs, -1, D_MODEL).astype(BF16)
        y, kr, vr, pn, hn = _layer(y, l, p0_sample, rope_sample, past, pool0, state_hgrn[l],
                                   lbs[l].reshape(1, HG_W), mk16, mv16, weights[l], tiles=SAMPLE_TILES)
        sk.append(kr); sv.append(vr); spool.append(pn); shg.append(hn)
    y_sample = y

    return (y_prompt, y_sample,
            jnp.stack(pk), jnp.stack(pv), jnp.stack(pmk), jnp.stack(pmv),
            jnp.stack(ppool), jnp.stack(phg),
            jnp.stack(sk), jnp.stack(sv), jnp.stack(spool), jnp.stack(shg))
```

```python
import functools
import math

import jax
import jax.numpy as jnp
from jax import lax
from jax.experimental import pallas as pl
from jax.experimental.pallas import tpu as pltpu

F32 = jnp.float32
BF16 = jnp.bfloat16

D_MODEL = 1024
CHUNK = 64
EPS = 1e-6
MASK_VALUE = -1e30
POOL_WINDOWS = (2, 4, 8, 16)
POOL_STATE = 15
POOL_HIST = 16
ATT_HEADS = 4
ATT_DK = 64
ATT_DV = 128
ROT_DIMS = 16
ROPE_THETA = 500000.0
HG_HEADS = 4
HG_DK = 64
HG_DV = 128
X_HEADS = 4
X_HD = 256
D_FF = 2816
SEG = 512
N_SEG_A = 7
N_SEG_G = 6
LANES = 128
VMEM_LIMIT = 56 * 1024 * 1024
SCORE_BOUND_MAX = 20.0
KEY_TILES_PER_ITER = 4


def _cparams(sem):
    return pltpu.CompilerParams(dimension_semantics=sem, vmem_limit_bytes=VMEM_LIMIT)


def _rms(x, g):
    ms = jnp.mean(x * x, axis=-1, keepdims=True)
    return x * lax.rsqrt(ms + EPS) * g


def _silu(x):
    return x * jax.nn.sigmoid(x)


def _dot(a, b):
    return jnp.dot(a, b, preferred_element_type=F32)


def _dot_nt(a, b):
    return lax.dot_general(a, b, (((1,), (1,)), ((), ())), preferred_element_type=F32)


def _dot_tn(a, b):
    return lax.dot_general(a, b, (((0,), (0,)), ((), ())), preferred_element_type=F32)


def _shr(x, pow2):
    return lax.shift_right_logical(x, jnp.int32(int(math.log2(pow2))))


def _ffn_kernel(x_ref, g_ref, wi_ref, wo_ref, o_ref, *, tf):
    x = x_ref[...]
    h = _rms(x, g_ref[...]).astype(BF16)
    acc = None
    for c in range(D_FF // tf):
        a = _dot(h, wi_ref[:, pl.ds(c * tf, tf)])
        b = _dot(h, wi_ref[:, pl.ds(D_FF + c * tf, tf)])
        part = _dot((_silu(a) * b).astype(BF16), wo_ref[pl.ds(c * tf, tf), :])
        acc = part if acc is None else acc + part
    o_ref[...] = x + 0.5 * acc


def _resident(shape):
    return pl.BlockSpec(shape, lambda *_: (0,) * len(shape), pipeline_mode=pl.Buffered(1))


def _ffn_half(x, g, w_i, w_o, *, tm, tf):
    n = x.shape[0]
    return pl.pallas_call(
        functools.partial(_ffn_kernel, tf=tf),
        grid=(n // tm,),
        in_specs=[
            pl.BlockSpec((tm, D_MODEL), lambda i: (i, 0)),
            _resident((1, D_MODEL)),
            _resident((D_MODEL, 2 * D_FF)),
            _resident((D_FF, D_MODEL)),
        ],
        out_specs=pl.BlockSpec((tm, D_MODEL), lambda i: (i, 0)),
        out_shape=jax.ShapeDtypeStruct((n, D_MODEL), F32),
        compiler_params=_cparams(("parallel",)),
        name="ffn_half",
    )(x, g.reshape(1, D_MODEL), w_i, w_o)


def _inproj_kernel(x_ref, g_ref, w_ref, gm_ref, qn_ref, kn_ref, cos_ref, sa_ref, sb_ref,
                   pa_ref, pb_ref, kf_ref, vf_ref, pg_ref, qkv_ref, *rest, tq):
    h_sc = rest[-1]
    qt_ref, vt_ref = rest[:2] if len(rest) == 3 else (None, None)
    j = pl.program_id(1)

    def store_transposed(ref, y):
        if ref is None:
            return
        yt = y.T.astype(BF16)
        for h in range(ATT_HEADS):
            for c in range(y.shape[0] // tq):
                ref[h, c] = yt[h * ATT_DV:(h + 1) * ATT_DV, c * tq:(c + 1) * tq]

    @pl.when(j == 0)
    def _():
        h_sc[...] = _rms(x_ref[...], g_ref[...]).astype(BF16)

    def proj():
        return _dot(h_sc[...], w_ref[...])

    @pl.when(jnp.logical_or(j == 1, j == 2))
    def _():
        y = proj()
        ms = _dot((y * y).astype(BF16), gm_ref[...])
        gain = jnp.where(j == 1, qn_ref[...], kn_ref[...])
        yn = y * lax.rsqrt(ms + EPS) * gain
        rep = SEG // LANES
        cos = jnp.concatenate([cos_ref[...]] * rep, axis=1)
        sa = jnp.concatenate([sa_ref[...]] * rep, axis=1)
        sb = jnp.concatenate([sb_ref[...]] * rep, axis=1)
        half = ROT_DIMS // 2
        rot = (yn * cos + pltpu.roll(yn, half, 1) * sa + pltpu.roll(yn, SEG - half, 1) * sb)

        @pl.when(j == 1)
        def _():
            q = rot * (ATT_DK ** -0.5)
            qkv_ref[...] = q.astype(BF16)
            store_transposed(qt_ref, q)

        @pl.when(j == 2)
        def _():
            kf_ref[...] = rot
            qkv_ref[...] = rot.astype(BF16)

    @pl.when(j == 3)
    def _():
        y = proj()
        vf_ref[...] = y
        qkv_ref[...] = y.astype(BF16)
        store_transposed(vt_ref, y)

    @pl.when(jnp.logical_or(j == 0, j == 4))
    def _():
        pa_ref[...] = proj()

    @pl.when(jnp.logical_or(j == 5, j == 6))
    def _():
        pb_ref[...] = proj().astype(BF16)

    @pl.when(j >= N_SEG_A)
    def _():
        pg_ref[...] = proj().astype(BF16)


def _in_proj(x, g, w, gm, qn, kn, cos, sa, sb, *, tm, seq_len, tq):
    n = x.shape[0]
    tab_blocks = cos.shape[0] // tm
    nseg = N_SEG_A + N_SEG_G
    tab_spec = pl.BlockSpec((tm, LANES), lambda i, j: (i % tab_blocks, 0))
    vec_spec = pl.BlockSpec((1, SEG), lambda i, j: (0, 0))
    row_spec = pl.BlockSpec((tm, SEG), lambda i, j: (i, 0))
    out_specs = [
        pl.BlockSpec((tm, SEG), lambda i, j: (i, jnp.clip(j - 3, 0, 1))),
        pl.BlockSpec((tm, SEG), lambda i, j: (i, jnp.clip(j - 5, 0, 1))),
        row_spec, row_spec,
        pl.BlockSpec((tm, SEG), lambda i, j: (i, jnp.maximum(j - N_SEG_A, 0))),
        pl.BlockSpec((tm, SEG), lambda i, j: (i, jnp.clip(j - 1, 0, 2))),
    ]
    out_shape = [
        jax.ShapeDtypeStruct((n, 2 * SEG), F32),
        jax.ShapeDtypeStruct((n, 2 * SEG), BF16),
        jax.ShapeDtypeStruct((n, SEG), F32),
        jax.ShapeDtypeStruct((n, SEG), F32),
        jax.ShapeDtypeStruct((n, N_SEG_G * SEG), BF16),
        jax.ShapeDtypeStruct((n, 3 * SEG), BF16),
    ]
    if tq is not None:
        per_seq = seq_len // tm
        t_spec = pl.BlockSpec((None, ATT_HEADS, tm // tq, ATT_DV, tq),
                              lambda i, j: (i // per_seq, 0, i % per_seq, 0, 0))
        t_shape = jax.ShapeDtypeStruct((n // seq_len, ATT_HEADS, seq_len // tq, ATT_DV, tq), BF16)
        out_specs += [t_spec, t_spec]
        out_shape += [t_shape, t_shape]
    return pl.pallas_call(
        functools.partial(_inproj_kernel, tq=tq),
        grid=(n // tm, nseg),
        in_specs=[
            pl.BlockSpec((tm, D_MODEL), lambda i, j: (i, 0)),
            pl.BlockSpec((1, D_MODEL), lambda i, j: (0, 0)),
            pl.BlockSpec((D_MODEL, SEG), lambda i, j: (0, j)),
            pl.BlockSpec((SEG, SEG), lambda i, j: (0, 0)),
            vec_spec, vec_spec, tab_spec, tab_spec, tab_spec,
        ],
        out_specs=out_specs,
        out_shape=out_shape,
        scratch_shapes=[pltpu.VMEM((tm, D_MODEL), BF16)],
        compiler_params=_cparams(("parallel", "arbitrary")),
        name="in_proj",
    )(x, g.reshape(1, D_MODEL), w, gm, qn, kn, cos, sa, sb)


def _rope_tables(p0, t):
    half = ROT_DIMS // 2
    inv = ROPE_THETA ** (-jnp.arange(half, dtype=F32) / half)
    pos = p0 + jnp.arange(t, dtype=jnp.int32)
    ang = pos.astype(F32)[:, None] * inv[None, :]
    cos, sin = jnp.cos(ang), jnp.sin(ang)
    ones = jnp.ones((t, ATT_DK - ROT_DIMS), F32)
    zeros = jnp.zeros((t, ATT_DK - ROT_DIMS), F32)
    zh = jnp.zeros((t, half), F32)
    c64 = jnp.concatenate([cos, cos, ones], axis=1)
    sa64 = jnp.concatenate([zh, sin, zeros], axis=1)
    sb64 = jnp.concatenate([-sin, zh, zeros], axis=1)
    rep = LANES // ATT_DK
    return tuple(jnp.concatenate([a] * rep, axis=1) for a in (c64, sa64, sb64))


def _stack_maps(q):
    lane = lax.broadcasted_iota(jnp.int32, q.shape, 1)
    zero = jnp.zeros_like(q)
    return jnp.concatenate([jnp.where(lane < ATT_DK, q, zero), jnp.where(lane >= ATT_DK, q, zero)], axis=0)


def _attn_finish(acc, l, lam, gsub, post_scale, tq):
    o = acc[:tq] / l[:tq] - lam * (acc[tq:] / l[tq:])
    return _rms(o, gsub) * post_scale


def _attn_prompt_kernel(lam_ref, gsub_ref, q_ref, k_ref, v_ref, o_ref, acc_sc, m_sc, l_sc, *, tq, post_scale):
    i = pl.program_id(2)
    qbd = _stack_maps(q_ref[...])
    m_sc[...] = jnp.full_like(m_sc, MASK_VALUE)
    l_sc[...] = jnp.zeros_like(l_sc)
    acc_sc[...] = jnp.zeros_like(acc_sc)

    def step(j, masked):
        off = pl.multiple_of(j * tq, tq)
        kj = k_ref[pl.ds(off, tq), :]
        vj = v_ref[pl.ds(off, tq), :]
        s = _dot_nt(qbd, kj)
        if masked:
            qi = lax.broadcasted_iota(jnp.int32, s.shape, 0) & (tq - 1)
            ki = lax.broadcasted_iota(jnp.int32, s.shape, 1)
            s = jnp.where(_shr(ki, CHUNK) <= _shr(qi, CHUNK), s, MASK_VALUE)
        m_prev = m_sc[...]
        m_new = jnp.maximum(m_prev, jnp.max(s, axis=-1, keepdims=True))
        alpha = jnp.exp(m_prev - m_new)
        p = jnp.exp(s - m_new)
        l_sc[...] = alpha * l_sc[...] + jnp.sum(p, axis=-1, keepdims=True)
        acc_sc[...] = alpha * acc_sc[...] + _dot(p.astype(BF16), vj)
        m_sc[...] = m_new

    def body(j, carry):
        step(j, False)
        return carry

    lax.fori_loop(0, i, body, 0)
    step(i, True)
    o_ref[...] = _attn_finish(acc_sc[...], l_sc[...], lam_ref[...], gsub_ref[...], post_scale, tq)


def _attn_prompt(qkv, lam, gsub, post_scale, *, tq):
    b, t, _ = qkv.shape
    kern = functools.partial(_attn_prompt_kernel, tq=tq, post_scale=post_scale)
    vec_spec = pl.BlockSpec((1, ATT_DV), lambda b_, h, i: (0, 0))
    return pl.pallas_call(
        kern,
        grid=(b, ATT_HEADS, t // tq),
        in_specs=[
            vec_spec, vec_spec,
            pl.BlockSpec((None, tq, ATT_DV), lambda b_, h, i: (b_, i, h)),
            pl.BlockSpec((None, t, ATT_DV), lambda b_, h, i: (b_, 0, ATT_HEADS + h)),
            pl.BlockSpec((None, t, ATT_DV), lambda b_, h, i: (b_, 0, 2 * ATT_HEADS + h)),
        ],
        out_specs=pl.BlockSpec((None, tq, ATT_DV), lambda b_, h, i: (b_, i, h)),
        out_shape=jax.ShapeDtypeStruct((b, t, ATT_HEADS * ATT_DV), F32),
        scratch_shapes=[pltpu.VMEM((2 * tq, ATT_DV), F32), pltpu.VMEM((2 * tq, 1), F32),
                        pltpu.VMEM((2 * tq, 1), F32)],
        compiler_params=_cparams(("parallel", "parallel", "arbitrary")),
        name="diff_attn_prompt",
    )(lam, gsub, qkv, qkv, qkv)


def _attn_bounded_kernel(lam_ref, gsub_ref, qt_ref, k_ref, vt_ref, o_ref, acc_sc, l_sc, *, tq, post_scale):
    i = pl.program_id(2)
    qt = qt_ref[...].astype(F32)
    row = lax.broadcasted_iota(jnp.int32, qt.shape, 0)
    qbd = jnp.concatenate([jnp.where(row < ATT_DK, qt, 0.0), jnp.where(row >= ATT_DK, qt, 0.0)],
                          axis=1).astype(BF16)
    acc_sc[...] = jnp.zeros_like(acc_sc)
    l_sc[...] = jnp.zeros_like(l_sc)

    def probs(j, masked):
        off = pl.multiple_of(j * tq, tq)
        s = _dot(k_ref[pl.ds(off, tq), :], qbd)
        if masked:
            ki = lax.broadcasted_iota(jnp.int32, s.shape, 0)
            qi = lax.broadcasted_iota(jnp.int32, s.shape, 1) & (tq - 1)
            s = jnp.where(_shr(ki, CHUNK) <= _shr(qi, CHUNK), s, MASK_VALUE)
        p = jnp.exp(s)
        return jnp.sum(p.reshape(tq // 8, 8, 2 * tq), axis=0), _dot(vt_ref[j], p.astype(BF16))

    def accumulate(first, count, masked=False):
        parts = [probs(first + c, masked) for c in range(count)]
        l_sc[...] += functools.reduce(lambda x, y: x + y, [p_[0] for p_ in parts])
        acc_sc[...] += functools.reduce(lambda x, y: x + y, [p_[1] for p_ in parts])

    def quad(jj, carry):
        accumulate(KEY_TILES_PER_ITER * jj, KEY_TILES_PER_ITER)
        return carry

    lax.fori_loop(0, i // KEY_TILES_PER_ITER, quad, 0)
    rem = i % KEY_TILES_PER_ITER

    @pl.when(rem >= 2)
    def _():
        accumulate(i - rem, 2)

    @pl.when(rem % 2 == 1)
    def _():
        accumulate(i - 1, 1)

    accumulate(i, 1, masked=True)
    l = jnp.sum(l_sc[...], axis=0, keepdims=True)
    acc = acc_sc[...]
    ot = acc[:, :tq] / l[:, :tq] - lam_ref[...] * (acc[:, tq:] / l[:, tq:])
    ms = jnp.mean(ot * ot, axis=0, keepdims=True)
    yt = ot * lax.rsqrt(ms + EPS) * (gsub_ref[...] * post_scale)
    o_ref[...] = yt.T


def _attn_bounded(qkv, q5, v5, lam, gsub, post_scale):
    b, t, _ = qkv.shape
    nt, tq = q5.shape[2], q5.shape[4]
    kern = functools.partial(_attn_bounded_kernel, tq=tq, post_scale=post_scale)
    return pl.pallas_call(
        kern,
        grid=(b, ATT_HEADS, nt),
        in_specs=[
            pl.BlockSpec((1, 1), lambda b_, h, i: (0, 0)),
            pl.BlockSpec((ATT_DV, 1), lambda b_, h, i: (0, 0)),
            pl.BlockSpec((None, None, None, ATT_DV, tq), lambda b_, h, i: (b_, h, i, 0, 0)),
            pl.BlockSpec((None, t, ATT_DV), lambda b_, h, i: (b_, 0, ATT_HEADS + h)),
            pl.BlockSpec((None, None, nt, ATT_DV, tq), lambda b_, h, i: (b_, h, 0, 0, 0)),
        ],
        out_specs=pl.BlockSpec((None, tq, ATT_DV), lambda b_, h, i: (b_, i, h)),
        out_shape=jax.ShapeDtypeStruct((b, t, ATT_HEADS * ATT_DV), F32),
        scratch_shapes=[pltpu.VMEM((ATT_DV, 2 * tq), F32), pltpu.VMEM((8, 2 * tq), F32)],
        compiler_params=_cparams(("parallel", "parallel", "arbitrary")),
        name="diff_attn_bounded",
    )(lam[:, :1], gsub.reshape(ATT_DV, 1), q5, qkv, v5)


def _attn_sample_kernel(lam_ref, gsub_ref, q_ref, kn_ref, vn_ref, kp_ref, vp_ref, o_ref, *, tq, post_scale):
    qbd = _stack_maps(q_ref[...])
    sp = _dot_nt(qbd, kp_ref[...])
    sn = _dot_nt(qbd, kn_ref[...])
    m = jnp.maximum(jnp.max(sp, axis=-1, keepdims=True), jnp.max(sn, axis=-1, keepdims=True))
    pp = jnp.exp(sp - m)
    pn = jnp.exp(sn - m)
    l = jnp.sum(pp, axis=-1, keepdims=True) + jnp.sum(pn, axis=-1, keepdims=True)
    acc = _dot(pp.astype(BF16), vp_ref[...]) + _dot(pn.astype(BF16), vn_ref[...])
    o_ref[...] = _attn_finish(acc, l, lam_ref[...], gsub_ref[...], post_scale, tq)


def _attn_sample(qkv, kp, vp, lam, gsub, post_scale):
    b, t, _ = qkv.shape
    tp = kp.shape[1]
    kern = functools.partial(_attn_sample_kernel, tq=t, post_scale=post_scale)
    vec_spec = pl.BlockSpec((1, ATT_DV), lambda b_, h: (0, 0))
    return pl.pallas_call(
        kern,
        grid=(b, ATT_HEADS),
        in_specs=[
            vec_spec, vec_spec,
            pl.BlockSpec((None, t, ATT_DV), lambda b_, h: (b_, 0, h)),
            pl.BlockSpec((None, t, ATT_DV), lambda b_, h: (b_, 0, ATT_HEADS + h)),
            pl.BlockSpec((None, t, ATT_DV), lambda b_, h: (b_, 0, 2 * ATT_HEADS + h)),
            pl.BlockSpec((None, tp, ATT_DV), lambda b_, h: (b_, 0, h)),
            pl.BlockSpec((None, tp, ATT_DV), lambda b_, h: (b_, 0, h)),
        ],
        out_specs=pl.BlockSpec((None, t, ATT_DV), lambda b_, h: (b_, 0, h)),
        out_shape=jax.ShapeDtypeStruct((b, t, ATT_HEADS * ATT_DV), F32),
        compiler_params=_cparams(("parallel", "parallel")),
        name="diff_attn_sample",
    )(lam, gsub, qkv, qkv, qkv, kp, vp)


HG_W = HG_HEADS * HG_DK
HG_V = HG_HEADS * HG_DV
B_PAD = 8


def _level_reference(b_sc, w):
    if w >= 4:
        pieces = []
        for p in range(CHUNK // (2 * w)):
            row = b_sc[pl.ds(B_PAD + p * 2 * w + w, 1), :]
            pieces.append(jnp.broadcast_to(row, (2 * w, HG_W)))
        return jnp.concatenate(pieces, axis=0)
    t = lax.broadcasted_iota(jnp.int32, (CHUNK, HG_W), 0)
    phase = t & (2 * w - 1)
    r = None
    for ph in range(2 * w):
        shifted = b_sc[pl.ds(B_PAD + w - ph, CHUNK), :]
        r = shifted if r is None else jnp.where(phase == ph, shifted, r)
    return r


def _hgrn_chunk(hq, hf, hi, lb, st_sc, b_sc, masks):
    tril, bd_k, bd_v, bd_s = masks
    f = lb + (1.0 - lb) * jax.nn.sigmoid(hf)
    g = jnp.log(f)
    kk = 1.0 - f
    q = _silu(hq) * (HG_DK ** -0.5)
    g0 = g.astype(BF16)
    r1 = g - g0.astype(F32)
    g1 = r1.astype(BF16)
    g2 = (r1 - g1.astype(F32)).astype(BF16)
    b = _dot(tril, g0) + _dot(tril, g1) + _dot(tril, g2)
    b_sc[pl.ds(B_PAD, CHUNK), :] = b

    t_idx = lax.broadcasted_iota(jnp.int32, (CHUNK, HG_W), 0)
    s_idx = lax.broadcasted_iota(jnp.int32, (CHUNK, HG_W), 1) & (CHUNK - 1)
    zero = jnp.zeros((CHUNK, HG_W), F32)

    def block_diag_k(x):
        return jnp.where(bd_k, jnp.concatenate([x] * HG_HEADS, axis=0), jnp.zeros((), F32)).astype(BF16)

    a = jnp.where(t_idx == s_idx, _dot_nt(q.astype(BF16), block_diag_k(kk)), zero)
    w = CHUNK // 2
    while w >= 1:
        r = _level_reference(b_sc, w)
        upper = (t_idx & w) != 0
        e = jnp.exp(jnp.where(upper, b - r, r - b))
        ql = jnp.where(upper, q * e, zero)
        kl = jnp.where(upper, zero, kk * e)
        same_pair = _shr(t_idx, 2 * w) == _shr(s_idx, 2 * w)
        a = a + jnp.where(same_pair, _dot_nt(ql.astype(BF16), block_diag_k(kl)), zero)
        w //= 2

    b_last = jnp.broadcast_to(b_sc[pl.ds(B_PAD + CHUNK - 1, 1), :], (CHUNK, HG_W))
    qb = q * jnp.exp(b)
    kd = kk * jnp.exp(b_last - b)
    v16 = hi
    vbd = jnp.where(bd_v, jnp.concatenate([hi.astype(F32)] * HG_HEADS, axis=0), jnp.zeros((), F32)).astype(BF16)
    st = st_sc[...]
    o = _dot(a.astype(BF16), vbd) + _dot_nt(qb.astype(BF16), st.astype(BF16))
    upd = _dot_tn(v16, kd.astype(BF16))
    decay = jnp.exp(b_sc[pl.ds(B_PAD + CHUNK - 1, 1), :])
    st_sc[...] = st * decay + jnp.where(bd_s, upd, jnp.zeros((), F32))
    return o


def _mix_kernel(x_ref, u_ref, hqf_ref, hi_ref, hg_ref, g_ref, ya_ref, st0_ref, pool0_ref,
                lb_ref, pw_ref, ps_ref, hgn_ref, wup_ref, wua_ref, wuh_ref, wo_ref,
                xo_ref, stn_ref, pooln_ref,
                st_sc, b_sc, ext_sc, yh_sc, *, tm, p0):
    it = pl.program_id(1)

    @pl.when(it == 0)
    def _():
        st_sc[...] = st0_ref[...]
        ext_sc[pl.ds(0, POOL_HIST), :] = pool0_ref[...]
        b_sc[...] = jnp.zeros_like(b_sc)

    u = u_ref[...]
    ext_sc[pl.ds(POOL_HIST, tm), :] = u
    row = lax.broadcasted_iota(jnp.int32, (tm, LANES), 0)
    seen = (p0 + 1 + it * tm + row).astype(F32)
    cols = []
    for gi, w in enumerate(POOL_WINDOWS):
        sl = pl.ds(gi * LANES, LANES)
        win = u[:, gi * LANES:(gi + 1) * LANES]
        for d in range(1, w):
            win = win + ext_sc[pl.ds(POOL_HIST - d, tm), sl]
        dlt = win / jnp.minimum(seen, float(w)) - u[:, gi * LANES:(gi + 1) * LANES]
        cols.append(_dot(dlt.astype(BF16), pw_ref[gi]))
    y_pool = jnp.concatenate(cols, axis=1) * ps_ref[...]
    ext_sc[pl.ds(0, POOL_HIST), :] = ext_sc[pl.ds(tm, POOL_HIST), :]

    tri_r = lax.broadcasted_iota(jnp.int32, (CHUNK, CHUNK), 0)
    tri_c = lax.broadcasted_iota(jnp.int32, (CHUNK, CHUNK), 1)
    tril = (tri_c <= tri_r).astype(BF16)
    rk = lax.broadcasted_iota(jnp.int32, (HG_W, HG_W), 0)
    ck = lax.broadcasted_iota(jnp.int32, (HG_W, HG_W), 1)
    bd_k = _shr(rk, CHUNK) == _shr(ck, HG_DK)
    rv = lax.broadcasted_iota(jnp.int32, (HG_W, HG_V), 0)
    cv = lax.broadcasted_iota(jnp.int32, (HG_W, HG_V), 1)
    bd_v = _shr(rv, CHUNK) == _shr(cv, HG_DV)
    rs = lax.broadcasted_iota(jnp.int32, (HG_V, HG_W), 0)
    cs = lax.broadcasted_iota(jnp.int32, (HG_V, HG_W), 1)
    bd_s = _shr(rs, HG_DV) == _shr(cs, HG_DK)
    masks = (tril, bd_k, bd_v, bd_s)
    lb = lb_ref[...]
    for c in range(tm // CHUNK):
        rows = pl.ds(c * CHUNK, CHUNK)
        o = _hgrn_chunk(hqf_ref[rows, pl.ds(0, HG_W)], hqf_ref[rows, pl.ds(HG_W, HG_W)],
                        hi_ref[rows, :], lb, st_sc, b_sc, masks)
        yh_sc[rows, :] = o
    oh = yh_sc[...]
    hg = hg_ref[...].astype(F32)
    heads = []
    for h in range(HG_HEADS):
        sl = slice(h * HG_DV, (h + 1) * HG_DV)
        heads.append(_rms(oh[:, sl], hgn_ref[...]) * _silu(hg[:, sl]))
    y_hg = jnp.concatenate(heads, axis=1)

    gts = g_ref[...].astype(F32)
    merged = (jax.nn.sigmoid(gts[:, :D_MODEL]) * _dot(y_pool.astype(BF16), wup_ref[...])
              + jax.nn.sigmoid(gts[:, D_MODEL:2 * D_MODEL]) * _dot(ya_ref[...].astype(BF16), wua_ref[...])
              + jax.nn.sigmoid(gts[:, 2 * D_MODEL:]) * _dot(y_hg.astype(BF16), wuh_ref[...]))
    xo_ref[...] = x_ref[...] + _dot(merged.astype(BF16), wo_ref[...])

    @pl.when(it == pl.num_programs(1) - 1)
    def _():
        stn_ref[...] = st_sc[...]
        pooln_ref[...] = ext_sc[pl.ds(0, POOL_HIST), :]


def _mix_out(x, proj_a, proj_b, gates, y_att, st0, pool0, lb, pool_w, pool_scale, hg_outn,
             w_up_pool, w_up_att, w_up_hgrn, w_out, *, tm, p0):
    b, t, _ = x.shape
    kern = functools.partial(_mix_kernel, tm=tm, p0=p0)

    def rows(width, col):
        return pl.BlockSpec((None, tm, width), lambda b_, i: (b_, i, col))

    def const(shape):
        return pl.BlockSpec(shape, lambda b_, i: (0,) * len(shape))

    def per_batch(shape):
        return pl.BlockSpec((None,) + shape, lambda b_, i: (b_, 0, 0))

    return pl.pallas_call(
        kern,
        grid=(b, t // tm),
        in_specs=[
            rows(D_MODEL, 0),
            rows(SEG, 0),
            rows(SEG, 1),
            rows(SEG, 0),
            rows(SEG, 1),
            rows(3 * D_MODEL, 0),
            rows(SEG, 0),
            per_batch((HG_V, HG_W)),
            per_batch((POOL_HIST, SEG)),
            const((1, HG_W)), const((4, LANES, LANES)), const((1, SEG)), const((1, HG_DV)),
            const((SEG, D_MODEL)), const((SEG, D_MODEL)), const((SEG, D_MODEL)), const((D_MODEL, D_MODEL)),
        ],
        out_specs=[
            rows(D_MODEL, 0),
            per_batch((HG_V, HG_W)),
            per_batch((POOL_HIST, SEG)),
        ],
        out_shape=[
            jax.ShapeDtypeStruct((b, t, D_MODEL), F32),
            jax.ShapeDtypeStruct((b, HG_V, HG_W), F32),
            jax.ShapeDtypeStruct((b, POOL_HIST, SEG), F32),
        ],
        scratch_shapes=[
            pltpu.VMEM((HG_V, HG_W), F32),
            pltpu.VMEM((CHUNK + 2 * B_PAD, HG_W), F32),
            pltpu.VMEM((POOL_HIST + tm, SEG), F32),
            pltpu.VMEM((tm, HG_V), F32),
        ],
        compiler_params=_cparams(("parallel", "arbitrary")),
        name="mix_out",
    )(x, proj_a, proj_a, proj_b, proj_b, gates, y_att, st0, pool0,
      lb, pool_w, pool_scale, hg_outn, w_up_pool, w_up_att, w_up_hgrn, w_out)


def _state_to_block_diag(s):
    b = s.shape[0]
    st = jnp.swapaxes(s, 2, 3)
    eye = jnp.eye(HG_HEADS, dtype=s.dtype)
    return jnp.einsum('bhed,hg->bhegd', st, eye).reshape(b, HG_V, HG_W)


def _block_diag_to_state(st):
    b = st.shape[0]
    s5 = st.reshape(b, HG_HEADS, HG_DV, HG_HEADS, HG_DK)
    diag = jnp.stack([s5[:, h, :, h, :] for h in range(HG_HEADS)], axis=1)
    return jnp.swapaxes(diag, 2, 3)


def _head_rms(x, gain):
    return jnp.concatenate(
        [_rms(x[:, h * X_HD:(h + 1) * X_HD], gain) for h in range(X_HEADS)], axis=1)


def _memkv_kernel(m_ref, g_ref, w_ref, kn_ref, k_ref, v_ref, k16_ref, v16_ref):
    h = _rms(m_ref[...], g_ref[...]).astype(BF16)
    kv = _dot(h, w_ref[...])
    mk = _head_rms(kv[:, :D_MODEL], kn_ref[...])
    mv = kv[:, D_MODEL:]
    k_ref[...] = mk
    v_ref[...] = mv
    k16_ref[...] = mk.astype(BF16)
    v16_ref[...] = mv.astype(BF16)


def _memory_kv(mem, g, w_ckv, kn):
    b, n, _ = mem.shape
    blk = pl.BlockSpec((None, n, D_MODEL), lambda b_: (b_, 0, 0))
    return pl.pallas_call(
        _memkv_kernel,
        grid=(b,),
        in_specs=[
            blk,
            pl.BlockSpec((1, D_MODEL), lambda b_: (0, 0)),
            pl.BlockSpec((D_MODEL, 2 * D_MODEL), lambda b_: (0, 0)),
            pl.BlockSpec((1, X_HD), lambda b_: (0, 0)),
        ],
        out_specs=[blk, blk, blk, blk],
        out_shape=[jax.ShapeDtypeStruct((b, n, D_MODEL), F32)] * 2
        + [jax.ShapeDtypeStruct((b, n, D_MODEL), BF16)] * 2,
        compiler_params=_cparams(("parallel",)),
        name="memory_kv",
    )(mem, g.reshape(1, D_MODEL), w_ckv, kn.reshape(1, X_HD))


def _cross_kernel(x_ref, g_ref, wq_ref, qn_ref, mk_ref, mv_ref, wo_ref, o_ref):
    x = x_ref[...]
    q = _head_rms(_dot(_rms(x, g_ref[...]).astype(BF16), wq_ref[...]), qn_ref[...])
    q = (q * (X_HD ** -0.5)).astype(BF16)
    outs = []
    for h in range(X_HEADS):
        sl = slice(h * X_HD, (h + 1) * X_HD)
        s = _dot_nt(q[:, sl], mk_ref[:, sl])
        p = jnp.exp(s - jnp.max(s, axis=-1, keepdims=True))
        p = p / jnp.sum(p, axis=-1, keepdims=True)
        outs.append(_dot(p.astype(BF16), mv_ref[:, sl]))
    o = jnp.concatenate(outs, axis=1)
    o_ref[...] = x + _dot(o.astype(BF16), wo_ref[...])


def _cross_attend(x, mk, mv, g, w_cq, qn, w_co, *, tm):
    b, t, _ = x.shape
    n = mk.shape[1]
    rows = pl.BlockSpec((None, tm, D_MODEL), lambda b_, i: (b_, i, 0))
    mem = pl.BlockSpec((None, n, D_MODEL), lambda b_, i: (b_, 0, 0))
    wsq = pl.BlockSpec((D_MODEL, D_MODEL), lambda b_, i: (0, 0))
    return pl.pallas_call(
        _cross_kernel,
        grid=(b, t // tm),
        in_specs=[
            rows,
            pl.BlockSpec((1, D_MODEL), lambda b_, i: (0, 0)),
            wsq,
            pl.BlockSpec((1, X_HD), lambda b_, i: (0, 0)),
            mem, mem, wsq,
        ],
        out_specs=rows,
        out_shape=jax.ShapeDtypeStruct((b, t, D_MODEL), F32),
        compiler_params=_cparams(("parallel", "parallel")),
        name="cross_attn",
    )(x, g.reshape(1, D_MODEL), w_cq, qn.reshape(1, X_HD), mk, mv, w_co)


def _layer(x, l, p0, rope, past_kv, pool_prev, hg_state, lb, mk16, mv16, w, *, tiles):
    b, t, _ = x.shape
    n = b * t
    x2 = _ffn_half(x.reshape(n, D_MODEL), w['norm_ffn1'], w['w_ffn1_in'], w['w_ffn1_out'],
                   tm=tiles['ffn'], tf=tiles['tf'])

    proj_a, proj_b, kf, vf, gates, qkv, *tiles_t = _in_proj(
        x2, w['norm_mix'], w['w_in'], w['group_mean'], w['att_qn'], w['att_kn'], *rope,
        tm=tiles['proj'], seq_len=t, tq=None if past_kv is not None else tiles['attn_bounded'])
    k_rows = kf.reshape(b, t, ATT_HEADS, 2 * ATT_DK)
    v_rows = vf.reshape(b, t, ATT_HEADS, ATT_DV)

    lam_init = 0.8 - 0.6 * math.exp(-0.3 * l)
    lam = (jnp.exp(jnp.sum(w['lq1'] * w['lk1'])) - jnp.exp(jnp.sum(w['lq2'] * w['lk2'])) + lam_init)
    lam = jnp.full((1, ATT_DV), lam, F32)
    gsub = w['att_subln'].reshape(1, ATT_DV)
    qkv3 = qkv.reshape(b, t, 3 * SEG)
    if past_kv is None:
        bound = (ATT_DK ** 0.5) * jnp.max(jnp.abs(w['att_qn'])) * jnp.max(jnp.abs(w['att_kn']))
        y_att = lax.cond(
            bound <= SCORE_BOUND_MAX,
            lambda a, q5, v5: _attn_bounded(a, q5, v5, lam, gsub, 1.0 - lam_init),
            lambda a, q5, v5: _attn_prompt(a, lam, gsub, 1.0 - lam_init, tq=tiles['attn']),
            qkv3, *tiles_t)
    else:
        y_att = _attn_sample(qkv3, past_kv[0], past_kv[1], lam, gsub, 1.0 - lam_init)

    x3, st_new, pool_new = _mix_out(
        x2.reshape(b, t, D_MODEL), proj_a.reshape(b, t, 2 * SEG), proj_b.reshape(b, t, 2 * SEG),
        gates.reshape(b, t, N_SEG_G * SEG),
        y_att, _state_to_block_diag(hg_state), pool_prev, lb,
        w['pool_w'], w['pool_scale'], w['hg_outn'], w['w_up_pool'], w['w_up_att'], w['w_up_hgrn'], w['w_out'],
        tm=tiles['mix'], p0=p0)

    x4 = _cross_attend(x3, mk16, mv16, w['norm_cross'], w['w_cq'], w['cross_qn'], w['w_co'], tm=tiles['cross'])
    x5 = _ffn_half(x4.reshape(n, D_MODEL), w['norm_ffn2'], w['w_ffn2_in'], w['w_ffn2_out'],
                   tm=tiles['ffn'], tf=tiles['tf'])
    return (x5.reshape(b, t, D_MODEL), k_rows, v_rows, pool_new[:, POOL_HIST - POOL_STATE:],
            _block_diag_to_state(st_new))


PROMPT_TILES = dict(ffn=1024, tf=256, proj=1024, attn=256, attn_bounded=512, mix=256, cross=512)
SAMPLE_TILES = dict(ffn=512, tf=256, proj=512, attn=64, mix=64, cross=64)


def kernel(x_prompt, x_sample, cache_attn_k, cache_attn_v, cache_mem_k, cache_mem_v, state_pool, state_hgrn, mem_prompt, norm_ffn1, w_ffn1_in, w_ffn1_out, norm_mix, w_in, pool_w, pool_scale, att_q_norm, att_k_norm, lambda_q1, lambda_k1, lambda_q2, lambda_k2, att_subln, hgrn_lower, hgrn_out_norm, w_up_pool, w_up_att, w_up_hgrn, w_out, norm_cross, norm_mem, w_cq, w_ckv, cross_q_norm, cross_k_norm, w_co, norm_ffn2, w_ffn2_in, w_ffn2_out):
    depth = w_in.shape[0]
    bp = x_prompt.shape[0]
    bs = x_sample.shape[0]
    p0_sample = cache_attn_k.shape[2]

    lp = jax.nn.softmax(hgrn_lower.astype(F32), axis=0)
    lbs = jnp.cumsum(lp, axis=0) - lp[0:1]

    gidx = jnp.arange(SEG) // ATT_DK
    group_mean = ((gidx[:, None] == gidx[None, :]).astype(F32) / ATT_DK).astype(BF16)

    def layer_weights(l):
        return dict(
            norm_ffn1=norm_ffn1[l], w_ffn1_in=w_ffn1_in[l].astype(BF16), w_ffn1_out=w_ffn1_out[l].astype(BF16),
            norm_mix=norm_mix[l], w_in=w_in[l].astype(BF16), group_mean=group_mean,
            att_qn=jnp.tile(att_q_norm[l], SEG // ATT_DK).reshape(1, SEG),
            att_kn=jnp.tile(att_k_norm[l], SEG // ATT_DK).reshape(1, SEG),
            lq1=lambda_q1[l].astype(F32), lk1=lambda_k1[l].astype(F32),
            lq2=lambda_q2[l].astype(F32), lk2=lambda_k2[l].astype(F32),
            att_subln=att_subln[l],
            pool_w=pool_w[l].astype(BF16), pool_scale=pool_scale[l].reshape(1, SEG),
            hg_outn=hgrn_out_norm[l].reshape(1, HG_DV),
            w_up_pool=w_up_pool[l].astype(BF16), w_up_att=w_up_att[l].astype(BF16),
            w_up_hgrn=w_up_hgrn[l].astype(BF16), w_out=w_out[l].astype(BF16),
            norm_cross=norm_cross[l], w_cq=w_cq[l].astype(BF16), cross_qn=cross_q_norm[l],
            w_co=w_co[l].astype(BF16),
            norm_ffn2=norm_ffn2[l], w_ffn2_in=w_ffn2_in[l].astype(BF16), w_ffn2_out=w_ffn2_out[l].astype(BF16),
        )

    weights = [layer_weights(l) for l in range(depth)]

    def rope_for(p0, t, tm):
        tabs = _rope_tables(p0, t)
        if t < tm:
            tabs = tuple(jnp.concatenate([a] * (tm // t), axis=0) for a in tabs)
        return tabs

    rope_prompt = rope_for(0, x_prompt.shape[1], PROMPT_TILES['proj'])
    rope_sample = rope_for(p0_sample, x_sample.shape[1], SAMPLE_TILES['proj'])

    y = x_prompt
    pk, pv, pmk, pmv, ppool, phg = [], [], [], [], [], []
    for l in range(depth):
        mk, mv, mk16, mv16 = _memory_kv(mem_prompt, norm_mem[l], w_ckv[l].astype(BF16), cross_k_norm[l])
        pool0 = jnp.zeros((bp, POOL_HIST, SEG), F32)
        hg0 = jnp.zeros((bp, HG_HEADS, HG_DK, HG_DV), F32)
        y, kr, vr, pn, hn = _layer(y, l, 0, rope_prompt, None, pool0, hg0, lbs[l].reshape(1, HG_W), mk16, mv16,
                                   weights[l], tiles=PROMPT_TILES)
        pk.append(kr); pv.append(vr)
        pmk.append(mk.reshape(bp, -1, X_HEADS, X_HD)); pmv.append(mv.reshape(bp, -1, X_HEADS, X_HD))
        ppool.append(pn); phg.append(hn)
    y_prompt = y

    y = x_sample
    sk, sv, spool, shg = [], [], [], []
    for l in range(depth):
        past = (cache_attn_k[l].reshape(bs, p0_sample, ATT_HEADS * 2 * ATT_DK).astype(BF16),
                cache_attn_v[l].reshape(bs, p0_sample, ATT_HEADS * ATT_DV).astype(BF16))
        pool0 = jnp.pad(state_pool[l], ((0, 0), (POOL_HIST - POOL_STATE, 0), (0, 0)))
        mk16 = cache_mem_k[l].reshape(bs, -1, D_MODEL).astype(BF16)
        mv16 = cache_mem_v[l].reshape(bs, -1, D_MODEL).astype(BF16)
        y, kr, vr, pn, hn = _layer(y, l, p0_sample, rope_sample, past, pool0, state_hgrn[l],
                                   lbs[l].reshape(1, HG_W), mk16, mv16, weights[l], tiles=SAMPLE_TILES)
        sk.append(kr); sv.append(vr); spool.append(pn); shg.append(hn)
    y_sample = y

    return (y_prompt, y_sample,
            jnp.stack(pk), jnp.stack(pv), jnp.stack(pmk), jnp.stack(pmv),
            jnp.stack(ppool), jnp.stack(phg),
            jnp.stack(sk), jnp.stack(sv), jnp.stack(spool), jnp.stack(shg))
```

```python
import functools
import math

import jax
import jax.numpy as jnp
from jax import lax
from jax.experimental import pallas as pl
from jax.experimental.pallas import tpu as pltpu

F32 = jnp.float32
BF16 = jnp.bfloat16

D_MODEL = 1024
CHUNK = 64
EPS = 1e-6
MASK_VALUE = -1e30
POOL_WINDOWS = (2, 4, 8, 16)
POOL_STATE = 15
POOL_HIST = 16
ATT_HEADS = 4
ATT_DK = 64
ATT_DV = 128
ROT_DIMS = 16
ROPE_THETA = 500000.0
HG_HEADS = 4
HG_DK = 64
HG_DV = 128
X_HEADS = 4
X_HD = 256
D_FF = 2816
SEG = 512
N_SEG_A = 7
N_SEG_G = 6
LANES = 128
VMEM_LIMIT = 56 * 1024 * 1024
SCORE_BOUND_MAX = 20.0
KEY_TILES_PER_ITER = 4


def _cparams(sem):
    return pltpu.CompilerParams(dimension_semantics=sem, vmem_limit_bytes=VMEM_LIMIT)


def _rms(x, g):
    ms = jnp.mean(x * x, axis=-1, keepdims=True)
    return x * lax.rsqrt(ms + EPS) * g


def _sigmoid(x):
    return 0.5 * jnp.tanh(0.5 * x) + 0.5


def _silu(x):
    return x * _sigmoid(x)


def _dot(a, b):
    return jnp.dot(a, b, preferred_element_type=F32)


def _dot_nt(a, b):
    return lax.dot_general(a, b, (((1,), (1,)), ((), ())), preferred_element_type=F32)


def _dot_tn(a, b):
    return lax.dot_general(a, b, (((0,), (0,)), ((), ())), preferred_element_type=F32)


def _shr(x, pow2):
    return lax.shift_right_logical(x, jnp.int32(int(math.log2(pow2))))


def _ffn_kernel(x_ref, g_ref, wi_ref, wo_ref, o_ref, *, tf):
    x = x_ref[...]
    h = _rms(x, g_ref[...]).astype(BF16)
    acc = None
    for c in range(D_FF // tf):
        a = _dot(h, wi_ref[:, pl.ds(c * tf, tf)])
        b = _dot(h, wi_ref[:, pl.ds(D_FF + c * tf, tf)])
        part = _dot((_silu(a) * b).astype(BF16), wo_ref[pl.ds(c * tf, tf), :])
        acc = part if acc is None else acc + part
    o_ref[...] = x + 0.5 * acc


def _resident(shape):
    return pl.BlockSpec(shape, lambda *_: (0,) * len(shape), pipeline_mode=pl.Buffered(1))


def _ffn_half(x, g, w_i, w_o, *, tm, tf):
    n = x.shape[0]
    return pl.pallas_call(
        functools.partial(_ffn_kernel, tf=tf),
        grid=(n // tm,),
        in_specs=[
            pl.BlockSpec((tm, D_MODEL), lambda i: (i, 0)),
            _resident((1, D_MODEL)),
            _resident((D_MODEL, 2 * D_FF)),
            _resident((D_FF, D_MODEL)),
        ],
        out_specs=pl.BlockSpec((tm, D_MODEL), lambda i: (i, 0)),
        out_shape=jax.ShapeDtypeStruct((n, D_MODEL), F32),
        compiler_params=_cparams(("parallel",)),
        name="ffn_half",
    )(x, g.reshape(1, D_MODEL), w_i, w_o)


def _inproj_kernel(x_ref, g_ref, w_ref, gm_ref, qn_ref, kn_ref, cos_ref, sa_ref, sb_ref, *rest, tq, chained):
    rest = rest[2:] if chained else rest
    pa_ref, pb_ref, kf_ref, vf_ref, pg_ref, qkv_ref = rest[:6]
    qt_ref, vt_ref = rest[6:8] if tq is not None else (None, None)
    h_sc = rest[-1]
    j = pl.program_id(1)

    def store_head_rows(ref, y):
        for h in range(ATT_HEADS):
            ref[pl.ds(h, y.shape[0], stride=ATT_HEADS), :] = y[:, h * ATT_DV:(h + 1) * ATT_DV]

    def store_transposed(ref, y):
        if ref is None:
            return
        yt = y.T.astype(BF16)
        for h in range(ATT_HEADS):
            for c in range(y.shape[0] // tq):
                ref[h, c] = yt[h * ATT_DV:(h + 1) * ATT_DV, c * tq:(c + 1) * tq]

    @pl.when(j == 0)
    def _():
        h_sc[...] = _rms(x_ref[...], g_ref[...]).astype(BF16)

    def proj():
        return _dot(h_sc[...], w_ref[...])

    @pl.when(jnp.logical_or(j == 1, j == 2))
    def _():
        y = proj()
        ms = _dot((y * y).astype(BF16), gm_ref[...])
        gain = jnp.where(j == 1, qn_ref[...], kn_ref[...])
        yn = y * lax.rsqrt(ms + EPS) * gain
        rep = SEG // LANES
        cos = jnp.concatenate([cos_ref[...]] * rep, axis=1)
        sa = jnp.concatenate([sa_ref[...]] * rep, axis=1)
        sb = jnp.concatenate([sb_ref[...]] * rep, axis=1)
        half = ROT_DIMS // 2
        rot = (yn * cos + pltpu.roll(yn, half, 1) * sa + pltpu.roll(yn, SEG - half, 1) * sb)

        @pl.when(j == 1)
        def _():
            q = rot * (ATT_DK ** -0.5)
            qkv_ref[...] = q.astype(BF16)
            store_transposed(qt_ref, q)

        @pl.when(j == 2)
        def _():
            store_head_rows(kf_ref, rot)
            qkv_ref[...] = rot.astype(BF16)

    @pl.when(j == 3)
    def _():
        y = proj()
        store_head_rows(vf_ref, y)
        qkv_ref[...] = y.astype(BF16)
        store_transposed(vt_ref, y)

    @pl.when(jnp.logical_or(j == 0, j == 4))
    def _():
        pa_ref[...] = proj()

    @pl.when(jnp.logical_or(j == 5, j == 6))
    def _():
        pb_ref[...] = proj().astype(BF16)

    @pl.when(j >= N_SEG_A)
    def _():
        pg_ref[...] = proj().astype(BF16)


def _in_proj(x, g, w, gm, qn, kn, cos, sa, sb, kv_rows, *, tm, seq_len, tq, layer, depth):
    n = x.shape[0]
    tab_blocks = cos.shape[0] // tm
    nseg = N_SEG_A + N_SEG_G
    tab_spec = pl.BlockSpec((tm, LANES), lambda i, j: (i % tab_blocks, 0))
    vec_spec = pl.BlockSpec((1, SEG), lambda i, j: (0, 0))
    kv_spec = pl.BlockSpec((tm * ATT_HEADS, ATT_DV), lambda i, j: (layer * (n // tm) + i, 0))
    kv_shape = jax.ShapeDtypeStruct((depth * n * ATT_HEADS, ATT_DV), F32)
    out_specs = [
        pl.BlockSpec((tm, SEG), lambda i, j: (i, jnp.clip(j - 3, 0, 1))),
        pl.BlockSpec((tm, SEG), lambda i, j: (i, jnp.clip(j - 5, 0, 1))),
        kv_spec, kv_spec,
        pl.BlockSpec((tm, SEG), lambda i, j: (i, jnp.maximum(j - N_SEG_A, 0))),
        pl.BlockSpec((tm, SEG), lambda i, j: (i, jnp.clip(j - 1, 0, 2))),
    ]
    out_shape = [
        jax.ShapeDtypeStruct((n, 2 * SEG), F32),
        jax.ShapeDtypeStruct((n, 2 * SEG), BF16),
        kv_shape, kv_shape,
        jax.ShapeDtypeStruct((n, N_SEG_G * SEG), BF16),
        jax.ShapeDtypeStruct((n, 3 * SEG), BF16),
    ]
    chained = kv_rows is not None
    n_in = 9
    any_spec = pl.BlockSpec(memory_space=pl.ANY)
    extra_in = list(kv_rows) if chained else []
    aliases = {n_in: 2, n_in + 1: 3} if chained else {}
    if tq is not None:
        per_seq = seq_len // tm
        t_spec = pl.BlockSpec((None, ATT_HEADS, tm // tq, ATT_DV, tq),
                              lambda i, j: (i // per_seq, 0, i % per_seq, 0, 0))
        t_shape = jax.ShapeDtypeStruct((n // seq_len, ATT_HEADS, seq_len // tq, ATT_DV, tq), BF16)
        out_specs += [t_spec, t_spec]
        out_shape += [t_shape, t_shape]
    return pl.pallas_call(
        functools.partial(_inproj_kernel, tq=tq, chained=chained),
        grid=(n // tm, nseg),
        in_specs=[
            pl.BlockSpec((tm, D_MODEL), lambda i, j: (i, 0)),
            pl.BlockSpec((1, D_MODEL), lambda i, j: (0, 0)),
            pl.BlockSpec((D_MODEL, SEG), lambda i, j: (0, j)),
            pl.BlockSpec((SEG, SEG), lambda i, j: (0, 0)),
            vec_spec, vec_spec, tab_spec, tab_spec, tab_spec,
        ] + [any_spec] * len(extra_in),
        out_specs=out_specs,
        out_shape=out_shape,
        scratch_shapes=[pltpu.VMEM((tm, D_MODEL), BF16)],
        input_output_aliases=aliases,
        compiler_params=_cparams(("parallel", "arbitrary")),
        name="in_proj",
    )(x, g.reshape(1, D_MODEL), w, gm, qn, kn, cos, sa, sb, *extra_in)


def _rope_tables(p0, t):
    half = ROT_DIMS // 2
    inv = ROPE_THETA ** (-jnp.arange(half, dtype=F32) / half)
    pos = p0 + jnp.arange(t, dtype=jnp.int32)
    ang = pos.astype(F32)[:, None] * inv[None, :]
    cos, sin = jnp.cos(ang), jnp.sin(ang)
    ones = jnp.ones((t, ATT_DK - ROT_DIMS), F32)
    zeros = jnp.zeros((t, ATT_DK - ROT_DIMS), F32)
    zh = jnp.zeros((t, half), F32)
    c64 = jnp.concatenate([cos, cos, ones], axis=1)
    sa64 = jnp.concatenate([zh, sin, zeros], axis=1)
    sb64 = jnp.concatenate([-sin, zh, zeros], axis=1)
    rep = LANES // ATT_DK
    return tuple(jnp.concatenate([a] * rep, axis=1) for a in (c64, sa64, sb64))


def _stack_maps(q):
    lane = lax.broadcasted_iota(jnp.int32, q.shape, 1)
    zero = jnp.zeros_like(q)
    return jnp.concatenate([jnp.where(lane < ATT_DK, q, zero), jnp.where(lane >= ATT_DK, q, zero)], axis=0)


def _attn_finish(acc, l, lam, gsub, post_scale, tq):
    o = acc[:tq] / l[:tq] - lam * (acc[tq:] / l[tq:])
    return _rms(o, gsub) * post_scale


def _attn_prompt_kernel(lam_ref, gsub_ref, q_ref, k_ref, v_ref, o_ref, acc_sc, m_sc, l_sc, *, tq, post_scale):
    i = pl.program_id(2)
    qbd = _stack_maps(q_ref[...])
    m_sc[...] = jnp.full_like(m_sc, MASK_VALUE)
    l_sc[...] = jnp.zeros_like(l_sc)
    acc_sc[...] = jnp.zeros_like(acc_sc)

    def step(j, masked):
        off = pl.multiple_of(j * tq, tq)
        kj = k_ref[pl.ds(off, tq), :]
        vj = v_ref[pl.ds(off, tq), :]
        s = _dot_nt(qbd, kj)
        if masked:
            qi = lax.broadcasted_iota(jnp.int32, s.shape, 0) & (tq - 1)
            ki = lax.broadcasted_iota(jnp.int32, s.shape, 1)
            s = jnp.where(_shr(ki, CHUNK) <= _shr(qi, CHUNK), s, MASK_VALUE)
        m_prev = m_sc[...]
        m_new = jnp.maximum(m_prev, jnp.max(s, axis=-1, keepdims=True))
        alpha = jnp.exp(m_prev - m_new)
        p = jnp.exp(s - m_new)
        l_sc[...] = alpha * l_sc[...] + jnp.sum(p, axis=-1, keepdims=True)
        acc_sc[...] = alpha * acc_sc[...] + _dot(p.astype(BF16), vj)
        m_sc[...] = m_new

    def body(j, carry):
        step(j, False)
        return carry

    lax.fori_loop(0, i, body, 0)
    step(i, True)
    o_ref[...] = _attn_finish(acc_sc[...], l_sc[...], lam_ref[...], gsub_ref[...], post_scale, tq)


def _attn_prompt(qkv, lam, gsub, post_scale, *, tq):
    b, t, _ = qkv.shape
    kern = functools.partial(_attn_prompt_kernel, tq=tq, post_scale=post_scale)
    vec_spec = pl.BlockSpec((1, ATT_DV), lambda b_, h, i: (0, 0))
    return pl.pallas_call(
        kern,
        grid=(b, ATT_HEADS, t // tq),
        in_specs=[
            vec_spec, vec_spec,
            pl.BlockSpec((None, tq, ATT_DV), lambda b_, h, i: (b_, i, h)),
            pl.BlockSpec((None, t, ATT_DV), lambda b_, h, i: (b_, 0, ATT_HEADS + h)),
            pl.BlockSpec((None, t, ATT_DV), lambda b_, h, i: (b_, 0, 2 * ATT_HEADS + h)),
        ],
        out_specs=pl.BlockSpec((None, tq, ATT_DV), lambda b_, h, i: (b_, i, h)),
        out_shape=jax.ShapeDtypeStruct((b, t, ATT_HEADS * ATT_DV), F32),
        scratch_shapes=[pltpu.VMEM((2 * tq, ATT_DV), F32), pltpu.VMEM((2 * tq, 1), F32),
                        pltpu.VMEM((2 * tq, 1), F32)],
        compiler_params=_cparams(("parallel", "parallel", "arbitrary")),
        name="diff_attn_prompt",
    )(lam, gsub, qkv, qkv, qkv)


def _attn_bounded_kernel(lam_ref, gsub_ref, qt_ref, k_ref, vt_ref, o_ref, acc_sc, l_sc, *, tq, post_scale):
    i = pl.program_id(2)
    qt = qt_ref[...].astype(F32)
    row = lax.broadcasted_iota(jnp.int32, qt.shape, 0)
    qbd = jnp.concatenate([jnp.where(row < ATT_DK, qt, 0.0), jnp.where(row >= ATT_DK, qt, 0.0)],
                          axis=1).astype(BF16)
    acc_sc[...] = jnp.zeros_like(acc_sc)
    l_sc[...] = jnp.zeros_like(l_sc)

    def probs(j, masked):
        off = pl.multiple_of(j * tq, tq)
        s = _dot(k_ref[pl.ds(off, tq), :], qbd)
        if masked:
            ki = lax.broadcasted_iota(jnp.int32, s.shape, 0)
            qi = lax.broadcasted_iota(jnp.int32, s.shape, 1) & (tq - 1)
            s = jnp.where(_shr(ki, CHUNK) <= _shr(qi, CHUNK), s, MASK_VALUE)
        p = jnp.exp(s)
        return jnp.sum(p.reshape(tq // 8, 8, 2 * tq), axis=0), _dot(vt_ref[j], p.astype(BF16))

    def accumulate(first, count, masked=False):
        parts = [probs(first + c, masked) for c in range(count)]
        l_sc[...] += functools.reduce(lambda x, y: x + y, [p_[0] for p_ in parts])
        acc_sc[...] += functools.reduce(lambda x, y: x + y, [p_[1] for p_ in parts])

    def quad(jj, carry):
        accumulate(KEY_TILES_PER_ITER * jj, KEY_TILES_PER_ITER)
        return carry

    lax.fori_loop(0, i // KEY_TILES_PER_ITER, quad, 0)
    rem = i % KEY_TILES_PER_ITER

    @pl.when(rem >= 2)
    def _():
        accumulate(i - rem, 2)

    @pl.when(rem % 2 == 1)
    def _():
        accumulate(i - 1, 1)

    accumulate(i, 1, masked=True)
    l = jnp.sum(l_sc[...], axis=0, keepdims=True)
    acc = acc_sc[...]
    ot = acc[:, :tq] / l[:, :tq] - lam_ref[...] * (acc[:, tq:] / l[:, tq:])
    ms = jnp.mean(ot * ot, axis=0, keepdims=True)
    yt = ot * lax.rsqrt(ms + EPS) * (gsub_ref[...] * post_scale)
    o_ref[...] = yt.T


def _attn_bounded(qkv, q5, v5, lam, gsub, post_scale):
    b, t, _ = qkv.shape
    nt, tq = q5.shape[2], q5.shape[4]
    kern = functools.partial(_attn_bounded_kernel, tq=tq, post_scale=post_scale)
    return pl.pallas_call(
        kern,
        grid=(b, ATT_HEADS, nt),
        in_specs=[
            pl.BlockSpec((1, 1), lambda b_, h, i: (0, 0)),
            pl.BlockSpec((ATT_DV, 1), lambda b_, h, i: (0, 0)),
            pl.BlockSpec((None, None, None, ATT_DV, tq), lambda b_, h, i: (b_, h, i, 0, 0)),
            pl.BlockSpec((None, t, ATT_DV), lambda b_, h, i: (b_, 0, ATT_HEADS + h)),
            pl.BlockSpec((None, None, nt, ATT_DV, tq), lambda b_, h, i: (b_, h, 0, 0, 0)),
        ],
        out_specs=pl.BlockSpec((None, tq, ATT_DV), lambda b_, h, i: (b_, i, h)),
        out_shape=jax.ShapeDtypeStruct((b, t, ATT_HEADS * ATT_DV), F32),
        scratch_shapes=[pltpu.VMEM((ATT_DV, 2 * tq), F32), pltpu.VMEM((8, 2 * tq), F32)],
        compiler_params=_cparams(("parallel", "parallel", "arbitrary")),
        name="diff_attn_bounded",
    )(lam[:, :1], gsub.reshape(ATT_DV, 1), q5, qkv, v5)


def _attn_sample_kernel(lam_ref, gsub_ref, q_ref, kn_ref, vn_ref, kp_ref, vp_ref, o_ref, *, tq, post_scale):
    qbd = _stack_maps(q_ref[...])
    sp = _dot_nt(qbd, kp_ref[...])
    sn = _dot_nt(qbd, kn_ref[...])
    m = jnp.maximum(jnp.max(sp, axis=-1, keepdims=True), jnp.max(sn, axis=-1, keepdims=True))
    pp = jnp.exp(sp - m)
    pn = jnp.exp(sn - m)
    l = jnp.sum(pp, axis=-1, keepdims=True) + jnp.sum(pn, axis=-1, keepdims=True)
    acc = _dot(pp.astype(BF16), vp_ref[...]) + _dot(pn.astype(BF16), vn_ref[...])
    o_ref[...] = _attn_finish(acc, l, lam_ref[...], gsub_ref[...], post_scale, tq)


def _attn_sample(qkv, kp, vp, lam, gsub, post_scale):
    b, t, _ = qkv.shape
    tp = kp.shape[1]
    kern = functools.partial(_attn_sample_kernel, tq=t, post_scale=post_scale)
    vec_spec = pl.BlockSpec((1, ATT_DV), lambda b_, h: (0, 0))
    return pl.pallas_call(
        kern,
        grid=(b, ATT_HEADS),
        in_specs=[
            vec_spec, vec_spec,
            pl.BlockSpec((None, t, ATT_DV), lambda b_, h: (b_, 0, h)),
            pl.BlockSpec((None, t, ATT_DV), lambda b_, h: (b_, 0, ATT_HEADS + h)),
            pl.BlockSpec((None, t, ATT_DV), lambda b_, h: (b_, 0, 2 * ATT_HEADS + h)),
            pl.BlockSpec((None, tp, ATT_DV), lambda b_, h: (b_, 0, h)),
            pl.BlockSpec((None, tp, ATT_DV), lambda b_, h: (b_, 0, h)),
        ],
        out_specs=pl.BlockSpec((None, t, ATT_DV), lambda b_, h: (b_, 0, h)),
        out_shape=jax.ShapeDtypeStruct((b, t, ATT_HEADS * ATT_DV), F32),
        compiler_params=_cparams(("parallel", "parallel")),
        name="diff_attn_sample",
    )(lam, gsub, qkv, qkv, qkv, kp, vp)


HG_W = HG_HEADS * HG_DK
HG_V = HG_HEADS * HG_DV
B_PAD = 8


def _level_reference(b_sc, w):
    if w >= 4:
        pieces = []
        for p in range(CHUNK // (2 * w)):
            row = b_sc[pl.ds(B_PAD + p * 2 * w + w, 1), :]
            pieces.append(jnp.broadcast_to(row, (2 * w, HG_W)))
        return jnp.concatenate(pieces, axis=0)
    t = lax.broadcasted_iota(jnp.int32, (CHUNK, HG_W), 0)
    phase = t & (2 * w - 1)
    r = None
    for ph in range(2 * w):
        shifted = b_sc[pl.ds(B_PAD + w - ph, CHUNK), :]
        r = shifted if r is None else jnp.where(phase == ph, shifted, r)
    return r


def _hgrn_chunk(hq, hf, hi, lb, st_sc, b_sc, masks):
    tril, bd_k, bd_v, bd_s = masks
    f = lb + (1.0 - lb) * jax.nn.sigmoid(hf)
    g = jnp.log(f)
    kk = 1.0 - f
    q = _silu(hq) * (HG_DK ** -0.5)
    g0 = g.astype(BF16)
    r1 = g - g0.astype(F32)
    g1 = r1.astype(BF16)
    g2 = (r1 - g1.astype(F32)).astype(BF16)
    b = _dot(tril, g0) + _dot(tril, g1) + _dot(tril, g2)
    b_sc[pl.ds(B_PAD, CHUNK), :] = b

    t_idx = lax.broadcasted_iota(jnp.int32, (CHUNK, HG_W), 0)
    s_idx = lax.broadcasted_iota(jnp.int32, (CHUNK, HG_W), 1) & (CHUNK - 1)
    zero = jnp.zeros((CHUNK, HG_W), F32)

    def block_diag_k(x):
        return jnp.where(bd_k, jnp.concatenate([x] * HG_HEADS, axis=0), jnp.zeros((), F32)).astype(BF16)

    a = jnp.where(t_idx == s_idx, _dot_nt(q.astype(BF16), block_diag_k(kk)), zero)
    w = CHUNK // 2
    while w >= 1:
        r = _level_reference(b_sc, w)
        upper = (t_idx & w) != 0
        e = jnp.exp(jnp.where(upper, b - r, r - b))
        ql = jnp.where(upper, q * e, zero)
        kl = jnp.where(upper, zero, kk * e)
        same_pair = _shr(t_idx, 2 * w) == _shr(s_idx, 2 * w)
        a = a + jnp.where(same_pair, _dot_nt(ql.astype(BF16), block_diag_k(kl)), zero)
        w //= 2

    b_last = jnp.broadcast_to(b_sc[pl.ds(B_PAD + CHUNK - 1, 1), :], (CHUNK, HG_W))
    qb = q * jnp.exp(b)
    kd = kk * jnp.exp(b_last - b)
    v16 = hi
    vbd = jnp.where(bd_v, jnp.concatenate([hi.astype(F32)] * HG_HEADS, axis=0), jnp.zeros((), F32)).astype(BF16)
    st = st_sc[...]
    o = _dot(a.astype(BF16), vbd) + _dot_nt(qb.astype(BF16), st.astype(BF16))
    upd = _dot_tn(v16, kd.astype(BF16))
    decay = jnp.exp(b_sc[pl.ds(B_PAD + CHUNK - 1, 1), :])
    st_sc[...] = st * decay + jnp.where(bd_s, upd, jnp.zeros((), F32))
    return o


def _mix_kernel(x_ref, u_ref, hqf_ref, hi_ref, hg_ref, g_ref, ya_ref, st0_ref, pool0_ref,
                lb_ref, pw_ref, ps_ref, hgn_ref, wup_ref, wua_ref, wuh_ref, wo_ref,
                xo_ref, stn_ref, pooln_ref,
                st_sc, b_sc, ext_sc, yh_sc, *, tm, p0):
    it = pl.program_id(1)

    @pl.when(it == 0)
    def _():
        st_sc[...] = st0_ref[...]
        ext_sc[pl.ds(0, POOL_HIST), :] = pool0_ref[...]
        b_sc[...] = jnp.zeros_like(b_sc)

    u = u_ref[...]
    ext_sc[pl.ds(POOL_HIST, tm), :] = u
    row = lax.broadcasted_iota(jnp.int32, (tm, LANES), 0)
    seen = (p0 + 1 + it * tm + row).astype(F32)
    cols = []
    for gi, w in enumerate(POOL_WINDOWS):
        sl = pl.ds(gi * LANES, LANES)
        win = u[:, gi * LANES:(gi + 1) * LANES]
        for d in range(1, w):
            win = win + ext_sc[pl.ds(POOL_HIST - d, tm), sl]
        dlt = win / jnp.minimum(seen, float(w)) - u[:, gi * LANES:(gi + 1) * LANES]
        cols.append(_dot(dlt.astype(BF16), pw_ref[gi]))
    y_pool = jnp.concatenate(cols, axis=1) * ps_ref[...]
    ext_sc[pl.ds(0, POOL_HIST), :] = ext_sc[pl.ds(tm, POOL_HIST), :]

    tri_r = lax.broadcasted_iota(jnp.int32, (CHUNK, CHUNK), 0)
    tri_c = lax.broadcasted_iota(jnp.int32, (CHUNK, CHUNK), 1)
    tril = (tri_c <= tri_r).astype(BF16)
    rk = lax.broadcasted_iota(jnp.int32, (HG_W, HG_W), 0)
    ck = lax.broadcasted_iota(jnp.int32, (HG_W, HG_W), 1)
    bd_k = _shr(rk, CHUNK) == _shr(ck, HG_DK)
    rv = lax.broadcasted_iota(jnp.int32, (HG_W, HG_V), 0)
    cv = lax.broadcasted_iota(jnp.int32, (HG_W, HG_V), 1)
    bd_v = _shr(rv, CHUNK) == _shr(cv, HG_DV)
    rs = lax.broadcasted_iota(jnp.int32, (HG_V, HG_W), 0)
    cs = lax.broadcasted_iota(jnp.int32, (HG_V, HG_W), 1)
    bd_s = _shr(rs, HG_DV) == _shr(cs, HG_DK)
    masks = (tril, bd_k, bd_v, bd_s)
    lb = lb_ref[...]
    for c in range(tm // CHUNK):
        rows = pl.ds(c * CHUNK, CHUNK)
        o = _hgrn_chunk(hqf_ref[rows, pl.ds(0, HG_W)], hqf_ref[rows, pl.ds(HG_W, HG_W)],
                        hi_ref[rows, :], lb, st_sc, b_sc, masks)
        yh_sc[rows, :] = o
    oh = yh_sc[...]
    hg = hg_ref[...].astype(F32)
    heads = []
    for h in range(HG_HEADS):
        sl = slice(h * HG_DV, (h + 1) * HG_DV)
        heads.append(_rms(oh[:, sl], hgn_ref[...]) * _silu(hg[:, sl]))
    y_hg = jnp.concatenate(heads, axis=1)

    gts = g_ref[...].astype(F32)
    merged = (_sigmoid(gts[:, :D_MODEL]) * _dot(y_pool.astype(BF16), wup_ref[...])
              + _sigmoid(gts[:, D_MODEL:2 * D_MODEL]) * _dot(ya_ref[...].astype(BF16), wua_ref[...])
              + _sigmoid(gts[:, 2 * D_MODEL:]) * _dot(y_hg.astype(BF16), wuh_ref[...]))
    xo_ref[...] = x_ref[...] + _dot(merged.astype(BF16), wo_ref[...])

    @pl.when(it == pl.num_programs(1) - 1)
    def _():
        stn_ref[...] = st_sc[...]
        pooln_ref[...] = ext_sc[pl.ds(0, POOL_HIST), :]


def _mix_out(x, proj_a, proj_b, gates, y_att, st0, pool0, lb, pool_w, pool_scale, hg_outn,
             w_up_pool, w_up_att, w_up_hgrn, w_out, *, tm, p0):
    b, t, _ = x.shape
    kern = functools.partial(_mix_kernel, tm=tm, p0=p0)

    def rows(width, col):
        return pl.BlockSpec((None, tm, width), lambda b_, i: (b_, i, col))

    def const(shape):
        return pl.BlockSpec(shape, lambda b_, i: (0,) * len(shape))

    def per_batch(shape):
        return pl.BlockSpec((None,) + shape, lambda b_, i: (b_, 0, 0))

    return pl.pallas_call(
        kern,
        grid=(b, t // tm),
        in_specs=[
            rows(D_MODEL, 0),
            rows(SEG, 0),
            rows(SEG, 1),
            rows(SEG, 0),
            rows(SEG, 1),
            rows(3 * D_MODEL, 0),
            rows(SEG, 0),
            per_batch((HG_V, HG_W)),
            per_batch((POOL_HIST, SEG)),
            const((1, HG_W)), const((4, LANES, LANES)), const((1, SEG)), const((1, HG_DV)),
            const((SEG, D_MODEL)), const((SEG, D_MODEL)), const((SEG, D_MODEL)), const((D_MODEL, D_MODEL)),
        ],
        out_specs=[
            rows(D_MODEL, 0),
            per_batch((HG_V, HG_W)),
            per_batch((POOL_HIST, SEG)),
        ],
        out_shape=[
            jax.ShapeDtypeStruct((b, t, D_MODEL), F32),
            jax.ShapeDtypeStruct((b, HG_V, HG_W), F32),
            jax.ShapeDtypeStruct((b, POOL_HIST, SEG), F32),
        ],
        scratch_shapes=[
            pltpu.VMEM((HG_V, HG_W), F32),
            pltpu.VMEM((CHUNK + 2 * B_PAD, HG_W), F32),
            pltpu.VMEM((POOL_HIST + tm, SEG), F32),
            pltpu.VMEM((tm, HG_V), F32),
        ],
        compiler_params=_cparams(("parallel", "arbitrary")),
        name="mix_out",
    )(x, proj_a, proj_a, proj_b, proj_b, gates, y_att, st0, pool0,
      lb, pool_w, pool_scale, hg_outn, w_up_pool, w_up_att, w_up_hgrn, w_out)


def _state_to_block_diag(s):
    b = s.shape[0]
    st = jnp.swapaxes(s, 2, 3)
    eye = jnp.eye(HG_HEADS, dtype=s.dtype)
    return jnp.einsum('bhed,hg->bhegd', st, eye).reshape(b, HG_V, HG_W)


def _block_diag_to_state(st):
    b = st.shape[0]
    s5 = st.reshape(b, HG_HEADS, HG_DV, HG_HEADS, HG_DK)
    diag = jnp.stack([s5[:, h, :, h, :] for h in range(HG_HEADS)], axis=1)
    return jnp.swapaxes(diag, 2, 3)


def _head_rms(x, gain):
    return jnp.concatenate(
        [_rms(x[:, h * X_HD:(h + 1) * X_HD], gain) for h in range(X_HEADS)], axis=1)


def _memkv_kernel(m_ref, g_ref, w_ref, kn_ref, k_ref, v_ref, k16_ref, v16_ref):
    h = _rms(m_ref[...], g_ref[...]).astype(BF16)
    kv = _dot(h, w_ref[...])
    mk = _head_rms(kv[:, :D_MODEL], kn_ref[...])
    mv = kv[:, D_MODEL:]
    k_ref[...] = mk
    v_ref[...] = mv
    k16_ref[...] = mk.astype(BF16)
    v16_ref[...] = mv.astype(BF16)


def _memory_kv(mem, g, w_ckv, kn):
    b, n, _ = mem.shape
    blk = pl.BlockSpec((None, n, D_MODEL), lambda b_: (b_, 0, 0))
    return pl.pallas_call(
        _memkv_kernel,
        grid=(b,),
        in_specs=[
            blk,
            pl.BlockSpec((1, D_MODEL), lambda b_: (0, 0)),
            pl.BlockSpec((D_MODEL, 2 * D_MODEL), lambda b_: (0, 0)),
            pl.BlockSpec((1, X_HD), lambda b_: (0, 0)),
        ],
        out_specs=[blk, blk, blk, blk],
        out_shape=[jax.ShapeDtypeStruct((b, n, D_MODEL), F32)] * 2
        + [jax.ShapeDtypeStruct((b, n, D_MODEL), BF16)] * 2,
        compiler_params=_cparams(("parallel",)),
        name="memory_kv",
    )(mem, g.reshape(1, D_MODEL), w_ckv, kn.reshape(1, X_HD))


def _cross_kernel(x_ref, g_ref, wq_ref, qn_ref, mk_ref, mv_ref, wo_ref, o_ref):
    x = x_ref[...]
    q = _head_rms(_dot(_rms(x, g_ref[...]).astype(BF16), wq_ref[...]), qn_ref[...])
    q = (q * (X_HD ** -0.5)).astype(BF16)
    outs = []
    for h in range(X_HEADS):
        sl = slice(h * X_HD, (h + 1) * X_HD)
        s = _dot_nt(q[:, sl], mk_ref[:, sl])
        p = jnp.exp(s - jnp.max(s, axis=-1, keepdims=True))
        p = p / jnp.sum(p, axis=-1, keepdims=True)
        outs.append(_dot(p.astype(BF16), mv_ref[:, sl]))
    o = jnp.concatenate(outs, axis=1)
    o_ref[...] = x + _dot(o.astype(BF16), wo_ref[...])


def _cross_attend(x, mk, mv, g, w_cq, qn, w_co, *, tm):
    b, t, _ = x.shape
    n = mk.shape[1]
    rows = pl.BlockSpec((None, tm, D_MODEL), lambda b_, i: (b_, i, 0))
    mem = pl.BlockSpec((None, n, D_MODEL), lambda b_, i: (b_, 0, 0))
    wsq = pl.BlockSpec((D_MODEL, D_MODEL), lambda b_, i: (0, 0))
    return pl.pallas_call(
        _cross_kernel,
        grid=(b, t // tm),
        in_specs=[
            rows,
            pl.BlockSpec((1, D_MODEL), lambda b_, i: (0, 0)),
            wsq,
            pl.BlockSpec((1, X_HD), lambda b_, i: (0, 0)),
            mem, mem, wsq,
        ],
        out_specs=rows,
        out_shape=jax.ShapeDtypeStruct((b, t, D_MODEL), F32),
        compiler_params=_cparams(("parallel", "parallel")),
        name="cross_attn",
    )(x, g.reshape(1, D_MODEL), w_cq, qn.reshape(1, X_HD), mk, mv, w_co)


def _layer(x, l, depth, p0, rope, kv_rows, past_kv, pool_prev, hg_state, lb, mk16, mv16, w, *, tiles):
    b, t, _ = x.shape
    n = b * t
    x2 = _ffn_half(x.reshape(n, D_MODEL), w['norm_ffn1'], w['w_ffn1_in'], w['w_ffn1_out'],
                   tm=tiles['ffn'], tf=tiles['tf'])

    proj_a, proj_b, kf, vf, gates, qkv, *tiles_t = _in_proj(
        x2, w['norm_mix'], w['w_in'], w['group_mean'], w['att_qn'], w['att_kn'], *rope, kv_rows,
        tm=tiles['proj'], seq_len=t, tq=None if past_kv is not None else tiles['attn_bounded'],
        layer=l, depth=depth)

    lam_init = 0.8 - 0.6 * math.exp(-0.3 * l)
    lam = (jnp.exp(jnp.sum(w['lq1'] * w['lk1'])) - jnp.exp(jnp.sum(w['lq2'] * w['lk2'])) + lam_init)
    lam = jnp.full((1, ATT_DV), lam, F32)
    gsub = w['att_subln'].reshape(1, ATT_DV)
    qkv3 = qkv.reshape(b, t, 3 * SEG)
    if past_kv is None:
        bound = (ATT_DK ** 0.5) * jnp.max(jnp.abs(w['att_qn'])) * jnp.max(jnp.abs(w['att_kn']))
        y_att = lax.cond(
            bound <= SCORE_BOUND_MAX,
            lambda a, q5, v5: _attn_bounded(a, q5, v5, lam, gsub, 1.0 - lam_init),
            lambda a, q5, v5: _attn_prompt(a, lam, gsub, 1.0 - lam_init, tq=tiles['attn']),
            qkv3, *tiles_t)
    else:
        y_att = _attn_sample(qkv3, past_kv[0], past_kv[1], lam, gsub, 1.0 - lam_init)

    x3, st_new, pool_new = _mix_out(
        x2.reshape(b, t, D_MODEL), proj_a.reshape(b, t, 2 * SEG), proj_b.reshape(b, t, 2 * SEG),
        gates.reshape(b, t, N_SEG_G * SEG),
        y_att, _state_to_block_diag(hg_state), pool_prev, lb,
        w['pool_w'], w['pool_scale'], w['hg_outn'], w['w_up_pool'], w['w_up_att'], w['w_up_hgrn'], w['w_out'],
        tm=tiles['mix'], p0=p0)

    x4 = _cross_attend(x3, mk16, mv16, w['norm_cross'], w['w_cq'], w['cross_qn'], w['w_co'], tm=tiles['cross'])
    x5 = _ffn_half(x4.reshape(n, D_MODEL), w['norm_ffn2'], w['w_ffn2_in'], w['w_ffn2_out'],
                   tm=tiles['ffn'], tf=tiles['tf'])
    return (x5.reshape(b, t, D_MODEL), (kf, vf), pool_new[:, POOL_HIST - POOL_STATE:],
            _block_diag_to_state(st_new))


PROMPT_TILES = dict(ffn=1024, tf=256, proj=1024, attn=256, attn_bounded=512, mix=256, cross=512)
SAMPLE_TILES = dict(ffn=512, tf=256, proj=512, attn=64, mix=64, cross=64)


def kernel(x_prompt, x_sample, cache_attn_k, cache_attn_v, cache_mem_k, cache_mem_v, state_pool, state_hgrn, mem_prompt, norm_ffn1, w_ffn1_in, w_ffn1_out, norm_mix, w_in, pool_w, pool_scale, att_q_norm, att_k_norm, lambda_q1, lambda_k1, lambda_q2, lambda_k2, att_subln, hgrn_lower, hgrn_out_norm, w_up_pool, w_up_att, w_up_hgrn, w_out, norm_cross, norm_mem, w_cq, w_ckv, cross_q_norm, cross_k_norm, w_co, norm_ffn2, w_ffn2_in, w_ffn2_out):
    depth = w_in.shape[0]
    bp = x_prompt.shape[0]
    bs = x_sample.shape[0]
    p0_sample = cache_attn_k.shape[2]

    lp = jax.nn.softmax(hgrn_lower.astype(F32), axis=0)
    lbs = jnp.cumsum(lp, axis=0) - lp[0:1]

    gidx = jnp.arange(SEG) // ATT_DK
    group_mean = ((gidx[:, None] == gidx[None, :]).astype(F32) / ATT_DK).astype(BF16)

    def layer_weights(l):
        return dict(
            norm_ffn1=norm_ffn1[l], w_ffn1_in=w_ffn1_in[l].astype(BF16), w_ffn1_out=w_ffn1_out[l].astype(BF16),
            norm_mix=norm_mix[l], w_in=w_in[l].astype(BF16), group_mean=group_mean,
            att_qn=jnp.tile(att_q_norm[l], SEG // ATT_DK).reshape(1, SEG),
            att_kn=jnp.tile(att_k_norm[l], SEG // ATT_DK).reshape(1, SEG),
            lq1=lambda_q1[l].astype(F32), lk1=lambda_k1[l].astype(F32),
            lq2=lambda_q2[l].astype(F32), lk2=lambda_k2[l].astype(F32),
            att_subln=att_subln[l],
            pool_w=pool_w[l].astype(BF16), pool_scale=pool_scale[l].reshape(1, SEG),
            hg_outn=hgrn_out_norm[l].reshape(1, HG_DV),
            w_up_pool=w_up_pool[l].astype(BF16), w_up_att=w_up_att[l].astype(BF16),
            w_up_hgrn=w_up_hgrn[l].astype(BF16), w_out=w_out[l].astype(BF16),
            norm_cross=norm_cross[l], w_cq=w_cq[l].astype(BF16), cross_qn=cross_q_norm[l],
            w_co=w_co[l].astype(BF16),
            norm_ffn2=norm_ffn2[l], w_ffn2_in=w_ffn2_in[l].astype(BF16), w_ffn2_out=w_ffn2_out[l].astype(BF16),
        )

    weights = [layer_weights(l) for l in range(depth)]

    def rope_for(p0, t, tm):
        tabs = _rope_tables(p0, t)
        if t < tm:
            tabs = tuple(jnp.concatenate([a] * (tm // t), axis=0) for a in tabs)
        return tabs

    rope_prompt = rope_for(0, x_prompt.shape[1], PROMPT_TILES['proj'])
    rope_sample = rope_for(p0_sample, x_sample.shape[1], SAMPLE_TILES['proj'])

    y = x_prompt
    pkv = None
    pmk, pmv, ppool, phg = [], [], [], []
    for l in range(depth):
        mk, mv, mk16, mv16 = _memory_kv(mem_prompt, norm_mem[l], w_ckv[l].astype(BF16), cross_k_norm[l])
        pool0 = jnp.zeros((bp, POOL_HIST, SEG), F32)
        hg0 = jnp.zeros((bp, HG_HEADS, HG_DK, HG_DV), F32)
        y, pkv, pn, hn = _layer(y, l, depth, 0, rope_prompt, pkv, None, pool0, hg0, lbs[l].reshape(1, HG_W),
                                mk16, mv16, weights[l], tiles=PROMPT_TILES)
        pmk.append(mk.reshape(bp, -1, X_HEADS, X_HD)); pmv.append(mv.reshape(bp, -1, X_HEADS, X_HD))
        ppool.append(pn); phg.append(hn)
    y_prompt = y
    kv_shape = (depth, bp, x_prompt.shape[1], ATT_HEADS, ATT_DV)
    pk, pv = pkv[0].reshape(kv_shape), pkv[1].reshape(kv_shape)

    y = x_sample
    skv = None
    spool, shg = [], []
    for l in range(depth):
        past = (cache_attn_k[l].reshape(bs, p0_sample, ATT_HEADS * 2 * ATT_DK).astype(BF16),
                cache_attn_v[l].reshape(bs, p0_sample, ATT_HEADS * ATT_DV).astype(BF16))
        pool0 = jnp.pad(state_pool[l], ((0, 0), (POOL_HIST - POOL_STATE, 0), (0, 0)))
        mk16 = cache_mem_k[l].reshape(bs, -1, D_MODEL).astype(BF16)
        mv16 = cache_mem_v[l].reshape(bs, -1, D_MODEL).astype(BF16)
        y, skv, pn, hn = _layer(y, l, depth, p0_sample, rope_sample, skv, past, pool0, state_hgrn[l],
                                lbs[l].reshape(1, HG_W), mk16, mv16, weights[l], tiles=SAMPLE_TILES)
        spool.append(pn); shg.append(hn)
    y_sample = y
    kv_shape = (depth, bs, x_sample.shape[1], ATT_HEADS, ATT_DV)
    sk, sv = skv[0].reshape(kv_shape), skv[1].reshape(kv_shape)

    return (y_prompt, y_sample,
            pk, pv, jnp.stack(pmk), jnp.stack(pmv),
            jnp.stack(ppool), jnp.stack(phg),
            sk, sv, jnp.stack(spool), jnp.stack(shg))
```

```python
import functools
import math

import jax
import jax.numpy as jnp
from jax import lax
from jax.experimental import pallas as pl
from jax.experimental.pallas import tpu as pltpu

F32 = jnp.float32
BF16 = jnp.bfloat16

D_MODEL = 1024
CHUNK = 64
EPS = 1e-6
MASK_VALUE = -1e30
POOL_WINDOWS = (2, 4, 8, 16)
POOL_STATE = 15
POOL_HIST = 16
ATT_HEADS = 4
ATT_DK = 64
ATT_DV = 128
ROT_DIMS = 16
ROPE_THETA = 500000.0
HG_HEADS = 4
HG_DK = 64
HG_DV = 128
X_HEADS = 4
X_HD = 256
D_FF = 2816
SEG = 512
N_SEG_A = 7
N_SEG_G = 6
LANES = 128
VMEM_LIMIT = 56 * 1024 * 1024
SCORE_BOUND_MAX = 20.0
KEY_TILES_PER_ITER = 4


def _cparams(sem):
    return pltpu.CompilerParams(dimension_semantics=sem, vmem_limit_bytes=VMEM_LIMIT)


def _rms(x, g):
    ms = jnp.mean(x * x, axis=-1, keepdims=True)
    return x * lax.rsqrt(ms + EPS) * g


def _sigmoid(x):
    return 0.5 * jnp.tanh(0.5 * x) + 0.5


def _silu(x):
    return x * _sigmoid(x)


def _dot(a, b):
    return jnp.dot(a, b, preferred_element_type=F32)


def _dot_nt(a, b):
    return lax.dot_general(a, b, (((1,), (1,)), ((), ())), preferred_element_type=F32)


def _dot_tn(a, b):
    return lax.dot_general(a, b, (((0,), (0,)), ((), ())), preferred_element_type=F32)


def _shr(x, pow2):
    return lax.shift_right_logical(x, jnp.int32(int(math.log2(pow2))))


def _ffn_kernel(x_ref, g_ref, wi_ref, wo_ref, o_ref, *, tf):
    x = x_ref[...]
    h = _rms(x, g_ref[...]).astype(BF16)
    acc = None
    for c in range(D_FF // tf):
        a = _dot(h, wi_ref[:, pl.ds(c * tf, tf)])
        b = _dot(h, wi_ref[:, pl.ds(D_FF + c * tf, tf)])
        part = _dot((_silu(a) * b).astype(BF16), wo_ref[pl.ds(c * tf, tf), :])
        acc = part if acc is None else acc + part
    o_ref[...] = x + 0.5 * acc


def _resident(shape):
    return pl.BlockSpec(shape, lambda *_: (0,) * len(shape), pipeline_mode=pl.Buffered(1))


def _ffn_half(x, g, w_i, w_o, *, tm, tf):
    n = x.shape[0]
    return pl.pallas_call(
        functools.partial(_ffn_kernel, tf=tf),
        grid=(n // tm,),
        in_specs=[
            pl.BlockSpec((tm, D_MODEL), lambda i: (i, 0)),
            _resident((1, D_MODEL)),
            _resident((D_MODEL, 2 * D_FF)),
            _resident((D_FF, D_MODEL)),
        ],
        out_specs=pl.BlockSpec((tm, D_MODEL), lambda i: (i, 0)),
        out_shape=jax.ShapeDtypeStruct((n, D_MODEL), F32),
        compiler_params=_cparams(("parallel",)),
        name="ffn_half",
    )(x, g.reshape(1, D_MODEL), w_i, w_o)


def _inproj_kernel(x_ref, g_ref, w_ref, gm_ref, qn_ref, kn_ref, cos_ref, sa_ref, sb_ref, *rest, tq, chained):
    rest = rest[2:] if chained else rest
    pa_ref, pb_ref, kf_ref, vf_ref, pg_ref, qkv_ref = rest[:6]
    qt_ref, vt_ref = rest[6:8] if tq is not None else (None, None)

    def store_head_rows(ref, y):
        for h in range(ATT_HEADS):
            ref[pl.ds(h, y.shape[0], stride=ATT_HEADS), :] = y[:, h * ATT_DV:(h + 1) * ATT_DV]

    def store_transposed(ref, y):
        if ref is None:
            return
        yt = y.T.astype(BF16)
        for h in range(ATT_HEADS):
            for c in range(y.shape[0] // tq):
                ref[h, c] = yt[h * ATT_DV:(h + 1) * ATT_DV, c * tq:(c + 1) * tq]

    h = _rms(x_ref[...], g_ref[...]).astype(BF16)

    def proj(seg):
        return _dot(h, w_ref[:, pl.ds(seg * SEG, SEG)])

    rep = SEG // LANES
    cos = jnp.concatenate([cos_ref[...]] * rep, axis=1)
    sa = jnp.concatenate([sa_ref[...]] * rep, axis=1)
    sb = jnp.concatenate([sb_ref[...]] * rep, axis=1)

    def norm_rope(y, gain):
        ms = _dot((y * y).astype(BF16), gm_ref[...])
        yn = y * lax.rsqrt(ms + EPS) * gain
        half = ROT_DIMS // 2
        return yn * cos + pltpu.roll(yn, half, 1) * sa + pltpu.roll(yn, SEG - half, 1) * sb

    pa_ref[:, pl.ds(0, SEG)] = proj(0)

    q = norm_rope(proj(1), qn_ref[...]) * (ATT_DK ** -0.5)
    qkv_ref[:, pl.ds(0, SEG)] = q.astype(BF16)
    store_transposed(qt_ref, q)

    k = norm_rope(proj(2), kn_ref[...])
    store_head_rows(kf_ref, k)
    qkv_ref[:, pl.ds(SEG, SEG)] = k.astype(BF16)

    v = proj(3)
    store_head_rows(vf_ref, v)
    qkv_ref[:, pl.ds(2 * SEG, SEG)] = v.astype(BF16)
    store_transposed(vt_ref, v)

    pa_ref[:, pl.ds(SEG, SEG)] = proj(4)
    pb_ref[:, pl.ds(0, SEG)] = proj(5).astype(BF16)
    pb_ref[:, pl.ds(SEG, SEG)] = proj(6).astype(BF16)
    for s in range(N_SEG_G):
        pg_ref[:, pl.ds(s * SEG, SEG)] = proj(N_SEG_A + s).astype(BF16)


def _in_proj(x, g, w, gm, qn, kn, cos, sa, sb, kv_rows, *, tm, seq_len, tq, layer, depth):
    n = x.shape[0]
    tab_blocks = cos.shape[0] // tm
    nseg = N_SEG_A + N_SEG_G
    tab_spec = pl.BlockSpec((tm, LANES), lambda i: (i % tab_blocks, 0))
    kv_spec = pl.BlockSpec((tm * ATT_HEADS, ATT_DV), lambda i: (layer * (n // tm) + i, 0))
    kv_shape = jax.ShapeDtypeStruct((depth * n * ATT_HEADS, ATT_DV), F32)

    def rows(width):
        return pl.BlockSpec((tm, width), lambda i: (i, 0))

    out_specs = [rows(2 * SEG), rows(2 * SEG), kv_spec, kv_spec, rows(N_SEG_G * SEG), rows(3 * SEG)]
    out_shape = [
        jax.ShapeDtypeStruct((n, 2 * SEG), F32),
        jax.ShapeDtypeStruct((n, 2 * SEG), BF16),
        kv_shape, kv_shape,
        jax.ShapeDtypeStruct((n, N_SEG_G * SEG), BF16),
        jax.ShapeDtypeStruct((n, 3 * SEG), BF16),
    ]
    chained = kv_rows is not None
    n_in = 9
    any_spec = pl.BlockSpec(memory_space=pl.ANY)
    extra_in = list(kv_rows) if chained else []
    aliases = {n_in: 2, n_in + 1: 3} if chained else {}
    if tq is not None:
        per_seq = seq_len // tm
        t_spec = pl.BlockSpec((None, ATT_HEADS, tm // tq, ATT_DV, tq),
                              lambda i: (i // per_seq, 0, i % per_seq, 0, 0))
        t_shape = jax.ShapeDtypeStruct((n // seq_len, ATT_HEADS, seq_len // tq, ATT_DV, tq), BF16)
        out_specs += [t_spec, t_spec]
        out_shape += [t_shape, t_shape]
    return pl.pallas_call(
        functools.partial(_inproj_kernel, tq=tq, chained=chained),
        grid=(n // tm,),
        in_specs=[
            rows(D_MODEL),
            _resident((1, D_MODEL)),
            _resident((D_MODEL, nseg * SEG)),
            _resident((SEG, SEG)),
            _resident((1, SEG)), _resident((1, SEG)), tab_spec, tab_spec, tab_spec,
        ] + [any_spec] * len(extra_in),
        out_specs=out_specs,
        out_shape=out_shape,
        input_output_aliases=aliases,
        compiler_params=_cparams(("parallel",)),
        name="in_proj",
    )(x, g.reshape(1, D_MODEL), w, gm, qn, kn, cos, sa, sb, *extra_in)


def _rope_tables(p0, t):
    half = ROT_DIMS // 2
    inv = ROPE_THETA ** (-jnp.arange(half, dtype=F32) / half)
    in_head = jnp.arange(LANES) % ATT_DK
    pos = p0 + jnp.arange(t, dtype=jnp.int32)
    ang = pos.astype(F32)[:, None] * inv[in_head % half][None, :]
    cos, sin = jnp.cos(ang), jnp.sin(ang)
    lower = (in_head < half)[None, :]
    upper = jnp.logical_and(in_head >= half, in_head < ROT_DIMS)[None, :]
    return (jnp.where(in_head[None, :] < ROT_DIMS, cos, 1.0),
            jnp.where(upper, sin, 0.0),
            jnp.where(lower, -sin, 0.0))


def _stack_maps(q):
    lane = lax.broadcasted_iota(jnp.int32, q.shape, 1)
    zero = jnp.zeros_like(q)
    return jnp.concatenate([jnp.where(lane < ATT_DK, q, zero), jnp.where(lane >= ATT_DK, q, zero)], axis=0)


def _attn_finish(acc, l, lam, gsub, post_scale, tq):
    o = acc[:tq] / l[:tq] - lam * (acc[tq:] / l[tq:])
    return _rms(o, gsub) * post_scale


def _attn_prompt_kernel(lam_ref, gsub_ref, q_ref, k_ref, v_ref, o_ref, acc_sc, m_sc, l_sc, *, tq, post_scale):
    i = pl.program_id(2)
    qbd = _stack_maps(q_ref[...])
    m_sc[...] = jnp.full_like(m_sc, MASK_VALUE)
    l_sc[...] = jnp.zeros_like(l_sc)
    acc_sc[...] = jnp.zeros_like(acc_sc)

    def step(j, masked):
        off = pl.multiple_of(j * tq, tq)
        kj = k_ref[pl.ds(off, tq), :]
        vj = v_ref[pl.ds(off, tq), :]
        s = _dot_nt(qbd, kj)
        if masked:
            qi = lax.broadcasted_iota(jnp.int32, s.shape, 0) & (tq - 1)
            ki = lax.broadcasted_iota(jnp.int32, s.shape, 1)
            s = jnp.where(_shr(ki, CHUNK) <= _shr(qi, CHUNK), s, MASK_VALUE)
        m_prev = m_sc[...]
        m_new = jnp.maximum(m_prev, jnp.max(s, axis=-1, keepdims=True))
        alpha = jnp.exp(m_prev - m_new)
        p = jnp.exp(s - m_new)
        l_sc[...] = alpha * l_sc[...] + jnp.sum(p, axis=-1, keepdims=True)
        acc_sc[...] = alpha * acc_sc[...] + _dot(p.astype(BF16), vj)
        m_sc[...] = m_new

    def body(j, carry):
        step(j, False)
        return carry

    lax.fori_loop(0, i, body, 0)
    step(i, True)
    o_ref[...] = _attn_finish(acc_sc[...], l_sc[...], lam_ref[...], gsub_ref[...], post_scale, tq)


def _attn_prompt(qkv, lam, gsub, post_scale, *, tq):
    b, t, _ = qkv.shape
    kern = functools.partial(_attn_prompt_kernel, tq=tq, post_scale=post_scale)
    vec_spec = pl.BlockSpec((1, ATT_DV), lambda b_, h, i: (0, 0))
    return pl.pallas_call(
        kern,
        grid=(b, ATT_HEADS, t // tq),
        in_specs=[
            vec_spec, vec_spec,
            pl.BlockSpec((None, tq, ATT_DV), lambda b_, h, i: (b_, i, h)),
            pl.BlockSpec((None, t, ATT_DV), lambda b_, h, i: (b_, 0, ATT_HEADS + h)),
            pl.BlockSpec((None, t, ATT_DV), lambda b_, h, i: (b_, 0, 2 * ATT_HEADS + h)),
        ],
        out_specs=pl.BlockSpec((None, tq, ATT_DV), lambda b_, h, i: (b_, i, h)),
        out_shape=jax.ShapeDtypeStruct((b, t, ATT_HEADS * ATT_DV), F32),
        scratch_shapes=[pltpu.VMEM((2 * tq, ATT_DV), F32), pltpu.VMEM((2 * tq, 1), F32),
                        pltpu.VMEM((2 * tq, 1), F32)],
        compiler_params=_cparams(("parallel", "parallel", "arbitrary")),
        name="diff_attn_prompt",
    )(lam, gsub, qkv, qkv, qkv)


def _attn_bounded_kernel(lam_ref, gsub_ref, qt_ref, k_ref, vt_ref, o_ref, acc_sc, l_sc, *, tq, post_scale):
    i = pl.program_id(2)
    qt = qt_ref[...].astype(F32)
    row = lax.broadcasted_iota(jnp.int32, qt.shape, 0)
    qbd = jnp.concatenate([jnp.where(row < ATT_DK, qt, 0.0), jnp.where(row >= ATT_DK, qt, 0.0)],
                          axis=1).astype(BF16)
    acc_sc[...] = jnp.zeros_like(acc_sc)
    l_sc[...] = jnp.zeros_like(l_sc)

    def probs(j, masked):
        off = pl.multiple_of(j * tq, tq)
        s = _dot(k_ref[pl.ds(off, tq), :], qbd)
        if masked:
            ki = lax.broadcasted_iota(jnp.int32, s.shape, 0)
            qi = lax.broadcasted_iota(jnp.int32, s.shape, 1) & (tq - 1)
            s = jnp.where(_shr(ki, CHUNK) <= _shr(qi, CHUNK), s, MASK_VALUE)
        p = jnp.exp(s)
        return jnp.sum(p.reshape(tq // 8, 8, 2 * tq), axis=0), _dot(vt_ref[j], p.astype(BF16))

    def accumulate(first, count, masked=False):
        parts = [probs(first + c, masked) for c in range(count)]
        l_sc[...] += functools.reduce(lambda x, y: x + y, [p_[0] for p_ in parts])
        acc_sc[...] += functools.reduce(lambda x, y: x + y, [p_[1] for p_ in parts])

    def quad(jj, carry):
        accumulate(KEY_TILES_PER_ITER * jj, KEY_TILES_PER_ITER)
        return carry

    lax.fori_loop(0, i // KEY_TILES_PER_ITER, quad, 0)
    rem = i % KEY_TILES_PER_ITER

    @pl.when(rem >= 2)
    def _():
        accumulate(i - rem, 2)

    @pl.when(rem % 2 == 1)
    def _():
        accumulate(i - 1, 1)

    accumulate(i, 1, masked=True)
    l = jnp.sum(l_sc[...], axis=0, keepdims=True)
    acc = acc_sc[...]
    ot = acc[:, :tq] / l[:, :tq] - lam_ref[...] * (acc[:, tq:] / l[:, tq:])
    ms = jnp.mean(ot * ot, axis=0, keepdims=True)
    yt = ot * lax.rsqrt(ms + EPS) * (gsub_ref[...] * post_scale)
    o_ref[...] = yt.T


def _attn_bounded(qkv, q5, v5, lam, gsub, post_scale):
    b, t, _ = qkv.shape
    nt, tq = q5.shape[2], q5.shape[4]
    kern = functools.partial(_attn_bounded_kernel, tq=tq, post_scale=post_scale)
    return pl.pallas_call(
        kern,
        grid=(b, ATT_HEADS, nt),
        in_specs=[
            pl.BlockSpec((1, 1), lambda b_, h, i: (0, 0)),
            pl.BlockSpec((ATT_DV, 1), lambda b_, h, i: (0, 0)),
            pl.BlockSpec((None, None, None, ATT_DV, tq), lambda b_, h, i: (b_, h, i, 0, 0)),
            pl.BlockSpec((None, t, ATT_DV), lambda b_, h, i: (b_, 0, ATT_HEADS + h)),
            pl.BlockSpec((None, None, nt, ATT_DV, tq), lambda b_, h, i: (b_, h, 0, 0, 0)),
        ],
        out_specs=pl.BlockSpec((None, tq, ATT_DV), lambda b_, h, i: (b_, i, h)),
        out_shape=jax.ShapeDtypeStruct((b, t, ATT_HEADS * ATT_DV), F32),
        scratch_shapes=[pltpu.VMEM((ATT_DV, 2 * tq), F32), pltpu.VMEM((8, 2 * tq), F32)],
        compiler_params=_cparams(("parallel", "parallel", "arbitrary")),
        name="diff_attn_bounded",
    )(lam[:, :1], gsub.reshape(ATT_DV, 1), q5, qkv, v5)


def _attn_sample_kernel(lam_ref, gsub_ref, q_ref, kn_ref, vn_ref, kp_ref, vp_ref, o_ref, *, tq, post_scale):
    qbd = _stack_maps(q_ref[...])
    sp = _dot_nt(qbd, kp_ref[...])
    sn = _dot_nt(qbd, kn_ref[...])
    m = jnp.maximum(jnp.max(sp, axis=-1, keepdims=True), jnp.max(sn, axis=-1, keepdims=True))
    pp = jnp.exp(sp - m)
    pn = jnp.exp(sn - m)
    l = jnp.sum(pp, axis=-1, keepdims=True) + jnp.sum(pn, axis=-1, keepdims=True)
    acc = _dot(pp.astype(BF16), vp_ref[...]) + _dot(pn.astype(BF16), vn_ref[...])
    o_ref[...] = _attn_finish(acc, l, lam_ref[...], gsub_ref[...], post_scale, tq)


def _attn_sample(qkv, kp, vp, lam, gsub, post_scale):
    b, t, _ = qkv.shape
    tp = kp.shape[1]
    kern = functools.partial(_attn_sample_kernel, tq=t, post_scale=post_scale)
    vec_spec = pl.BlockSpec((1, ATT_DV), lambda b_, h: (0, 0))
    return pl.pallas_call(
        kern,
        grid=(b, ATT_HEADS),
        in_specs=[
            vec_spec, vec_spec,
            pl.BlockSpec((None, t, ATT_DV), lambda b_, h: (b_, 0, h)),
            pl.BlockSpec((None, t, ATT_DV), lambda b_, h: (b_, 0, ATT_HEADS + h)),
            pl.BlockSpec((None, t, ATT_DV), lambda b_, h: (b_, 0, 2 * ATT_HEADS + h)),
            pl.BlockSpec((None, tp, ATT_DV), lambda b_, h: (b_, 0, h)),
            pl.BlockSpec((None, tp, ATT_DV), lambda b_, h: (b_, 0, h)),
        ],
        out_specs=pl.BlockSpec((None, t, ATT_DV), lambda b_, h: (b_, 0, h)),
        out_shape=jax.ShapeDtypeStruct((b, t, ATT_HEADS * ATT_DV), F32),
        compiler_params=_cparams(("parallel", "parallel")),
        name="diff_attn_sample",
    )(lam, gsub, qkv, qkv, qkv, kp, vp)


HG_W = HG_HEADS * HG_DK
HG_V = HG_HEADS * HG_DV
B_PAD = 8


def _level_reference(b_sc, w):
    if w >= 4:
        pieces = []
        for p in range(CHUNK // (2 * w)):
            row = b_sc[pl.ds(B_PAD + p * 2 * w + w, 1), :]
            pieces.append(jnp.broadcast_to(row, (2 * w, HG_W)))
        return jnp.concatenate(pieces, axis=0)
    t = lax.broadcasted_iota(jnp.int32, (CHUNK, HG_W), 0)
    phase = t & (2 * w - 1)
    r = None
    for ph in range(2 * w):
        shifted = b_sc[pl.ds(B_PAD + w - ph, CHUNK), :]
        r = shifted if r is None else jnp.where(phase == ph, shifted, r)
    return r


def _hgrn_chunk(hq, hf, hi, lb, st_sc, b_sc, masks):
    tril, bd_k, bd_v, bd_s = masks
    f = lb + (1.0 - lb) * jax.nn.sigmoid(hf)
    g = jnp.log(f)
    kk = 1.0 - f
    q = _silu(hq) * (HG_DK ** -0.5)
    g0 = g.astype(BF16)
    r1 = g - g0.astype(F32)
    g1 = r1.astype(BF16)
    g2 = (r1 - g1.astype(F32)).astype(BF16)
    b = _dot(tril, g0) + _dot(tril, g1) + _dot(tril, g2)
    b_sc[pl.ds(B_PAD, CHUNK), :] = b

    t_idx = lax.broadcasted_iota(jnp.int32, (CHUNK, HG_W), 0)
    s_idx = lax.broadcasted_iota(jnp.int32, (CHUNK, HG_W), 1) & (CHUNK - 1)
    zero = jnp.zeros((CHUNK, HG_W), F32)

    def block_diag_k(x):
        return jnp.where(bd_k, jnp.concatenate([x] * HG_HEADS, axis=0), jnp.zeros((), F32)).astype(BF16)

    a = jnp.where(t_idx == s_idx, _dot_nt(q.astype(BF16), block_diag_k(kk)), zero)
    w = CHUNK // 2
    while w >= 1:
        r = _level_reference(b_sc, w)
        upper = (t_idx & w) != 0
        e = jnp.exp(jnp.where(upper, b - r, r - b))
        ql = jnp.where(upper, q * e, zero)
        kl = jnp.where(upper, zero, kk * e)
        same_pair = _shr(t_idx, 2 * w) == _shr(s_idx, 2 * w)
        a = a + jnp.where(same_pair, _dot_nt(ql.astype(BF16), block_diag_k(kl)), zero)
        w //= 2

    b_last = jnp.broadcast_to(b_sc[pl.ds(B_PAD + CHUNK - 1, 1), :], (CHUNK, HG_W))
    qb = q * jnp.exp(b)
    kd = kk * jnp.exp(b_last - b)
    v16 = hi
    vbd = jnp.where(bd_v, jnp.concatenate([hi.astype(F32)] * HG_HEADS, axis=0), jnp.zeros((), F32)).astype(BF16)
    st = st_sc[...]
    o = _dot(a.astype(BF16), vbd) + _dot_nt(qb.astype(BF16), st.astype(BF16))
    upd = _dot_tn(v16, kd.astype(BF16))
    decay = jnp.exp(b_sc[pl.ds(B_PAD + CHUNK - 1, 1), :])
    st_sc[...] = st * decay + jnp.where(bd_s, upd, jnp.zeros((), F32))
    return o


def _mix_kernel(x_ref, u_ref, hqf_ref, hi_ref, hg_ref, g_ref, ya_ref, st0_ref, pool0_ref,
                lb_ref, pw_ref, ps_ref, hgn_ref, wup_ref, wua_ref, wuh_ref, wo_ref,
                xo_ref, stn_ref, pooln_ref,
                st_sc, b_sc, ext_sc, yh_sc, *, tm, p0):
    it = pl.program_id(1)

    @pl.when(it == 0)
    def _():
        st_sc[...] = st0_ref[...]
        ext_sc[pl.ds(0, POOL_HIST), :] = pool0_ref[...]
        b_sc[...] = jnp.zeros_like(b_sc)

    u = u_ref[...]
    ext_sc[pl.ds(POOL_HIST, tm), :] = u
    row = lax.broadcasted_iota(jnp.int32, (tm, LANES), 0)
    seen = (p0 + 1 + it * tm + row).astype(F32)
    cols = []
    for gi, w in enumerate(POOL_WINDOWS):
        sl = pl.ds(gi * LANES, LANES)
        win = u[:, gi * LANES:(gi + 1) * LANES]
        for d in range(1, w):
            win = win + ext_sc[pl.ds(POOL_HIST - d, tm), sl]
        dlt = win / jnp.minimum(seen, float(w)) - u[:, gi * LANES:(gi + 1) * LANES]
        cols.append(_dot(dlt.astype(BF16), pw_ref[gi]))
    y_pool = jnp.concatenate(cols, axis=1) * ps_ref[...]
    ext_sc[pl.ds(0, POOL_HIST), :] = ext_sc[pl.ds(tm, POOL_HIST), :]

    tri_r = lax.broadcasted_iota(jnp.int32, (CHUNK, CHUNK), 0)
    tri_c = lax.broadcasted_iota(jnp.int32, (CHUNK, CHUNK), 1)
    tril = (tri_c <= tri_r).astype(BF16)
    rk = lax.broadcasted_iota(jnp.int32, (HG_W, HG_W), 0)
    ck = lax.broadcasted_iota(jnp.int32, (HG_W, HG_W), 1)
    bd_k = _shr(rk, CHUNK) == _shr(ck, HG_DK)
    rv = lax.broadcasted_iota(jnp.int32, (HG_W, HG_V), 0)
    cv = lax.broadcasted_iota(jnp.int32, (HG_W, HG_V), 1)
    bd_v = _shr(rv, CHUNK) == _shr(cv, HG_DV)
    rs = lax.broadcasted_iota(jnp.int32, (HG_V, HG_W), 0)
    cs = lax.broadcasted_iota(jnp.int32, (HG_V, HG_W), 1)
    bd_s = _shr(rs, HG_DV) == _shr(cs, HG_DK)
    masks = (tril, bd_k, bd_v, bd_s)
    lb = lb_ref[...]
    for c in range(tm // CHUNK):
        rows = pl.ds(c * CHUNK, CHUNK)
        o = _hgrn_chunk(hqf_ref[rows, pl.ds(0, HG_W)], hqf_ref[rows, pl.ds(HG_W, HG_W)],
                        hi_ref[rows, :], lb, st_sc, b_sc, masks)
        yh_sc[rows, :] = o
    oh = yh_sc[...]
    hg = hg_ref[...].astype(F32)
    heads = []
    for h in range(HG_HEADS):
        sl = slice(h * HG_DV, (h + 1) * HG_DV)
        heads.append(_rms(oh[:, sl], hgn_ref[...]) * _silu(hg[:, sl]))
    y_hg = jnp.concatenate(heads, axis=1)

    gts = g_ref[...].astype(F32)
    merged = (_sigmoid(gts[:, :D_MODEL]) * _dot(y_pool.astype(BF16), wup_ref[...])
              + _sigmoid(gts[:, D_MODEL:2 * D_MODEL]) * _dot(ya_ref[...].astype(BF16), wua_ref[...])
              + _sigmoid(gts[:, 2 * D_MODEL:]) * _dot(y_hg.astype(BF16), wuh_ref[...]))
    xo_ref[...] = x_ref[...] + _dot(merged.astype(BF16), wo_ref[...])

    @pl.when(it == pl.num_programs(1) - 1)
    def _():
        stn_ref[...] = st_sc[...]
        pooln_ref[...] = ext_sc[pl.ds(0, POOL_HIST), :]


def _mix_out(x, proj_a, proj_b, gates, y_att, st0, pool0, lb, pool_w, pool_scale, hg_outn,
             w_up_pool, w_up_att, w_up_hgrn, w_out, *, tm, p0):
    b, t, _ = x.shape
    kern = functools.partial(_mix_kernel, tm=tm, p0=p0)

    def rows(width, col):
        return pl.BlockSpec((None, tm, width), lambda b_, i: (b_, i, col))

    def const(shape):
        return pl.BlockSpec(shape, lambda b_, i: (0,) * len(shape))

    def per_batch(shape):
        return pl.BlockSpec((None,) + shape, lambda b_, i: (b_, 0, 0))

    return pl.pallas_call(
        kern,
        grid=(b, t // tm),
        in_specs=[
            rows(D_MODEL, 0),
            rows(SEG, 0),
            rows(SEG, 1),
            rows(SEG, 0),
            rows(SEG, 1),
            rows(3 * D_MODEL, 0),
            rows(SEG, 0),
            per_batch((HG_V, HG_W)),
            per_batch((POOL_HIST, SEG)),
            const((1, HG_W)), const((4, LANES, LANES)), const((1, SEG)), const((1, HG_DV)),
            const((SEG, D_MODEL)), const((SEG, D_MODEL)), const((SEG, D_MODEL)), const((D_MODEL, D_MODEL)),
        ],
        out_specs=[
            rows(D_MODEL, 0),
            per_batch((HG_V, HG_W)),
            per_batch((POOL_HIST, SEG)),
        ],
        out_shape=[
            jax.ShapeDtypeStruct((b, t, D_MODEL), F32),
            jax.ShapeDtypeStruct((b, HG_V, HG_W), F32),
            jax.ShapeDtypeStruct((b, POOL_HIST, SEG), F32),
        ],
        scratch_shapes=[
            pltpu.VMEM((HG_V, HG_W), F32),
            pltpu.VMEM((CHUNK + 2 * B_PAD, HG_W), F32),
            pltpu.VMEM((POOL_HIST + tm, SEG), F32),
            pltpu.VMEM((tm, HG_V), F32),
        ],
        compiler_params=_cparams(("parallel", "arbitrary")),
        name="mix_out",
    )(x, proj_a, proj_a, proj_b, proj_b, gates, y_att, st0, pool0,
      lb, pool_w, pool_scale, hg_outn, w_up_pool, w_up_att, w_up_hgrn, w_out)


def _state_to_block_diag(s):
    b = s.shape[0]
    st = jnp.swapaxes(s, 2, 3)
    eye = jnp.eye(HG_HEADS, dtype=s.dtype)
    return jnp.einsum('bhed,hg->bhegd', st, eye).reshape(b, HG_V, HG_W)


def _block_diag_to_state(st):
    b = st.shape[0]
    s5 = st.reshape(b, HG_HEADS, HG_DV, HG_HEADS, HG_DK)
    diag = jnp.stack([s5[:, h, :, h, :] for h in range(HG_HEADS)], axis=1)
    return jnp.swapaxes(diag, 2, 3)


def _head_rms(x, gain):
    return jnp.concatenate(
        [_rms(x[:, h * X_HD:(h + 1) * X_HD], gain) for h in range(X_HEADS)], axis=1)


def _memkv_kernel(m_ref, g_ref, w_ref, kn_ref, k_ref, v_ref, k16_ref, v16_ref):
    h = _rms(m_ref[...], g_ref[...]).astype(BF16)
    kv = _dot(h, w_ref[...])
    mk = _head_rms(kv[:, :D_MODEL], kn_ref[...])
    mv = kv[:, D_MODEL:]
    k_ref[...] = mk
    v_ref[...] = mv
    k16_ref[...] = mk.astype(BF16)
    v16_ref[...] = mv.astype(BF16)


def _memory_kv(mem, g, w_ckv, kn):
    b, n, _ = mem.shape
    blk = pl.BlockSpec((None, n, D_MODEL), lambda b_: (b_, 0, 0))
    return pl.pallas_call(
        _memkv_kernel,
        grid=(b,),
        in_specs=[
            blk,
            pl.BlockSpec((1, D_MODEL), lambda b_: (0, 0)),
            pl.BlockSpec((D_MODEL, 2 * D_MODEL), lambda b_: (0, 0)),
            pl.BlockSpec((1, X_HD), lambda b_: (0, 0)),
        ],
        out_specs=[blk, blk, blk, blk],
        out_shape=[jax.ShapeDtypeStruct((b, n, D_MODEL), F32)] * 2
        + [jax.ShapeDtypeStruct((b, n, D_MODEL), BF16)] * 2,
        compiler_params=_cparams(("parallel",)),
        name="memory_kv",
    )(mem, g.reshape(1, D_MODEL), w_ckv, kn.reshape(1, X_HD))


def _cross_kernel(x_ref, g_ref, wq_ref, qn_ref, mk_ref, mv_ref, wo_ref, o_ref):
    x = x_ref[...]
    q = _head_rms(_dot(_rms(x, g_ref[...]).astype(BF16), wq_ref[...]), qn_ref[...])
    q = (q * (X_HD ** -0.5)).astype(BF16)
    outs = []
    for h in range(X_HEADS):
        sl = slice(h * X_HD, (h + 1) * X_HD)
        s = _dot_nt(q[:, sl], mk_ref[:, sl])
        p = jnp.exp(s - jnp.max(s, axis=-1, keepdims=True))
        p = p / jnp.sum(p, axis=-1, keepdims=True)
        outs.append(_dot(p.astype(BF16), mv_ref[:, sl]))
    o = jnp.concatenate(outs, axis=1)
    o_ref[...] = x + _dot(o.astype(BF16), wo_ref[...])


def _cross_attend(x, mk, mv, g, w_cq, qn, w_co, *, tm):
    b, t, _ = x.shape
    n = mk.shape[1]
    rows = pl.BlockSpec((None, tm, D_MODEL), lambda b_, i: (b_, i, 0))
    mem = pl.BlockSpec((None, n, D_MODEL), lambda b_, i: (b_, 0, 0))
    wsq = pl.BlockSpec((D_MODEL, D_MODEL), lambda b_, i: (0, 0))
    return pl.pallas_call(
        _cross_kernel,
        grid=(b, t // tm),
        in_specs=[
            rows,
            pl.BlockSpec((1, D_MODEL), lambda b_, i: (0, 0)),
            wsq,
            pl.BlockSpec((1, X_HD), lambda b_, i: (0, 0)),
            mem, mem, wsq,
        ],
        out_specs=rows,
        out_shape=jax.ShapeDtypeStruct((b, t, D_MODEL), F32),
        compiler_params=_cparams(("parallel", "parallel")),
        name="cross_attn",
    )(x, g.reshape(1, D_MODEL), w_cq, qn.reshape(1, X_HD), mk, mv, w_co)


def _layer(x, l, depth, p0, rope, kv_rows, past_kv, pool_prev, hg_state, lb, mk16, mv16, w, *, tiles):
    b, t, _ = x.shape
    n = b * t
    x2 = _ffn_half(x.reshape(n, D_MODEL), w['norm_ffn1'], w['w_ffn1_in'], w['w_ffn1_out'],
                   tm=tiles['ffn'], tf=tiles['tf'])

    proj_a, proj_b, kf, vf, gates, qkv, *tiles_t = _in_proj(
        x2, w['norm_mix'], w['w_in'], w['group_mean'], w['att_qn'], w['att_kn'], *rope, kv_rows,
        tm=tiles['proj'], seq_len=t, tq=None if past_kv is not None else tiles['attn_bounded'],
        layer=l, depth=depth)

    lam_init = 0.8 - 0.6 * math.exp(-0.3 * l)
    lam = (jnp.exp(jnp.sum(w['lq1'] * w['lk1'])) - jnp.exp(jnp.sum(w['lq2'] * w['lk2'])) + lam_init)
    lam = jnp.full((1, ATT_DV), lam, F32)
    gsub = w['att_subln'].reshape(1, ATT_DV)
    qkv3 = qkv.reshape(b, t, 3 * SEG)
    if past_kv is None:
        bound = (ATT_DK ** 0.5) * jnp.max(jnp.abs(w['att_qn'])) * jnp.max(jnp.abs(w['att_kn']))
        y_att = lax.cond(
            bound <= SCORE_BOUND_MAX,
            lambda a, q5, v5: _attn_bounded(a, q5, v5, lam, gsub, 1.0 - lam_init),
            lambda a, q5, v5: _attn_prompt(a, lam, gsub, 1.0 - lam_init, tq=tiles['attn']),
            qkv3, *tiles_t)
    else:
        y_att = _attn_sample(qkv3, past_kv[0], past_kv[1], lam, gsub, 1.0 - lam_init)

    x3, st_new, pool_new = _mix_out(
        x2.reshape(b, t, D_MODEL), proj_a.reshape(b, t, 2 * SEG), proj_b.reshape(b, t, 2 * SEG),
        gates.reshape(b, t, N_SEG_G * SEG),
        y_att, _state_to_block_diag(hg_state), pool_prev, lb,
        w['pool_w'], w['pool_scale'], w['hg_outn'], w['w_up_pool'], w['w_up_att'], w['w_up_hgrn'], w['w_out'],
        tm=tiles['mix'], p0=p0)

    x4 = _cross_attend(x3, mk16, mv16, w['norm_cross'], w['w_cq'], w['cross_qn'], w['w_co'], tm=tiles['cross'])
    x5 = _ffn_half(x4.reshape(n, D_MODEL), w['norm_ffn2'], w['w_ffn2_in'], w['w_ffn2_out'],
                   tm=tiles['ffn'], tf=tiles['tf'])
    return (x5.reshape(b, t, D_MODEL), (kf, vf), pool_new[:, POOL_HIST - POOL_STATE:],
            _block_diag_to_state(st_new))


PROMPT_TILES = dict(ffn=1024, tf=256, proj=512, attn=256, attn_bounded=512, mix=256, cross=512)
SAMPLE_TILES = dict(ffn=512, tf=256, proj=512, attn=64, mix=64, cross=64)


def kernel(x_prompt, x_sample, cache_attn_k, cache_attn_v, cache_mem_k, cache_mem_v, state_pool, state_hgrn, mem_prompt, norm_ffn1, w_ffn1_in, w_ffn1_out, norm_mix, w_in, pool_w, pool_scale, att_q_norm, att_k_norm, lambda_q1, lambda_k1, lambda_q2, lambda_k2, att_subln, hgrn_lower, hgrn_out_norm, w_up_pool, w_up_att, w_up_hgrn, w_out, norm_cross, norm_mem, w_cq, w_ckv, cross_q_norm, cross_k_norm, w_co, norm_ffn2, w_ffn2_in, w_ffn2_out):
    depth = w_in.shape[0]
    bp = x_prompt.shape[0]
    bs = x_sample.shape[0]
    p0_sample = cache_attn_k.shape[2]

    lp = jax.nn.softmax(hgrn_lower.astype(F32), axis=0)
    lbs = jnp.cumsum(lp, axis=0) - lp[0:1]

    gidx = jnp.arange(SEG) // ATT_DK
    group_mean = ((gidx[:, None] == gidx[None, :]).astype(F32) / ATT_DK).astype(BF16)

    def layer_weights(l):
        return dict(
            norm_ffn1=norm_ffn1[l], w_ffn1_in=w_ffn1_in[l].astype(BF16), w_ffn1_out=w_ffn1_out[l].astype(BF16),
            norm_mix=norm_mix[l], w_in=w_in[l].astype(BF16), group_mean=group_mean,
            att_qn=jnp.tile(att_q_norm[l], SEG // ATT_DK).reshape(1, SEG),
            att_kn=jnp.tile(att_k_norm[l], SEG // ATT_DK).reshape(1, SEG),
            lq1=lambda_q1[l].astype(F32), lk1=lambda_k1[l].astype(F32),
            lq2=lambda_q2[l].astype(F32), lk2=lambda_k2[l].astype(F32),
            att_subln=att_subln[l],
            pool_w=pool_w[l].astype(BF16), pool_scale=pool_scale[l].reshape(1, SEG),
            hg_outn=hgrn_out_norm[l].reshape(1, HG_DV),
            w_up_pool=w_up_pool[l].astype(BF16), w_up_att=w_up_att[l].astype(BF16),
            w_up_hgrn=w_up_hgrn[l].astype(BF16), w_out=w_out[l].astype(BF16),
            norm_cross=norm_cross[l], w_cq=w_cq[l].astype(BF16), cross_qn=cross_q_norm[l],
            w_co=w_co[l].astype(BF16),
            norm_ffn2=norm_ffn2[l], w_ffn2_in=w_ffn2_in[l].astype(BF16), w_ffn2_out=w_ffn2_out[l].astype(BF16),
        )

    weights = [layer_weights(l) for l in range(depth)]

    def rope_for(p0, t, tm):
        tabs = _rope_tables(p0, t)
        if t < tm:
            tabs = tuple(jnp.concatenate([a] * (tm // t), axis=0) for a in tabs)
        return tabs

    rope_prompt = rope_for(0, x_prompt.shape[1], PROMPT_TILES['proj'])
    rope_sample = rope_for(p0_sample, x_sample.shape[1], SAMPLE_TILES['proj'])

    y = x_prompt
    pkv = None
    pmk, pmv, ppool, phg = [], [], [], []
    for l in range(depth):
        mk, mv, mk16, mv16 = _memory_kv(mem_prompt, norm_mem[l], w_ckv[l].astype(BF16), cross_k_norm[l])
        pool0 = jnp.zeros((bp, POOL_HIST, SEG), F32)
        hg0 = jnp.zeros((bp, HG_HEADS, HG_DK, HG_DV), F32)
        y, pkv, pn, hn = _layer(y, l, depth, 0, rope_prompt, pkv, None, pool0, hg0, lbs[l].reshape(1, HG_W),
                                mk16, mv16, weights[l], tiles=PROMPT_TILES)
        pmk.append(mk.reshape(bp, -1, X_HEADS, X_HD)); pmv.append(mv.reshape(bp, -1, X_HEADS, X_HD))
        ppool.append(pn); phg.append(hn)
    y_prompt = y
    kv_shape = (depth, bp, x_prompt.shape[1], ATT_HEADS, ATT_DV)
    pk, pv = pkv[0].reshape(kv_shape), pkv[1].reshape(kv_shape)

    y = x_sample
    skv = None
    spool, shg = [], []
    for l in range(depth):
        past = (cache_attn_k[l].reshape(bs, p0_sample, ATT_HEADS * 2 * ATT_DK).astype(BF16),
                cache_attn_v[l].reshape(bs, p0_sample, ATT_HEADS * ATT_DV).astype(BF16))
        pool0 = jnp.pad(state_pool[l], ((0, 0), (POOL_HIST - POOL_STATE, 0), (0, 0)))
        mk16 = cache_mem_k[l].reshape(bs, -1, D_MODEL).astype(BF16)
        mv16 = cache_mem_v[l].reshape(bs, -1, D_MODEL).astype(BF16)
        y, skv, pn, hn = _layer(y, l, depth, p0_sample, rope_sample, skv, past, pool0, state_hgrn[l],
                                lbs[l].reshape(1, HG_W), mk16, mv16, weights[l], tiles=SAMPLE_TILES)
        spool.append(pn); shg.append(hn)
    y_sample = y
    kv_shape = (depth, bs, x_sample.shape[1], ATT_HEADS, ATT_DV)
    sk, sv = skv[0].reshape(kv_shape), skv[1].reshape(kv_shape)

    return (y_prompt, y_sample,
            pk, pv, jnp.stack(pmk), jnp.stack(pmv),
            jnp.stack(ppool), jnp.stack(phg),
            sk, sv, jnp.stack(spool), jnp.stack(shg))
```

```python
import functools
import math

import jax
import jax.numpy as jnp
from jax import lax
from jax.experimental import pallas as pl
from jax.experimental.pallas import tpu as pltpu

F32 = jnp.float32
BF16 = jnp.bfloat16

D_MODEL = 1024
CHUNK = 64
EPS = 1e-6
MASK_VALUE = -1e30
POOL_WINDOWS = (2, 4, 8, 16)
POOL_STATE = 15
POOL_HIST = 16
ATT_HEADS = 4
ATT_DK = 64
ATT_DV = 128
ROT_DIMS = 16
ROPE_THETA = 500000.0
HG_HEADS = 4
HG_DK = 64
HG_DV = 128
X_HEADS = 4
X_HD = 256
D_FF = 2816
SEG = 512
N_SEG_A = 7
N_SEG_G = 6
LANES = 128
VMEM_LIMIT = 56 * 1024 * 1024
SCORE_BOUND_MAX = 20.0
KEY_TILES_PER_ITER = 4


def _cparams(sem):
    return pltpu.CompilerParams(dimension_semantics=sem, vmem_limit_bytes=VMEM_LIMIT)


def _rms(x, g):
    ms = jnp.mean(x * x, axis=-1, keepdims=True)
    return x * lax.rsqrt(ms + EPS) * g


def _sigmoid(x):
    return 0.5 * jnp.tanh(0.5 * x) + 0.5


def _silu(x):
    return x * _sigmoid(x)


def _dot(a, b):
    return jnp.dot(a, b, preferred_element_type=F32)


def _dot_nt(a, b):
    return lax.dot_general(a, b, (((1,), (1,)), ((), ())), preferred_element_type=F32)


def _dot_tn(a, b):
    return lax.dot_general(a, b, (((0,), (0,)), ((), ())), preferred_element_type=F32)


def _shr(x, pow2):
    return lax.shift_right_logical(x, jnp.int32(int(math.log2(pow2))))


def _ffn_kernel(x_ref, g_ref, wi_ref, wo_ref, o_ref, *, tf):
    x = x_ref[...]
    h = _rms(x, g_ref[...]).astype(BF16)
    acc = None
    for c in range(D_FF // tf):
        a = _dot(h, wi_ref[:, pl.ds(c * tf, tf)])
        b = _dot(h, wi_ref[:, pl.ds(D_FF + c * tf, tf)])
        part = _dot((_silu(a) * b).astype(BF16), wo_ref[pl.ds(c * tf, tf), :])
        acc = part if acc is None else acc + part
    o_ref[...] = x + 0.5 * acc


def _resident(shape):
    return pl.BlockSpec(shape, lambda *_: (0,) * len(shape), pipeline_mode=pl.Buffered(1))


def _ffn_half(x, g, w_i, w_o, *, tm, tf):
    n = x.shape[0]
    return pl.pallas_call(
        functools.partial(_ffn_kernel, tf=tf),
        grid=(n // tm,),
        in_specs=[
            pl.BlockSpec((tm, D_MODEL), lambda i: (i, 0)),
            _resident((1, D_MODEL)),
            _resident((D_MODEL, 2 * D_FF)),
            _resident((D_FF, D_MODEL)),
        ],
        out_specs=pl.BlockSpec((tm, D_MODEL), lambda i: (i, 0)),
        out_shape=jax.ShapeDtypeStruct((n, D_MODEL), F32),
        compiler_params=_cparams(("parallel",)),
        name="ffn_half",
    )(x, g.reshape(1, D_MODEL), w_i, w_o)


def _inproj_kernel(x_ref, g_ref, w_ref, gm_ref, qn_ref, kn_ref, cos_ref, sa_ref, sb_ref, *rest, tq, chained):
    rest = rest[2:] if chained else rest
    pa_ref, pb_ref, kf_ref, vf_ref, pg_ref, qkv_ref = rest[:6]
    qt_ref, vt_ref = rest[6:8] if tq is not None else (None, None)

    def store_head_rows(ref, y):
        for h in range(ATT_HEADS):
            ref[pl.ds(h, y.shape[0], stride=ATT_HEADS), :] = y[:, h * ATT_DV:(h + 1) * ATT_DV]

    def store_transposed(ref, y):
        if ref is None:
            return
        yt = y.T.astype(BF16)
        for h in range(ATT_HEADS):
            for c in range(y.shape[0] // tq):
                ref[h, c] = yt[h * ATT_DV:(h + 1) * ATT_DV, c * tq:(c + 1) * tq]

    h = _rms(x_ref[...], g_ref[...]).astype(BF16)

    def proj(seg):
        return _dot(h, w_ref[:, pl.ds(seg * SEG, SEG)])

    rep = SEG // LANES
    cos = jnp.concatenate([cos_ref[...]] * rep, axis=1)
    sa = jnp.concatenate([sa_ref[...]] * rep, axis=1)
    sb = jnp.concatenate([sb_ref[...]] * rep, axis=1)

    def norm_rope(y, gain):
        ms = _dot((y * y).astype(BF16), gm_ref[...])
        yn = y * lax.rsqrt(ms + EPS) * gain
        half = ROT_DIMS // 2
        return yn * cos + pltpu.roll(yn, half, 1) * sa + pltpu.roll(yn, SEG - half, 1) * sb

    pa_ref[:, pl.ds(0, SEG)] = proj(0)

    q = norm_rope(proj(1), qn_ref[...]) * (ATT_DK ** -0.5)
    qkv_ref[:, pl.ds(0, SEG)] = q.astype(BF16)
    store_transposed(qt_ref, q)

    k = norm_rope(proj(2), kn_ref[...])
    store_head_rows(kf_ref, k)
    qkv_ref[:, pl.ds(SEG, SEG)] = k.astype(BF16)

    v = proj(3)
    store_head_rows(vf_ref, v)
    qkv_ref[:, pl.ds(2 * SEG, SEG)] = v.astype(BF16)
    store_transposed(vt_ref, v)

    pa_ref[:, pl.ds(SEG, SEG)] = proj(4)
    pb_ref[:, pl.ds(0, SEG)] = proj(5).astype(BF16)
    pb_ref[:, pl.ds(SEG, SEG)] = proj(6).astype(BF16)
    for s in range(N_SEG_G):
        pg_ref[:, pl.ds(s * SEG, SEG)] = proj(N_SEG_A + s).astype(BF16)


def _in_proj(x, g, w, gm, qn, kn, cos, sa, sb, kv_rows, *, tm, seq_len, tq, layer, depth):
    n = x.shape[0]
    tab_blocks = cos.shape[0] // tm
    nseg = N_SEG_A + N_SEG_G
    tab_spec = pl.BlockSpec((tm, LANES), lambda i: (i % tab_blocks, 0))
    kv_spec = pl.BlockSpec((tm * ATT_HEADS, ATT_DV), lambda i: (layer * (n // tm) + i, 0))
    kv_shape = jax.ShapeDtypeStruct((depth * n * ATT_HEADS, ATT_DV), F32)

    def rows(width):
        return pl.BlockSpec((tm, width), lambda i: (i, 0))

    out_specs = [rows(2 * SEG), rows(2 * SEG), kv_spec, kv_spec, rows(N_SEG_G * SEG), rows(3 * SEG)]
    out_shape = [
        jax.ShapeDtypeStruct((n, 2 * SEG), F32),
        jax.ShapeDtypeStruct((n, 2 * SEG), BF16),
        kv_shape, kv_shape,
        jax.ShapeDtypeStruct((n, N_SEG_G * SEG), BF16),
        jax.ShapeDtypeStruct((n, 3 * SEG), BF16),
    ]
    chained = kv_rows is not None
    n_in = 9
    any_spec = pl.BlockSpec(memory_space=pl.ANY)
    extra_in = list(kv_rows) if chained else []
    aliases = {n_in: 2, n_in + 1: 3} if chained else {}
    if tq is not None:
        per_seq = seq_len // tm
        t_spec = pl.BlockSpec((None, ATT_HEADS, tm // tq, ATT_DV, tq),
                              lambda i: (i // per_seq, 0, i % per_seq, 0, 0))
        t_shape = jax.ShapeDtypeStruct((n // seq_len, ATT_HEADS, seq_len // tq, ATT_DV, tq), BF16)
        out_specs += [t_spec, t_spec]
        out_shape += [t_shape, t_shape]
    return pl.pallas_call(
        functools.partial(_inproj_kernel, tq=tq, chained=chained),
        grid=(n // tm,),
        in_specs=[
            rows(D_MODEL),
            _resident((1, D_MODEL)),
            _resident((D_MODEL, nseg * SEG)),
            _resident((SEG, SEG)),
            _resident((1, SEG)), _resident((1, SEG)), tab_spec, tab_spec, tab_spec,
        ] + [any_spec] * len(extra_in),
        out_specs=out_specs,
        out_shape=out_shape,
        input_output_aliases=aliases,
        compiler_params=_cparams(("parallel",)),
        name="in_proj",
    )(x, g.reshape(1, D_MODEL), w, gm, qn, kn, cos, sa, sb, *extra_in)


def _rope_tables(p0, t):
    half = ROT_DIMS // 2
    inv = ROPE_THETA ** (-jnp.arange(half, dtype=F32) / half)
    in_head = jnp.arange(LANES) % ATT_DK
    pos = p0 + jnp.arange(t, dtype=jnp.int32)
    ang = pos.astype(F32)[:, None] * inv[in_head % half][None, :]
    cos, sin = jnp.cos(ang), jnp.sin(ang)
    lower = (in_head < half)[None, :]
    upper = jnp.logical_and(in_head >= half, in_head < ROT_DIMS)[None, :]
    return (jnp.where(in_head[None, :] < ROT_DIMS, cos, 1.0),
            jnp.where(upper, sin, 0.0),
            jnp.where(lower, -sin, 0.0))


def _stack_maps(q):
    lane = lax.broadcasted_iota(jnp.int32, q.shape, 1)
    zero = jnp.zeros_like(q)
    return jnp.concatenate([jnp.where(lane < ATT_DK, q, zero), jnp.where(lane >= ATT_DK, q, zero)], axis=0)


def _attn_finish(acc, l, lam, gsub, post_scale, tq):
    o = acc[:tq] / l[:tq] - lam * (acc[tq:] / l[tq:])
    return _rms(o, gsub) * post_scale


def _attn_prompt_kernel(lam_ref, gsub_ref, q_ref, k_ref, v_ref, o_ref, acc_sc, m_sc, l_sc, *, tq, post_scale):
    i = pl.program_id(2)
    qbd = _stack_maps(q_ref[...])
    m_sc[...] = jnp.full_like(m_sc, MASK_VALUE)
    l_sc[...] = jnp.zeros_like(l_sc)
    acc_sc[...] = jnp.zeros_like(acc_sc)

    def step(j, masked):
        off = pl.multiple_of(j * tq, tq)
        kj = k_ref[pl.ds(off, tq), :]
        vj = v_ref[pl.ds(off, tq), :]
        s = _dot_nt(qbd, kj)
        if masked:
            qi = lax.broadcasted_iota(jnp.int32, s.shape, 0) & (tq - 1)
            ki = lax.broadcasted_iota(jnp.int32, s.shape, 1)
            s = jnp.where(_shr(ki, CHUNK) <= _shr(qi, CHUNK), s, MASK_VALUE)
        m_prev = m_sc[...]
        m_new = jnp.maximum(m_prev, jnp.max(s, axis=-1, keepdims=True))
        alpha = jnp.exp(m_prev - m_new)
        p = jnp.exp(s - m_new)
        l_sc[...] = alpha * l_sc[...] + jnp.sum(p, axis=-1, keepdims=True)
        acc_sc[...] = alpha * acc_sc[...] + _dot(p.astype(BF16), vj)
        m_sc[...] = m_new

    def body(j, carry):
        step(j, False)
        return carry

    lax.fori_loop(0, i, body, 0)
    step(i, True)
    o_ref[...] = _attn_finish(acc_sc[...], l_sc[...], lam_ref[...], gsub_ref[...], post_scale, tq)


def _attn_prompt(qkv, lam, gsub, post_scale, *, tq):
    b, t, _ = qkv.shape
    kern = functools.partial(_attn_prompt_kernel, tq=tq, post_scale=post_scale)
    vec_spec = pl.BlockSpec((1, ATT_DV), lambda b_, h, i: (0, 0))
    return pl.pallas_call(
        kern,
        grid=(b, ATT_HEADS, t // tq),
        in_specs=[
            vec_spec, vec_spec,
            pl.BlockSpec((None, tq, ATT_DV), lambda b_, h, i: (b_, i, h)),
            pl.BlockSpec((None, t, ATT_DV), lambda b_, h, i: (b_, 0, ATT_HEADS + h)),
            pl.BlockSpec((None, t, ATT_DV), lambda b_, h, i: (b_, 0, 2 * ATT_HEADS + h)),
        ],
        out_specs=pl.BlockSpec((None, tq, ATT_DV), lambda b_, h, i: (b_, i, h)),
        out_shape=jax.ShapeDtypeStruct((b, t, ATT_HEADS * ATT_DV), F32),
        scratch_shapes=[pltpu.VMEM((2 * tq, ATT_DV), F32), pltpu.VMEM((2 * tq, 1), F32),
                        pltpu.VMEM((2 * tq, 1), F32)],
        compiler_params=_cparams(("parallel", "parallel", "arbitrary")),
        name="diff_attn_prompt",
    )(lam, gsub, qkv, qkv, qkv)


def _attn_bounded_kernel(lam_ref, gsub_ref, qt_ref, k_ref, vt_ref, o_ref, acc_sc, l_sc, *, tq, post_scale):
    i = pl.program_id(2)
    qt = qt_ref[...].astype(F32)
    row = lax.broadcasted_iota(jnp.int32, qt.shape, 0)
    qbd = jnp.concatenate([jnp.where(row < ATT_DK, qt, 0.0), jnp.where(row >= ATT_DK, qt, 0.0)],
                          axis=1).astype(BF16)
    acc_sc[...] = jnp.zeros_like(acc_sc)
    l_sc[...] = jnp.zeros_like(l_sc)

    def probs(j, masked):
        off = pl.multiple_of(j * tq, tq)
        s = _dot(k_ref[pl.ds(off, tq), :], qbd)
        if masked:
            ki = lax.broadcasted_iota(jnp.int32, s.shape, 0)
            qi = lax.broadcasted_iota(jnp.int32, s.shape, 1) & (tq - 1)
            s = jnp.where(_shr(ki, CHUNK) <= _shr(qi, CHUNK), s, MASK_VALUE)
        p = jnp.exp(s)
        return jnp.sum(p.reshape(tq // 8, 8, 2 * tq), axis=0), _dot(vt_ref[j], p.astype(BF16))

    def accumulate(first, count, diagonal_last=False):
        parts = [probs(first + c, diagonal_last and c == count - 1) for c in range(count)]
        l_sc[...] += functools.reduce(lambda x, y: x + y, [p_[0] for p_ in parts])
        acc_sc[...] += functools.reduce(lambda x, y: x + y, [p_[1] for p_ in parts])

    def group(jj, carry):
        accumulate(KEY_TILES_PER_ITER * jj, KEY_TILES_PER_ITER)
        return carry

    lax.fori_loop(0, i // KEY_TILES_PER_ITER, group, 0)
    rem = i % KEY_TILES_PER_ITER
    for r in range(KEY_TILES_PER_ITER):
        @pl.when(rem == r)
        def _(r=r):
            accumulate(i - r, r + 1, diagonal_last=True)
    l = jnp.sum(l_sc[...], axis=0, keepdims=True)
    acc = acc_sc[...]
    ot = acc[:, :tq] / l[:, :tq] - lam_ref[...] * (acc[:, tq:] / l[:, tq:])
    ms = jnp.mean(ot * ot, axis=0, keepdims=True)
    yt = ot * lax.rsqrt(ms + EPS) * (gsub_ref[...] * post_scale)
    o_ref[...] = yt.T


def _attn_bounded(qkv, q5, v5, lam, gsub, post_scale):
    b, t, _ = qkv.shape
    nt, tq = q5.shape[2], q5.shape[4]
    kern = functools.partial(_attn_bounded_kernel, tq=tq, post_scale=post_scale)
    return pl.pallas_call(
        kern,
        grid=(b, ATT_HEADS, nt),
        in_specs=[
            pl.BlockSpec((1, 1), lambda b_, h, i: (0, 0)),
            pl.BlockSpec((ATT_DV, 1), lambda b_, h, i: (0, 0)),
            pl.BlockSpec((None, None, None, ATT_DV, tq), lambda b_, h, i: (b_, h, i, 0, 0)),
            pl.BlockSpec((None, t, ATT_DV), lambda b_, h, i: (b_, 0, ATT_HEADS + h)),
            pl.BlockSpec((None, None, nt, ATT_DV, tq), lambda b_, h, i: (b_, h, 0, 0, 0)),
        ],
        out_specs=pl.BlockSpec((None, tq, ATT_DV), lambda b_, h, i: (b_, i, h)),
        out_shape=jax.ShapeDtypeStruct((b, t, ATT_HEADS * ATT_DV), F32),
        scratch_shapes=[pltpu.VMEM((ATT_DV, 2 * tq), F32), pltpu.VMEM((8, 2 * tq), F32)],
        compiler_params=_cparams(("parallel", "parallel", "arbitrary")),
        name="diff_attn_bounded",
    )(lam[:, :1], gsub.reshape(ATT_DV, 1), q5, qkv, v5)


def _attn_sample_kernel(lam_ref, gsub_ref, q_ref, kn_ref, vn_ref, kp_ref, vp_ref, o_ref, *, tq, post_scale):
    qbd = _stack_maps(q_ref[...])
    sp = _dot_nt(qbd, kp_ref[...])
    sn = _dot_nt(qbd, kn_ref[...])
    m = jnp.maximum(jnp.max(sp, axis=-1, keepdims=True), jnp.max(sn, axis=-1, keepdims=True))
    pp = jnp.exp(sp - m)
    pn = jnp.exp(sn - m)
    l = jnp.sum(pp, axis=-1, keepdims=True) + jnp.sum(pn, axis=-1, keepdims=True)
    acc = _dot(pp.astype(BF16), vp_ref[...]) + _dot(pn.astype(BF16), vn_ref[...])
    o_ref[...] = _attn_finish(acc, l, lam_ref[...], gsub_ref[...], post_scale, tq)


def _attn_sample(qkv, kp, vp, lam, gsub, post_scale):
    b, t, _ = qkv.shape
    tp = kp.shape[1]
    kern = functools.partial(_attn_sample_kernel, tq=t, post_scale=post_scale)
    vec_spec = pl.BlockSpec((1, ATT_DV), lambda b_, h: (0, 0))
    return pl.pallas_call(
        kern,
        grid=(b, ATT_HEADS),
        in_specs=[
            vec_spec, vec_spec,
            pl.BlockSpec((None, t, ATT_DV), lambda b_, h: (b_, 0, h)),
            pl.BlockSpec((None, t, ATT_DV), lambda b_, h: (b_, 0, ATT_HEADS + h)),
            pl.BlockSpec((None, t, ATT_DV), lambda b_, h: (b_, 0, 2 * ATT_HEADS + h)),
            pl.BlockSpec((None, tp, ATT_DV), lambda b_, h: (b_, 0, h)),
            pl.BlockSpec((None, tp, ATT_DV), lambda b_, h: (b_, 0, h)),
        ],
        out_specs=pl.BlockSpec((None, t, ATT_DV), lambda b_, h: (b_, 0, h)),
        out_shape=jax.ShapeDtypeStruct((b, t, ATT_HEADS * ATT_DV), F32),
        compiler_params=_cparams(("parallel", "parallel")),
        name="diff_attn_sample",
    )(lam, gsub, qkv, qkv, qkv, kp, vp)


HG_W = HG_HEADS * HG_DK
HG_V = HG_HEADS * HG_DV
B_PAD = 8


def _level_reference(b_sc, w):
    if w >= 4:
        pieces = []
        for p in range(CHUNK // (2 * w)):
            row = b_sc[pl.ds(B_PAD + p * 2 * w + w, 1), :]
            pieces.append(jnp.broadcast_to(row, (2 * w, HG_W)))
        return jnp.concatenate(pieces, axis=0)
    t = lax.broadcasted_iota(jnp.int32, (CHUNK, HG_W), 0)
    phase = t & (2 * w - 1)
    r = None
    for ph in range(2 * w):
        shifted = b_sc[pl.ds(B_PAD + w - ph, CHUNK), :]
        r = shifted if r is None else jnp.where(phase == ph, shifted, r)
    return r


def _hgrn_chunk(hq, hf, hi, lb, st_sc, b_sc, masks):
    tril, bd_k, bd_v, bd_s = masks
    f = lb + (1.0 - lb) * jax.nn.sigmoid(hf)
    g = jnp.log(f)
    kk = 1.0 - f
    q = _silu(hq) * (HG_DK ** -0.5)
    g0 = g.astype(BF16)
    r1 = g - g0.astype(F32)
    g1 = r1.astype(BF16)
    g2 = (r1 - g1.astype(F32)).astype(BF16)
    b = _dot(tril, g0) + _dot(tril, g1) + _dot(tril, g2)
    b_sc[pl.ds(B_PAD, CHUNK), :] = b

    t_idx = lax.broadcasted_iota(jnp.int32, (CHUNK, HG_W), 0)
    s_idx = lax.broadcasted_iota(jnp.int32, (CHUNK, HG_W), 1) & (CHUNK - 1)
    zero = jnp.zeros((CHUNK, HG_W), F32)

    def block_diag_k(x):
        return jnp.where(bd_k, jnp.concatenate([x] * HG_HEADS, axis=0), jnp.zeros((), F32)).astype(BF16)

    a = jnp.where(t_idx == s_idx, _dot_nt(q.astype(BF16), block_diag_k(kk)), zero)
    w = CHUNK // 2
    while w >= 1:
        r = _level_reference(b_sc, w)
        upper = (t_idx & w) != 0
        e = jnp.exp(jnp.where(upper, b - r, r - b))
        ql = jnp.where(upper, q * e, zero)
        kl = jnp.where(upper, zero, kk * e)
        same_pair = _shr(t_idx, 2 * w) == _shr(s_idx, 2 * w)
        a = a + jnp.where(same_pair, _dot_nt(ql.astype(BF16), block_diag_k(kl)), zero)
        w //= 2

    b_last = jnp.broadcast_to(b_sc[pl.ds(B_PAD + CHUNK - 1, 1), :], (CHUNK, HG_W))
    qb = q * jnp.exp(b)
    kd = kk * jnp.exp(b_last - b)
    v16 = hi
    vbd = jnp.where(bd_v, jnp.concatenate([hi.astype(F32)] * HG_HEADS, axis=0), jnp.zeros((), F32)).astype(BF16)
    st = st_sc[...]
    o = _dot(a.astype(BF16), vbd) + _dot_nt(qb.astype(BF16), st.astype(BF16))
    upd = _dot_tn(v16, kd.astype(BF16))
    decay = jnp.exp(b_sc[pl.ds(B_PAD + CHUNK - 1, 1), :])
    st_sc[...] = st * decay + jnp.where(bd_s, upd, jnp.zeros((), F32))
    return o


def _mix_kernel(x_ref, u_ref, hqf_ref, hi_ref, hg_ref, g_ref, ya_ref, st0_ref, pool0_ref,
                lb_ref, pw_ref, ps_ref, hgn_ref, wup_ref, wua_ref, wuh_ref, wo_ref,
                xo_ref, stn_ref, pooln_ref,
                st_sc, b_sc, ext_sc, yh_sc, *, tm, p0):
    it = pl.program_id(1)

    @pl.when(it == 0)
    def _():
        st_sc[...] = st0_ref[...]
        ext_sc[pl.ds(0, POOL_HIST), :] = pool0_ref[...]
        b_sc[...] = jnp.zeros_like(b_sc)

    u = u_ref[...]
    ext_sc[pl.ds(POOL_HIST, tm), :] = u
    row = lax.broadcasted_iota(jnp.int32, (tm, LANES), 0)
    seen = (p0 + 1 + it * tm + row).astype(F32)
    cols = []
    for gi, w in enumerate(POOL_WINDOWS):
        sl = pl.ds(gi * LANES, LANES)
        win = u[:, gi * LANES:(gi + 1) * LANES]
        for d in range(1, w):
            win = win + ext_sc[pl.ds(POOL_HIST - d, tm), sl]
        dlt = win / jnp.minimum(seen, float(w)) - u[:, gi * LANES:(gi + 1) * LANES]
        cols.append(_dot(dlt.astype(BF16), pw_ref[gi]))
    y_pool = jnp.concatenate(cols, axis=1) * ps_ref[...]
    ext_sc[pl.ds(0, POOL_HIST), :] = ext_sc[pl.ds(tm, POOL_HIST), :]

    tri_r = lax.broadcasted_iota(jnp.int32, (CHUNK, CHUNK), 0)
    tri_c = lax.broadcasted_iota(jnp.int32, (CHUNK, CHUNK), 1)
    tril = (tri_c <= tri_r).astype(BF16)
    rk = lax.broadcasted_iota(jnp.int32, (HG_W, HG_W), 0)
    ck = lax.broadcasted_iota(jnp.int32, (HG_W, HG_W), 1)
    bd_k = _shr(rk, CHUNK) == _shr(ck, HG_DK)
    rv = lax.broadcasted_iota(jnp.int32, (HG_W, HG_V), 0)
    cv = lax.broadcasted_iota(jnp.int32, (HG_W, HG_V), 1)
    bd_v = _shr(rv, CHUNK) == _shr(cv, HG_DV)
    rs = lax.broadcasted_iota(jnp.int32, (HG_V, HG_W), 0)
    cs = lax.broadcasted_iota(jnp.int32, (HG_V, HG_W), 1)
    bd_s = _shr(rs, HG_DV) == _shr(cs, HG_DK)
    masks = (tril, bd_k, bd_v, bd_s)
    lb = lb_ref[...]
    for c in range(tm // CHUNK):
        rows = pl.ds(c * CHUNK, CHUNK)
        o = _hgrn_chunk(hqf_ref[rows, pl.ds(0, HG_W)], hqf_ref[rows, pl.ds(HG_W, HG_W)],
                        hi_ref[rows, :], lb, st_sc, b_sc, masks)
        yh_sc[rows, :] = o
    oh = yh_sc[...]
    hg = hg_ref[...].astype(F32)
    heads = []
    for h in range(HG_HEADS):
        sl = slice(h * HG_DV, (h + 1) * HG_DV)
        heads.append(_rms(oh[:, sl], hgn_ref[...]) * _silu(hg[:, sl]))
    y_hg = jnp.concatenate(heads, axis=1)

    gts = g_ref[...].astype(F32)
    merged = (_sigmoid(gts[:, :D_MODEL]) * _dot(y_pool.astype(BF16), wup_ref[...])
              + _sigmoid(gts[:, D_MODEL:2 * D_MODEL]) * _dot(ya_ref[...].astype(BF16), wua_ref[...])
              + _sigmoid(gts[:, 2 * D_MODEL:]) * _dot(y_hg.astype(BF16), wuh_ref[...]))
    xo_ref[...] = x_ref[...] + _dot(merged.astype(BF16), wo_ref[...])

    @pl.when(it == pl.num_programs(1) - 1)
    def _():
        stn_ref[...] = st_sc[...]
        pooln_ref[...] = ext_sc[pl.ds(0, POOL_HIST), :]


def _mix_out(x, proj_a, proj_b, gates, y_att, st0, pool0, lb, pool_w, pool_scale, hg_outn,
             w_up_pool, w_up_att, w_up_hgrn, w_out, *, tm, p0):
    b, t, _ = x.shape
    kern = functools.partial(_mix_kernel, tm=tm, p0=p0)

    def rows(width, col):
        return pl.BlockSpec((None, tm, width), lambda b_, i: (b_, i, col))

    def const(shape):
        return pl.BlockSpec(shape, lambda b_, i: (0,) * len(shape))

    def per_batch(shape):
        return pl.BlockSpec((None,) + shape, lambda b_, i: (b_, 0, 0))

    return pl.pallas_call(
        kern,
        grid=(b, t // tm),
        in_specs=[
            rows(D_MODEL, 0),
            rows(SEG, 0),
            rows(SEG, 1),
            rows(SEG, 0),
            rows(SEG, 1),
            rows(3 * D_MODEL, 0),
            rows(SEG, 0),
            per_batch((HG_V, HG_W)),
            per_batch((POOL_HIST, SEG)),
            const((1, HG_W)), const((4, LANES, LANES)), const((1, SEG)), const((1, HG_DV)),
            const((SEG, D_MODEL)), const((SEG, D_MODEL)), const((SEG, D_MODEL)), const((D_MODEL, D_MODEL)),
        ],
        out_specs=[
            rows(D_MODEL, 0),
            per_batch((HG_V, HG_W)),
            per_batch((POOL_HIST, SEG)),
        ],
        out_shape=[
            jax.ShapeDtypeStruct((b, t, D_MODEL), F32),
            jax.ShapeDtypeStruct((b, HG_V, HG_W), F32),
            jax.ShapeDtypeStruct((b, POOL_HIST, SEG), F32),
        ],
        scratch_shapes=[
            pltpu.VMEM((HG_V, HG_W), F32),
            pltpu.VMEM((CHUNK + 2 * B_PAD, HG_W), F32),
            pltpu.VMEM((POOL_HIST + tm, SEG), F32),
            pltpu.VMEM((tm, HG_V), F32),
        ],
        compiler_params=_cparams(("parallel", "arbitrary")),
        name="mix_out",
    )(x, proj_a, proj_a, proj_b, proj_b, gates, y_att, st0, pool0,
      lb, pool_w, pool_scale, hg_outn, w_up_pool, w_up_att, w_up_hgrn, w_out)


def _state_to_block_diag(s):
    b = s.shape[0]
    st = jnp.swapaxes(s, 2, 3)
    eye = jnp.eye(HG_HEADS, dtype=s.dtype)
    return jnp.einsum('bhed,hg->bhegd', st, eye).reshape(b, HG_V, HG_W)


def _block_diag_to_state(st):
    b = st.shape[0]
    s5 = st.reshape(b, HG_HEADS, HG_DV, HG_HEADS, HG_DK)
    diag = jnp.stack([s5[:, h, :, h, :] for h in range(HG_HEADS)], axis=1)
    return jnp.swapaxes(diag, 2, 3)


def _head_rms(x, gain):
    return jnp.concatenate(
        [_rms(x[:, h * X_HD:(h + 1) * X_HD], gain) for h in range(X_HEADS)], axis=1)


def _memkv_kernel(m_ref, g_ref, w_ref, kn_ref, k_ref, v_ref, k16_ref, v16_ref):
    h = _rms(m_ref[...], g_ref[...]).astype(BF16)
    kv = _dot(h, w_ref[...])
    mk = _head_rms(kv[:, :D_MODEL], kn_ref[...])
    mv = kv[:, D_MODEL:]
    k_ref[...] = mk
    v_ref[...] = mv
    k16_ref[...] = mk.astype(BF16)
    v16_ref[...] = mv.astype(BF16)


def _memory_kv(mem, g, w_ckv, kn):
    b, n, _ = mem.shape
    blk = pl.BlockSpec((None, n, D_MODEL), lambda b_: (b_, 0, 0))
    return pl.pallas_call(
        _memkv_kernel,
        grid=(b,),
        in_specs=[
            blk,
            pl.BlockSpec((1, D_MODEL), lambda b_: (0, 0)),
            pl.BlockSpec((D_MODEL, 2 * D_MODEL), lambda b_: (0, 0)),
            pl.BlockSpec((1, X_HD), lambda b_: (0, 0)),
        ],
        out_specs=[blk, blk, blk, blk],
        out_shape=[jax.ShapeDtypeStruct((b, n, D_MODEL), F32)] * 2
        + [jax.ShapeDtypeStruct((b, n, D_MODEL), BF16)] * 2,
        compiler_params=_cparams(("parallel",)),
        name="memory_kv",
    )(mem, g.reshape(1, D_MODEL), w_ckv, kn.reshape(1, X_HD))


def _cross_kernel(x_ref, g_ref, wq_ref, qn_ref, mk_ref, mv_ref, wo_ref, o_ref):
    x = x_ref[...]
    q = _head_rms(_dot(_rms(x, g_ref[...]).astype(BF16), wq_ref[...]), qn_ref[...])
    q = (q * (X_HD ** -0.5)).astype(BF16)
    outs = []
    for h in range(X_HEADS):
        sl = slice(h * X_HD, (h + 1) * X_HD)
        s = _dot_nt(q[:, sl], mk_ref[:, sl])
        p = jnp.exp(s - jnp.max(s, axis=-1, keepdims=True))
        p = p / jnp.sum(p, axis=-1, keepdims=True)
        outs.append(_dot(p.astype(BF16), mv_ref[:, sl]))
    o = jnp.concatenate(outs, axis=1)
    o_ref[...] = x + _dot(o.astype(BF16), wo_ref[...])


def _cross_attend(x, mk, mv, g, w_cq, qn, w_co, *, tm):
    b, t, _ = x.shape
    n = mk.shape[1]
    rows = pl.BlockSpec((None, tm, D_MODEL), lambda b_, i: (b_, i, 0))
    mem = pl.BlockSpec((None, n, D_MODEL), lambda b_, i: (b_, 0, 0))
    wsq = pl.BlockSpec((D_MODEL, D_MODEL), lambda b_, i: (0, 0))
    return pl.pallas_call(
        _cross_kernel,
        grid=(b, t // tm),
        in_specs=[
            rows,
            pl.BlockSpec((1, D_MODEL), lambda b_, i: (0, 0)),
            wsq,
            pl.BlockSpec((1, X_HD), lambda b_, i: (0, 0)),
            mem, mem, wsq,
        ],
        out_specs=rows,
        out_shape=jax.ShapeDtypeStruct((b, t, D_MODEL), F32),
        compiler_params=_cparams(("parallel", "parallel")),
        name="cross_attn",
    )(x, g.reshape(1, D_MODEL), w_cq, qn.reshape(1, X_HD), mk, mv, w_co)


def _layer(x, l, depth, p0, rope, kv_rows, past_kv, pool_prev, hg_state, lb, mk16, mv16, w, *, tiles):
    b, t, _ = x.shape
    n = b * t
    x2 = _ffn_half(x.reshape(n, D_MODEL), w['norm_ffn1'], w['w_ffn1_in'], w['w_ffn1_out'],
                   tm=tiles['ffn'], tf=tiles['tf'])

    proj_a, proj_b, kf, vf, gates, qkv, *tiles_t = _in_proj(
        x2, w['norm_mix'], w['w_in'], w['group_mean'], w['att_qn'], w['att_kn'], *rope, kv_rows,
        tm=tiles['proj'], seq_len=t, tq=None if past_kv is not None else tiles['attn_bounded'],
        layer=l, depth=depth)

    lam_init = 0.8 - 0.6 * math.exp(-0.3 * l)
    lam = (jnp.exp(jnp.sum(w['lq1'] * w['lk1'])) - jnp.exp(jnp.sum(w['lq2'] * w['lk2'])) + lam_init)
    lam = jnp.full((1, ATT_DV), lam, F32)
    gsub = w['att_subln'].reshape(1, ATT_DV)
    qkv3 = qkv.reshape(b, t, 3 * SEG)
    if past_kv is None:
        bound = (ATT_DK ** 0.5) * jnp.max(jnp.abs(w['att_qn'])) * jnp.max(jnp.abs(w['att_kn']))
        y_att = lax.cond(
            bound <= SCORE_BOUND_MAX,
            lambda a, q5, v5: _attn_bounded(a, q5, v5, lam, gsub, 1.0 - lam_init),
            lambda a, q5, v5: _attn_prompt(a, lam, gsub, 1.0 - lam_init, tq=tiles['attn']),
            qkv3, *tiles_t)
    else:
        y_att = _attn_sample(qkv3, past_kv[0], past_kv[1], lam, gsub, 1.0 - lam_init)

    x3, st_new, pool_new = _mix_out(
        x2.reshape(b, t, D_MODEL), proj_a.reshape(b, t, 2 * SEG), proj_b.reshape(b, t, 2 * SEG),
        gates.reshape(b, t, N_SEG_G * SEG),
        y_att, _state_to_block_diag(hg_state), pool_prev, lb,
        w['pool_w'], w['pool_scale'], w['hg_outn'], w['w_up_pool'], w['w_up_att'], w['w_up_hgrn'], w['w_out'],
        tm=tiles['mix'], p0=p0)

    x4 = _cross_attend(x3, mk16, mv16, w['norm_cross'], w['w_cq'], w['cross_qn'], w['w_co'], tm=tiles['cross'])
    x5 = _ffn_half(x4.reshape(n, D_MODEL), w['norm_ffn2'], w['w_ffn2_in'], w['w_ffn2_out'],
                   tm=tiles['ffn'], tf=tiles['tf'])
    return (x5.reshape(b, t, D_MODEL), (kf, vf), pool_new[:, POOL_HIST - POOL_STATE:],
            _block_diag_to_state(st_new))


PROMPT_TILES = dict(ffn=1024, tf=256, proj=512, attn=256, attn_bounded=512, mix=512, cross=1024)
SAMPLE_TILES = dict(ffn=512, tf=256, proj=512, attn=64, mix=64, cross=64)


def kernel(x_prompt, x_sample, cache_attn_k, cache_attn_v, cache_mem_k, cache_mem_v, state_pool, state_hgrn, mem_prompt, norm_ffn1, w_ffn1_in, w_ffn1_out, norm_mix, w_in, pool_w, pool_scale, att_q_norm, att_k_norm, lambda_q1, lambda_k1, lambda_q2, lambda_k2, att_subln, hgrn_lower, hgrn_out_norm, w_up_pool, w_up_att, w_up_hgrn, w_out, norm_cross, norm_mem, w_cq, w_ckv, cross_q_norm, cross_k_norm, w_co, norm_ffn2, w_ffn2_in, w_ffn2_out):
    depth = w_in.shape[0]
    bp = x_prompt.shape[0]
    bs = x_sample.shape[0]
    p0_sample = cache_attn_k.shape[2]

    lp = jax.nn.softmax(hgrn_lower.astype(F32), axis=0)
    lbs = jnp.cumsum(lp, axis=0) - lp[0:1]

    gidx = jnp.arange(SEG) // ATT_DK
    group_mean = ((gidx[:, None] == gidx[None, :]).astype(F32) / ATT_DK).astype(BF16)

    def layer_weights(l):
        return dict(
            norm_ffn1=norm_ffn1[l], w_ffn1_in=w_ffn1_in[l].astype(BF16), w_ffn1_out=w_ffn1_out[l].astype(BF16),
            norm_mix=norm_mix[l], w_in=w_in[l].astype(BF16), group_mean=group_mean,
            att_qn=jnp.tile(att_q_norm[l], SEG // ATT_DK).reshape(1, SEG),
            att_kn=jnp.tile(att_k_norm[l], SEG // ATT_DK).reshape(1, SEG),
            lq1=lambda_q1[l].astype(F32), lk1=lambda_k1[l].astype(F32),
            lq2=lambda_q2[l].astype(F32), lk2=lambda_k2[l].astype(F32),
            att_subln=att_subln[l],
            pool_w=pool_w[l].astype(BF16), pool_scale=pool_scale[l].reshape(1, SEG),
            hg_outn=hgrn_out_norm[l].reshape(1, HG_DV),
            w_up_pool=w_up_pool[l].astype(BF16), w_up_att=w_up_att[l].astype(BF16),
            w_up_hgrn=w_up_hgrn[l].astype(BF16), w_out=w_out[l].astype(BF16),
            norm_cross=norm_cross[l], w_cq=w_cq[l].astype(BF16), cross_qn=cross_q_norm[l],
            w_co=w_co[l].astype(BF16),
            norm_ffn2=norm_ffn2[l], w_ffn2_in=w_ffn2_in[l].astype(BF16), w_ffn2_out=w_ffn2_out[l].astype(BF16),
        )

    weights = [layer_weights(l) for l in range(depth)]

    def rope_for(p0, t, tm):
        tabs = _rope_tables(p0, t)
        if t < tm:
            tabs = tuple(jnp.concatenate([a] * (tm // t), axis=0) for a in tabs)
        return tabs

    rope_prompt = rope_for(0, x_prompt.shape[1], PROMPT_TILES['proj'])
    rope_sample = rope_for(p0_sample, x_sample.shape[1], SAMPLE_TILES['proj'])

    y = x_prompt
    pkv = None
    pmk, pmv, ppool, phg = [], [], [], []
    for l in range(depth):
        mk, mv, mk16, mv16 = _memory_kv(mem_prompt, norm_mem[l], w_ckv[l].astype(BF16), cross_k_norm[l])
        pool0 = jnp.zeros((bp, POOL_HIST, SEG), F32)
        hg0 = jnp.zeros((bp, HG_HEADS, HG_DK, HG_DV), F32)
        y, pkv, pn, hn = _layer(y, l, depth, 0, rope_prompt, pkv, None, pool0, hg0, lbs[l].reshape(1, HG_W),
                                mk16, mv16, weights[l], tiles=PROMPT_TILES)
        pmk.append(mk.reshape(bp, -1, X_HEADS, X_HD)); pmv.append(mv.reshape(bp, -1, X_HEADS, X_HD))
        ppool.append(pn); phg.append(hn)
    y_prompt = y
    kv_shape = (depth, bp, x_prompt.shape[1], ATT_HEADS, ATT_DV)
    pk, pv = pkv[0].reshape(kv_shape), pkv[1].reshape(kv_shape)

    y = x_sample
    skv = None
    spool, shg = [], []
    for l in range(depth):
        past = (cache_attn_k[l].reshape(bs, p0_sample, ATT_HEADS * 2 * ATT_DK).astype(BF16),
                cache_attn_v[l].reshape(bs, p0_sample, ATT_HEADS * ATT_DV).astype(BF16))
        pool0 = jnp.pad(state_pool[l], ((0, 0), (POOL_HIST - POOL_STATE, 0), (0, 0)))
        mk16 = cache_mem_k[l].reshape(bs, -1, D_MODEL).astype(BF16)
        mv16 = cache_mem_v[l].reshape(bs, -1, D_MODEL).astype(BF16)
        y, skv, pn, hn = _layer(y, l, depth, p0_sample, rope_sample, skv, past, pool0, state_hgrn[l],
                                lbs[l].reshape(1, HG_W), mk16, mv16, weights[l], tiles=SAMPLE_TILES)
        spool.append(pn); shg.append(hn)
    y_sample = y
    kv_shape = (depth, bs, x_sample.shape[1], ATT_HEADS, ATT_DV)
    sk, sv = skv[0].reshape(kv_shape), skv[1].reshape(kv_shape)

    return (y_prompt, y_sample,
            pk, pv, jnp.stack(pmk), jnp.stack(pmv),
            jnp.stack(ppool), jnp.stack(phg),
            sk, sv, jnp.stack(spool), jnp.stack(shg))
```

```python
import functools
import math

import jax
import jax.numpy as jnp
from jax import lax
from jax.experimental import pallas as pl
from jax.experimental.pallas import tpu as pltpu

F32 = jnp.float32
BF16 = jnp.bfloat16

D_MODEL = 1024
CHUNK = 64
EPS = 1e-6
MASK_VALUE = -1e30
POOL_WINDOWS = (2, 4, 8, 16)
POOL_STATE = 15
POOL_HIST = 16
ATT_HEADS = 4
ATT_DK = 64
ATT_DV = 128
ROT_DIMS = 16
ROPE_THETA = 500000.0
HG_HEADS = 4
HG_DK = 64
HG_DV = 128
X_HEADS = 4
X_HD = 256
D_FF = 2816
SEG = 512
N_SEG_A = 7
N_SEG_G = 6
LANES = 128
VMEM_LIMIT = 56 * 1024 * 1024
SCORE_BOUND_MAX = 20.0
KEY_TILES_PER_ITER = 8
KEY_TILES_TAIL = 4


def _cparams(sem):
    return pltpu.CompilerParams(dimension_semantics=sem, vmem_limit_bytes=VMEM_LIMIT)


def _rms(x, g):
    ms = jnp.mean(x * x, axis=-1, keepdims=True)
    return x * lax.rsqrt(ms + EPS) * g


def _sigmoid(x):
    return 0.5 * jnp.tanh(0.5 * x) + 0.5


def _silu(x):
    return x * _sigmoid(x)


def _dot(a, b):
    return jnp.dot(a, b, preferred_element_type=F32)


def _dot_nt(a, b):
    return lax.dot_general(a, b, (((1,), (1,)), ((), ())), preferred_element_type=F32)


def _dot_tn(a, b):
    return lax.dot_general(a, b, (((0,), (0,)), ((), ())), preferred_element_type=F32)


def _shr(x, pow2):
    return lax.shift_right_logical(x, jnp.int32(int(math.log2(pow2))))


def _ffn_kernel(x_ref, g_ref, wi_ref, wo_ref, o_ref, *, tf):
    x = x_ref[...]
    h = _rms(x, g_ref[...]).astype(BF16)
    acc = None
    for c in range(D_FF // tf):
        a = _dot(h, wi_ref[:, pl.ds(c * tf, tf)])
        b = _dot(h, wi_ref[:, pl.ds(D_FF + c * tf, tf)])
        part = _dot((_silu(a) * b).astype(BF16), wo_ref[pl.ds(c * tf, tf), :])
        acc = part if acc is None else acc + part
    o_ref[...] = x + 0.5 * acc


def _resident(shape, layer=None):
    if layer is None:
        return pl.BlockSpec(shape, lambda *_: (0,) * len(shape), pipeline_mode=pl.Buffered(1))
    return pl.BlockSpec((None,) + tuple(shape), lambda *_: (layer,) + (0,) * len(shape),
                        pipeline_mode=pl.Buffered(1))


def _ffn_half(x, g, w_i, w_o, *, layer, tm, tf):
    n = x.shape[0]
    return pl.pallas_call(
        functools.partial(_ffn_kernel, tf=tf),
        grid=(n // tm,),
        in_specs=[
            pl.BlockSpec((tm, D_MODEL), lambda i: (i, 0)),
            _resident((1, D_MODEL), layer),
            _resident((D_MODEL, 2 * D_FF), layer),
            _resident((D_FF, D_MODEL), layer),
        ],
        out_specs=pl.BlockSpec((tm, D_MODEL), lambda i: (i, 0)),
        out_shape=jax.ShapeDtypeStruct((n, D_MODEL), F32),
        compiler_params=_cparams(("parallel",)),
        name="ffn_half",
    )(x, g, w_i, w_o)


def _inproj_kernel(x_ref, g_ref, w_ref, gm_ref, qn_ref, kn_ref, cos_ref, sa_ref, sb_ref, *rest, tq, chained):
    rest = rest[2:] if chained else rest
    pa_ref, pb_ref, kf_ref, vf_ref, pg_ref, qkv_ref = rest[:6]
    qt_ref, vt_ref = rest[6:8] if tq is not None else (None, None)

    def store_head_rows(ref, y):
        for h in range(ATT_HEADS):
            ref[pl.ds(h, y.shape[0], stride=ATT_HEADS), :] = y[:, h * ATT_DV:(h + 1) * ATT_DV]

    def store_transposed(ref, y):
        if ref is None:
            return
        yt = y.T.astype(BF16)
        for h in range(ATT_HEADS):
            for c in range(y.shape[0] // tq):
                ref[h, c] = yt[h * ATT_DV:(h + 1) * ATT_DV, c * tq:(c + 1) * tq]

    h = _rms(x_ref[...], g_ref[...]).astype(BF16)

    def proj(seg):
        return _dot(h, w_ref[:, pl.ds(seg * SEG, SEG)])

    rep = SEG // LANES
    cos = jnp.concatenate([cos_ref[...]] * rep, axis=1)
    sa = jnp.concatenate([sa_ref[...]] * rep, axis=1)
    sb = jnp.concatenate([sb_ref[...]] * rep, axis=1)

    def norm_rope(y, gain):
        ms = _dot((y * y).astype(BF16), gm_ref[...])
        yn = y * lax.rsqrt(ms + EPS) * gain
        half = ROT_DIMS // 2
        return yn * cos + pltpu.roll(yn, half, 1) * sa + pltpu.roll(yn, SEG - half, 1) * sb

    pa_ref[:, pl.ds(0, SEG)] = proj(0)

    q = norm_rope(proj(1), qn_ref[...]) * (ATT_DK ** -0.5)
    qkv_ref[:, pl.ds(0, SEG)] = q.astype(BF16)
    store_transposed(qt_ref, q)

    k = norm_rope(proj(2), kn_ref[...])
    store_head_rows(kf_ref, k)
    qkv_ref[:, pl.ds(SEG, SEG)] = k.astype(BF16)

    v = proj(3)
    store_head_rows(vf_ref, v)
    qkv_ref[:, pl.ds(2 * SEG, SEG)] = v.astype(BF16)
    store_transposed(vt_ref, v)

    pa_ref[:, pl.ds(SEG, SEG)] = proj(4)
    pb_ref[:, pl.ds(0, SEG)] = proj(5).astype(BF16)
    pb_ref[:, pl.ds(SEG, SEG)] = proj(6).astype(BF16)
    for s in range(N_SEG_G):
        pg_ref[:, pl.ds(s * SEG, SEG)] = proj(N_SEG_A + s).astype(BF16)


def _in_proj(x, g, w, gm, qn, kn, cos, sa, sb, kv_rows, *, tm, seq_len, tq, layer, depth):
    n = x.shape[0]
    tab_blocks = cos.shape[0] // tm
    nseg = N_SEG_A + N_SEG_G
    tab_spec = pl.BlockSpec((tm, LANES), lambda i: (i % tab_blocks, 0))
    kv_spec = pl.BlockSpec((tm * ATT_HEADS, ATT_DV), lambda i: (layer * (n // tm) + i, 0))
    kv_shape = jax.ShapeDtypeStruct((depth * n * ATT_HEADS, ATT_DV), F32)

    def rows(width):
        return pl.BlockSpec((tm, width), lambda i: (i, 0))

    out_specs = [rows(2 * SEG), rows(2 * SEG), kv_spec, kv_spec, rows(N_SEG_G * SEG), rows(3 * SEG)]
    out_shape = [
        jax.ShapeDtypeStruct((n, 2 * SEG), F32),
        jax.ShapeDtypeStruct((n, 2 * SEG), BF16),
        kv_shape, kv_shape,
        jax.ShapeDtypeStruct((n, N_SEG_G * SEG), BF16),
        jax.ShapeDtypeStruct((n, 3 * SEG), BF16),
    ]
    chained = kv_rows is not None
    n_in = 9
    any_spec = pl.BlockSpec(memory_space=pl.ANY)
    extra_in = list(kv_rows) if chained else []
    aliases = {n_in: 2, n_in + 1: 3} if chained else {}
    if tq is not None:
        per_seq = seq_len // tm
        t_spec = pl.BlockSpec((None, ATT_HEADS, tm // tq, ATT_DV, tq),
                              lambda i: (i // per_seq, 0, i % per_seq, 0, 0))
        t_shape = jax.ShapeDtypeStruct((n // seq_len, ATT_HEADS, seq_len // tq, ATT_DV, tq), BF16)
        out_specs += [t_spec, t_spec]
        out_shape += [t_shape, t_shape]
    return pl.pallas_call(
        functools.partial(_inproj_kernel, tq=tq, chained=chained),
        grid=(n // tm,),
        in_specs=[
            rows(D_MODEL),
            _resident((1, D_MODEL), layer),
            _resident((D_MODEL, nseg * SEG), layer),
            _resident((SEG, SEG)),
            _resident((1, SEG), layer), _resident((1, SEG), layer), tab_spec, tab_spec, tab_spec,
        ] + [any_spec] * len(extra_in),
        out_specs=out_specs,
        out_shape=out_shape,
        input_output_aliases=aliases,
        compiler_params=_cparams(("parallel",)),
        name="in_proj",
    )(x, g, w, gm, qn, kn, cos, sa, sb, *extra_in)


def _rope_tables(p0, t):
    half = ROT_DIMS // 2
    inv = ROPE_THETA ** (-jnp.arange(half, dtype=F32) / half)
    in_head = jnp.arange(LANES) % ATT_DK
    pos = p0 + jnp.arange(t, dtype=jnp.int32)
    ang = pos.astype(F32)[:, None] * inv[in_head % half][None, :]
    cos, sin = jnp.cos(ang), jnp.sin(ang)
    lower = (in_head < half)[None, :]
    upper = jnp.logical_and(in_head >= half, in_head < ROT_DIMS)[None, :]
    return (jnp.where(in_head[None, :] < ROT_DIMS, cos, 1.0),
            jnp.where(upper, sin, 0.0),
            jnp.where(lower, -sin, 0.0))


def _stack_maps(q):
    lane = lax.broadcasted_iota(jnp.int32, q.shape, 1)
    zero = jnp.zeros_like(q)
    return jnp.concatenate([jnp.where(lane < ATT_DK, q, zero), jnp.where(lane >= ATT_DK, q, zero)], axis=0)


def _attn_finish(acc, l, lam, gsub, post_scale, tq):
    o = acc[:tq] / l[:tq] - lam * (acc[tq:] / l[tq:])
    return _rms(o, gsub) * post_scale


def _attn_prompt_kernel(lam_ref, gsub_ref, q_ref, k_ref, v_ref, o_ref, acc_sc, m_sc, l_sc, *, tq, post_scale):
    i = pl.program_id(2)
    qbd = _stack_maps(q_ref[...])
    m_sc[...] = jnp.full_like(m_sc, MASK_VALUE)
    l_sc[...] = jnp.zeros_like(l_sc)
    acc_sc[...] = jnp.zeros_like(acc_sc)

    def step(j, masked):
        off = pl.multiple_of(j * tq, tq)
        kj = k_ref[pl.ds(off, tq), :]
        vj = v_ref[pl.ds(off, tq), :]
        s = _dot_nt(qbd, kj)
        if masked:
            qi = lax.broadcasted_iota(jnp.int32, s.shape, 0) & (tq - 1)
            ki = lax.broadcasted_iota(jnp.int32, s.shape, 1)
            s = jnp.where(_shr(ki, CHUNK) <= _shr(qi, CHUNK), s, MASK_VALUE)
        m_prev = m_sc[...]
        m_new = jnp.maximum(m_prev, jnp.max(s, axis=-1, keepdims=True))
        alpha = jnp.exp(m_prev - m_new)
        p = jnp.exp(s - m_new)
        l_sc[...] = alpha * l_sc[...] + jnp.sum(p, axis=-1, keepdims=True)
        acc_sc[...] = alpha * acc_sc[...] + _dot(p.astype(BF16), vj)
        m_sc[...] = m_new

    def body(j, carry):
        step(j, False)
        return carry

    lax.fori_loop(0, i, body, 0)
    step(i, True)
    o_ref[...] = _attn_finish(acc_sc[...], l_sc[...], lam_ref[...], gsub_ref[...], post_scale, tq)


def _attn_prompt(qkv, lam, gsub, post_scale, *, tq):
    b, t, _ = qkv.shape
    kern = functools.partial(_attn_prompt_kernel, tq=tq, post_scale=post_scale)
    vec_spec = pl.BlockSpec((1, ATT_DV), lambda b_, h, i: (0, 0))
    return pl.pallas_call(
        kern,
        grid=(b, ATT_HEADS, t // tq),
        in_specs=[
            vec_spec, vec_spec,
            pl.BlockSpec((None, tq, ATT_DV), lambda b_, h, i: (b_, i, h)),
            pl.BlockSpec((None, t, ATT_DV), lambda b_, h, i: (b_, 0, ATT_HEADS + h)),
            pl.BlockSpec((None, t, ATT_DV), lambda b_, h, i: (b_, 0, 2 * ATT_HEADS + h)),
        ],
        out_specs=pl.BlockSpec((None, tq, ATT_DV), lambda b_, h, i: (b_, i, h)),
        out_shape=jax.ShapeDtypeStruct((b, t, ATT_HEADS * ATT_DV), F32),
        scratch_shapes=[pltpu.VMEM((2 * tq, ATT_DV), F32), pltpu.VMEM((2 * tq, 1), F32),
                        pltpu.VMEM((2 * tq, 1), F32)],
        compiler_params=_cparams(("parallel", "parallel", "arbitrary")),
        name="diff_attn_prompt",
    )(lam, gsub, qkv, qkv, qkv)


def _attn_bounded_kernel(lam_ref, gsub_ref, qt_ref, k_ref, vt_ref, o_ref, acc_sc, l_sc, *, tq, post_scale):
    i = pl.program_id(2)
    qt = qt_ref[...].astype(F32)
    row = lax.broadcasted_iota(jnp.int32, qt.shape, 0)
    qbd = jnp.concatenate([jnp.where(row < ATT_DK, qt, 0.0), jnp.where(row >= ATT_DK, qt, 0.0)],
                          axis=1).astype(BF16)
    acc_sc[...] = jnp.zeros_like(acc_sc)
    l_sc[...] = jnp.zeros_like(l_sc)

    def probs(j, masked):
        off = pl.multiple_of(j * tq, tq)
        s = _dot(k_ref[pl.ds(off, tq), :], qbd)
        if masked:
            ki = lax.broadcasted_iota(jnp.int32, s.shape, 0)
            qi = lax.broadcasted_iota(jnp.int32, s.shape, 1) & (tq - 1)
            s = jnp.where(_shr(ki, CHUNK) <= _shr(qi, CHUNK), s, MASK_VALUE)
        p = jnp.exp(s)
        return jnp.sum(p.reshape(tq // 8, 8, 2 * tq), axis=0), _dot(vt_ref[j], p.astype(BF16))

    def accumulate(first, count, diagonal_last=False):
        parts = [probs(first + c, diagonal_last and c == count - 1) for c in range(count)]
        l_sc[...] += functools.reduce(lambda x, y: x + y, [p_[0] for p_ in parts])
        acc_sc[...] += functools.reduce(lambda x, y: x + y, [p_[1] for p_ in parts])

    def group(jj, carry):
        accumulate(KEY_TILES_PER_ITER * jj, KEY_TILES_PER_ITER)
        return carry

    lax.fori_loop(0, i // KEY_TILES_PER_ITER, group, 0)
    left = i % KEY_TILES_PER_ITER

    @pl.when(left >= KEY_TILES_TAIL)
    def _():
        accumulate(i - left, KEY_TILES_TAIL)

    rem = left % KEY_TILES_TAIL
    for r in range(KEY_TILES_TAIL):
        @pl.when(rem == r)
        def _(r=r):
            accumulate(i - r, r + 1, diagonal_last=True)
    l = jnp.sum(l_sc[...], axis=0, keepdims=True)
    acc = acc_sc[...]
    ot = acc[:, :tq] / l[:, :tq] - lam_ref[...] * (acc[:, tq:] / l[:, tq:])
    ms = jnp.mean(ot * ot, axis=0, keepdims=True)
    yt = ot * lax.rsqrt(ms + EPS) * (gsub_ref[...] * post_scale)
    o_ref[...] = yt.T


def _attn_bounded(qkv, q5, v5, lam, gsub, post_scale):
    b, t, _ = qkv.shape
    nt, tq = q5.shape[2], q5.shape[4]
    kern = functools.partial(_attn_bounded_kernel, tq=tq, post_scale=post_scale)
    return pl.pallas_call(
        kern,
        grid=(b, ATT_HEADS, nt),
        in_specs=[
            pl.BlockSpec((1, 1), lambda b_, h, i: (0, 0)),
            pl.BlockSpec((ATT_DV, 1), lambda b_, h, i: (0, 0)),
            pl.BlockSpec((None, None, None, ATT_DV, tq), lambda b_, h, i: (b_, h, i, 0, 0)),
            pl.BlockSpec((None, t, ATT_DV), lambda b_, h, i: (b_, 0, ATT_HEADS + h)),
            pl.BlockSpec((None, None, nt, ATT_DV, tq), lambda b_, h, i: (b_, h, 0, 0, 0)),
        ],
        out_specs=pl.BlockSpec((None, tq, ATT_DV), lambda b_, h, i: (b_, i, h)),
        out_shape=jax.ShapeDtypeStruct((b, t, ATT_HEADS * ATT_DV), F32),
        scratch_shapes=[pltpu.VMEM((ATT_DV, 2 * tq), F32), pltpu.VMEM((8, 2 * tq), F32)],
        compiler_params=_cparams(("parallel", "parallel", "arbitrary")),
        name="diff_attn_bounded",
    )(lam[:, :1], gsub.reshape(ATT_DV, 1), q5, qkv, v5)


def _attn_sample_kernel(lam_ref, gsub_ref, q_ref, kn_ref, vn_ref, kp_ref, vp_ref, o_ref, *, tq, post_scale):
    qbd = _stack_maps(q_ref[...])
    sp = _dot_nt(qbd, kp_ref[...])
    sn = _dot_nt(qbd, kn_ref[...])
    m = jnp.maximum(jnp.max(sp, axis=-1, keepdims=True), jnp.max(sn, axis=-1, keepdims=True))
    pp = jnp.exp(sp - m)
    pn = jnp.exp(sn - m)
    l = jnp.sum(pp, axis=-1, keepdims=True) + jnp.sum(pn, axis=-1, keepdims=True)
    acc = _dot(pp.astype(BF16), vp_ref[...]) + _dot(pn.astype(BF16), vn_ref[...])
    o_ref[...] = _attn_finish(acc, l, lam_ref[...], gsub_ref[...], post_scale, tq)


def _attn_sample(qkv, kp, vp, lam, gsub, post_scale):
    b, t, _ = qkv.shape
    tp = kp.shape[1]
    kern = functools.partial(_attn_sample_kernel, tq=t, post_scale=post_scale)
    vec_spec = pl.BlockSpec((1, ATT_DV), lambda b_, h: (0, 0))
    return pl.pallas_call(
        kern,
        grid=(b, ATT_HEADS),
        in_specs=[
            vec_spec, vec_spec,
            pl.BlockSpec((None, t, ATT_DV), lambda b_, h: (b_, 0, h)),
            pl.BlockSpec((None, t, ATT_DV), lambda b_, h: (b_, 0, ATT_HEADS + h)),
            pl.BlockSpec((None, t, ATT_DV), lambda b_, h: (b_, 0, 2 * ATT_HEADS + h)),
            pl.BlockSpec((None, tp, ATT_DV), lambda b_, h: (b_, 0, h)),
            pl.BlockSpec((None, tp, ATT_DV), lambda b_, h: (b_, 0, h)),
        ],
        out_specs=pl.BlockSpec((None, t, ATT_DV), lambda b_, h: (b_, 0, h)),
        out_shape=jax.ShapeDtypeStruct((b, t, ATT_HEADS * ATT_DV), F32),
        compiler_params=_cparams(("parallel", "parallel")),
        name="diff_attn_sample",
    )(lam, gsub, qkv, qkv, qkv, kp, vp)


HG_W = HG_HEADS * HG_DK
HG_V = HG_HEADS * HG_DV
B_PAD = 8


def _level_reference(b_sc, w):
    if w >= 4:
        pieces = []
        for p in range(CHUNK // (2 * w)):
            row = b_sc[pl.ds(B_PAD + p * 2 * w + w, 1), :]
            pieces.append(jnp.broadcast_to(row, (2 * w, HG_W)))
        return jnp.concatenate(pieces, axis=0)
    t = lax.broadcasted_iota(jnp.int32, (CHUNK, HG_W), 0)
    phase = t & (2 * w - 1)
    r = None
    for ph in range(2 * w):
        shifted = b_sc[pl.ds(B_PAD + w - ph, CHUNK), :]
        r = shifted if r is None else jnp.where(phase == ph, shifted, r)
    return r


def _hgrn_chunk(hq, hf, hi, lb, st_sc, b_sc, masks):
    tril, bd_k, bd_v, bd_s = masks
    f = lb + (1.0 - lb) * jax.nn.sigmoid(hf)
    g = jnp.log(f)
    kk = 1.0 - f
    q = _silu(hq) * (HG_DK ** -0.5)
    g0 = g.astype(BF16)
    r1 = g - g0.astype(F32)
    g1 = r1.astype(BF16)
    g2 = (r1 - g1.astype(F32)).astype(BF16)
    b = _dot(tril, g0) + _dot(tril, g1) + _dot(tril, g2)
    b_sc[pl.ds(B_PAD, CHUNK), :] = b

    t_idx = lax.broadcasted_iota(jnp.int32, (CHUNK, HG_W), 0)
    s_idx = lax.broadcasted_iota(jnp.int32, (CHUNK, HG_W), 1) & (CHUNK - 1)
    zero = jnp.zeros((CHUNK, HG_W), F32)

    def block_diag_k(x):
        return jnp.where(bd_k, jnp.concatenate([x] * HG_HEADS, axis=0), jnp.zeros((), F32)).astype(BF16)

    a = jnp.where(t_idx == s_idx, _dot_nt(q.astype(BF16), block_diag_k(kk)), zero)
    w = CHUNK // 2
    while w >= 1:
        r = _level_reference(b_sc, w)
        upper = (t_idx & w) != 0
        e = jnp.exp(jnp.where(upper, b - r, r - b))
        ql = jnp.where(upper, q * e, zero)
        kl = jnp.where(upper, zero, kk * e)
        same_pair = _shr(t_idx, 2 * w) == _shr(s_idx, 2 * w)
        a = a + jnp.where(same_pair, _dot_nt(ql.astype(BF16), block_diag_k(kl)), zero)
        w //= 2

    b_last = jnp.broadcast_to(b_sc[pl.ds(B_PAD + CHUNK - 1, 1), :], (CHUNK, HG_W))
    qb = q * jnp.exp(b)
    kd = kk * jnp.exp(b_last - b)
    v16 = hi
    vbd = jnp.where(bd_v, jnp.concatenate([hi.astype(F32)] * HG_HEADS, axis=0), jnp.zeros((), F32)).astype(BF16)
    st = st_sc[...]
    o = _dot(a.astype(BF16), vbd) + _dot_nt(qb.astype(BF16), st.astype(BF16))
    upd = _dot_tn(v16, kd.astype(BF16))
    decay = jnp.exp(b_sc[pl.ds(B_PAD + CHUNK - 1, 1), :])
    st_sc[...] = st * decay + jnp.where(bd_s, upd, jnp.zeros((), F32))
    return o


def _mix_kernel(x_ref, u_ref, hqf_ref, hi_ref, hg_ref, g_ref, ya_ref, st0_ref, pool0_ref,
                lb_ref, pw_ref, ps_ref, hgn_ref, wup_ref, wua_ref, wuh_ref, wo_ref,
                xo_ref, stn_ref, pooln_ref,
                st_sc, b_sc, ext_sc, yh_sc, *, tm, p0):
    it = pl.program_id(1)

    @pl.when(it == 0)
    def _():
        st_sc[...] = st0_ref[...]
        ext_sc[pl.ds(0, POOL_HIST), :] = pool0_ref[...]
        b_sc[...] = jnp.zeros_like(b_sc)

    u = u_ref[...]
    ext_sc[pl.ds(POOL_HIST, tm), :] = u
    row = lax.broadcasted_iota(jnp.int32, (tm, LANES), 0)
    seen = (p0 + 1 + it * tm + row).astype(F32)
    cols = []
    for gi, w in enumerate(POOL_WINDOWS):
        sl = pl.ds(gi * LANES, LANES)
        win = u[:, gi * LANES:(gi + 1) * LANES]
        for d in range(1, w):
            win = win + ext_sc[pl.ds(POOL_HIST - d, tm), sl]
        dlt = win / jnp.minimum(seen, float(w)) - u[:, gi * LANES:(gi + 1) * LANES]
        cols.append(_dot(dlt.astype(BF16), pw_ref[gi]))
    y_pool = jnp.concatenate(cols, axis=1) * ps_ref[...]
    ext_sc[pl.ds(0, POOL_HIST), :] = ext_sc[pl.ds(tm, POOL_HIST), :]

    tri_r = lax.broadcasted_iota(jnp.int32, (CHUNK, CHUNK), 0)
    tri_c = lax.broadcasted_iota(jnp.int32, (CHUNK, CHUNK), 1)
    tril = (tri_c <= tri_r).astype(BF16)
    rk = lax.broadcasted_iota(jnp.int32, (HG_W, HG_W), 0)
    ck = lax.broadcasted_iota(jnp.int32, (HG_W, HG_W), 1)
    bd_k = _shr(rk, CHUNK) == _shr(ck, HG_DK)
    rv = lax.broadcasted_iota(jnp.int32, (HG_W, HG_V), 0)
    cv = lax.broadcasted_iota(jnp.int32, (HG_W, HG_V), 1)
    bd_v = _shr(rv, CHUNK) == _shr(cv, HG_DV)
    rs = lax.broadcasted_iota(jnp.int32, (HG_V, HG_W), 0)
    cs = lax.broadcasted_iota(jnp.int32, (HG_V, HG_W), 1)
    bd_s = _shr(rs, HG_DV) == _shr(cs, HG_DK)
    masks = (tril, bd_k, bd_v, bd_s)
    lb = lb_ref[...]
    for c in range(tm // CHUNK):
        rows = pl.ds(c * CHUNK, CHUNK)
        o = _hgrn_chunk(hqf_ref[rows, pl.ds(0, HG_W)], hqf_ref[rows, pl.ds(HG_W, HG_W)],
                        hi_ref[rows, :], lb, st_sc, b_sc, masks)
        yh_sc[rows, :] = o
    oh = yh_sc[...]
    hg = hg_ref[...].astype(F32)
    heads = []
    for h in range(HG_HEADS):
        sl = slice(h * HG_DV, (h + 1) * HG_DV)
        heads.append(_rms(oh[:, sl], hgn_ref[...]) * _silu(hg[:, sl]))
    y_hg = jnp.concatenate(heads, axis=1)

    gts = g_ref[...].astype(F32)
    merged = (_sigmoid(gts[:, :D_MODEL]) * _dot(y_pool.astype(BF16), wup_ref[...])
              + _sigmoid(gts[:, D_MODEL:2 * D_MODEL]) * _dot(ya_ref[...].astype(BF16), wua_ref[...])
              + _sigmoid(gts[:, 2 * D_MODEL:]) * _dot(y_hg.astype(BF16), wuh_ref[...]))
    xo_ref[...] = x_ref[...] + _dot(merged.astype(BF16), wo_ref[...])

    @pl.when(it == pl.num_programs(1) - 1)
    def _():
        stn_ref[...] = st_sc[...]
        pooln_ref[...] = ext_sc[pl.ds(0, POOL_HIST), :]


def _mix_out(x, proj_a, proj_b, gates, y_att, st0, pool0, lb, pool_w, pool_scale, hg_outn,
             w_up_pool, w_up_att, w_up_hgrn, w_out, *, layer, tm, p0):
    b, t, _ = x.shape
    kern = functools.partial(_mix_kernel, tm=tm, p0=p0)

    def rows(width, col):
        return pl.BlockSpec((None, tm, width), lambda b_, i: (b_, i, col))

    def const(shape):
        return _resident(shape, layer)

    def per_batch(shape):
        return pl.BlockSpec((None,) + shape, lambda b_, i: (b_, 0, 0))

    return pl.pallas_call(
        kern,
        grid=(b, t // tm),
        in_specs=[
            rows(D_MODEL, 0),
            rows(SEG, 0),
            rows(SEG, 1),
            rows(SEG, 0),
            rows(SEG, 1),
            rows(3 * D_MODEL, 0),
            rows(SEG, 0),
            per_batch((HG_V, HG_W)),
            per_batch((POOL_HIST, SEG)),
            const((1, HG_W)), const((4, LANES, LANES)), const((1, SEG)), const((1, HG_DV)),
            const((SEG, D_MODEL)), const((SEG, D_MODEL)), const((SEG, D_MODEL)), const((D_MODEL, D_MODEL)),
        ],
        out_specs=[
            rows(D_MODEL, 0),
            per_batch((HG_V, HG_W)),
            per_batch((POOL_HIST, SEG)),
        ],
        out_shape=[
            jax.ShapeDtypeStruct((b, t, D_MODEL), F32),
            jax.ShapeDtypeStruct((b, HG_V, HG_W), F32),
            jax.ShapeDtypeStruct((b, POOL_HIST, SEG), F32),
        ],
        scratch_shapes=[
            pltpu.VMEM((HG_V, HG_W), F32),
            pltpu.VMEM((CHUNK + 2 * B_PAD, HG_W), F32),
            pltpu.VMEM((POOL_HIST + tm, SEG), F32),
            pltpu.VMEM((tm, HG_V), F32),
        ],
        compiler_params=_cparams(("parallel", "arbitrary")),
        name="mix_out",
    )(x, proj_a, proj_a, proj_b, proj_b, gates, y_att, st0, pool0,
      lb, pool_w, pool_scale, hg_outn, w_up_pool, w_up_att, w_up_hgrn, w_out)


def _state_to_block_diag(s):
    b = s.shape[0]
    st = jnp.swapaxes(s, 2, 3)
    eye = jnp.eye(HG_HEADS, dtype=s.dtype)
    return jnp.einsum('bhed,hg->bhegd', st, eye).reshape(b, HG_V, HG_W)


def _block_diag_to_state(st):
    b = st.shape[0]
    s5 = st.reshape(b, HG_HEADS, HG_DV, HG_HEADS, HG_DK)
    diag = jnp.stack([s5[:, h, :, h, :] for h in range(HG_HEADS)], axis=1)
    return jnp.swapaxes(diag, 2, 3)


def _head_rms(x, gain):
    return jnp.concatenate(
        [_rms(x[:, h * X_HD:(h + 1) * X_HD], gain) for h in range(X_HEADS)], axis=1)


def _memkv_kernel(m_ref, g_ref, w_ref, kn_ref, k_ref, v_ref, k16_ref, v16_ref):
    h = _rms(m_ref[...], g_ref[...]).astype(BF16)
    kv = _dot(h, w_ref[...])
    mk = _head_rms(kv[:, :D_MODEL], kn_ref[...])
    mv = kv[:, D_MODEL:]
    k_ref[...] = mk
    v_ref[...] = mv
    k16_ref[...] = mk.astype(BF16)
    v16_ref[...] = mv.astype(BF16)


def _memory_kv(mem, g, w_ckv, kn, *, layer):
    b, n, _ = mem.shape
    blk = pl.BlockSpec((None, n, D_MODEL), lambda b_: (b_, 0, 0))
    return pl.pallas_call(
        _memkv_kernel,
        grid=(b,),
        in_specs=[
            blk,
            _resident((1, D_MODEL), layer),
            _resident((D_MODEL, 2 * D_MODEL), layer),
            _resident((1, X_HD), layer),
        ],
        out_specs=[blk, blk, blk, blk],
        out_shape=[jax.ShapeDtypeStruct((b, n, D_MODEL), F32)] * 2
        + [jax.ShapeDtypeStruct((b, n, D_MODEL), BF16)] * 2,
        compiler_params=_cparams(("parallel",)),
        name="memory_kv",
    )(mem, g, w_ckv, kn)


def _cross_kernel(x_ref, g_ref, wq_ref, qn_ref, mk_ref, mv_ref, wo_ref, o_ref):
    x = x_ref[...]
    q = _head_rms(_dot(_rms(x, g_ref[...]).astype(BF16), wq_ref[...]), qn_ref[...])
    q = (q * (X_HD ** -0.5)).astype(BF16)
    outs = []
    for h in range(X_HEADS):
        sl = slice(h * X_HD, (h + 1) * X_HD)
        s = _dot_nt(q[:, sl], mk_ref[:, sl])
        p = jnp.exp(s - jnp.max(s, axis=-1, keepdims=True))
        p = p / jnp.sum(p, axis=-1, keepdims=True)
        outs.append(_dot(p.astype(BF16), mv_ref[:, sl]))
    o = jnp.concatenate(outs, axis=1)
    o_ref[...] = x + _dot(o.astype(BF16), wo_ref[...])


def _cross_attend(x, mk, mv, g, w_cq, qn, w_co, *, layer, tm):
    b, t, _ = x.shape
    n = mk.shape[1]
    rows = pl.BlockSpec((None, tm, D_MODEL), lambda b_, i: (b_, i, 0))
    mem = pl.BlockSpec((None, n, D_MODEL), lambda b_, i: (b_, 0, 0))
    wsq = _resident((D_MODEL, D_MODEL), layer)
    return pl.pallas_call(
        _cross_kernel,
        grid=(b, t // tm),
        in_specs=[
            rows,
            _resident((1, D_MODEL), layer),
            wsq,
            _resident((1, X_HD), layer),
            mem, mem, wsq,
        ],
        out_specs=rows,
        out_shape=jax.ShapeDtypeStruct((b, t, D_MODEL), F32),
        compiler_params=_cparams(("parallel", "parallel")),
        name="cross_attn",
    )(x, g, w_cq, qn, mk, mv, w_co)


def _layer(x, l, depth, p0, rope, kv_rows, past_kv, pool_prev, hg_state, mk16, mv16, w, *, tiles):
    b, t, _ = x.shape
    n = b * t
    x2 = _ffn_half(x.reshape(n, D_MODEL), w['norm_ffn1'], w['w_ffn1_in'], w['w_ffn1_out'],
                   layer=l, tm=tiles['ffn'], tf=tiles['tf'])

    proj_a, proj_b, kf, vf, gates, qkv, *tiles_t = _in_proj(
        x2, w['norm_mix'], w['w_in'], w['group_mean'], w['att_qn'], w['att_kn'], *rope, kv_rows,
        tm=tiles['proj'], seq_len=t, tq=None if past_kv is not None else tiles['attn_bounded'],
        layer=l, depth=depth)

    lam_init = 0.8 - 0.6 * math.exp(-0.3 * l)
    lam = (jnp.exp(jnp.sum(w['lq1'][l] * w['lk1'][l])) - jnp.exp(jnp.sum(w['lq2'][l] * w['lk2'][l])) + lam_init)
    lam = jnp.full((1, ATT_DV), lam, F32)
    gsub = w['att_subln'][l].reshape(1, ATT_DV)
    qkv3 = qkv.reshape(b, t, 3 * SEG)
    if past_kv is None:
        bound = (ATT_DK ** 0.5) * jnp.max(jnp.abs(w['att_qn'][l])) * jnp.max(jnp.abs(w['att_kn'][l]))
        y_att = lax.cond(
            bound <= SCORE_BOUND_MAX,
            lambda a, q5, v5: _attn_bounded(a, q5, v5, lam, gsub, 1.0 - lam_init),
            lambda a, q5, v5: _attn_prompt(a, lam, gsub, 1.0 - lam_init, tq=tiles['attn']),
            qkv3, *tiles_t)
    else:
        y_att = _attn_sample(qkv3, past_kv[0], past_kv[1], lam, gsub, 1.0 - lam_init)

    x3, st_new, pool_new = _mix_out(
        x2.reshape(b, t, D_MODEL), proj_a.reshape(b, t, 2 * SEG), proj_b.reshape(b, t, 2 * SEG),
        gates.reshape(b, t, N_SEG_G * SEG),
        y_att, _state_to_block_diag(hg_state), pool_prev, w['lb'],
        w['pool_w'], w['pool_scale'], w['hg_outn'], w['w_up_pool'], w['w_up_att'], w['w_up_hgrn'], w['w_out'],
        layer=l, tm=tiles['mix'], p0=p0)

    x4 = _cross_attend(x3, mk16, mv16, w['norm_cross'], w['w_cq'], w['cross_qn'], w['w_co'],
                       layer=l, tm=tiles['cross'])
    x5 = _ffn_half(x4.reshape(n, D_MODEL), w['norm_ffn2'], w['w_ffn2_in'], w['w_ffn2_out'],
                   layer=l, tm=tiles['ffn'], tf=tiles['tf'])
    return (x5.reshape(b, t, D_MODEL), (kf, vf), pool_new[:, POOL_HIST - POOL_STATE:],
            _block_diag_to_state(st_new))


PROMPT_TILES = dict(ffn=1024, tf=256, proj=512, attn=256, attn_bounded=512, mix=512, cross=1024)
SAMPLE_TILES = dict(ffn=512, tf=256, proj=512, attn=64, mix=64, cross=64)


def kernel(x_prompt, x_sample, cache_attn_k, cache_attn_v, cache_mem_k, cache_mem_v, state_pool, state_hgrn, mem_prompt, norm_ffn1, w_ffn1_in, w_ffn1_out, norm_mix, w_in, pool_w, pool_scale, att_q_norm, att_k_norm, lambda_q1, lambda_k1, lambda_q2, lambda_k2, att_subln, hgrn_lower, hgrn_out_norm, w_up_pool, w_up_att, w_up_hgrn, w_out, norm_cross, norm_mem, w_cq, w_ckv, cross_q_norm, cross_k_norm, w_co, norm_ffn2, w_ffn2_in, w_ffn2_out):
    depth = w_in.shape[0]
    bp = x_prompt.shape[0]
    bs = x_sample.shape[0]
    p0_sample = cache_attn_k.shape[2]

    lp = jax.nn.softmax(hgrn_lower.astype(F32), axis=0)
    lbs = jnp.cumsum(lp, axis=0) - lp[0:1]

    gidx = jnp.arange(SEG) // ATT_DK
    group_mean = ((gidx[:, None] == gidx[None, :]).astype(F32) / ATT_DK).astype(BF16)

    def row(p):
        return p.reshape(depth, 1, p.shape[-1])

    weights = dict(
        norm_ffn1=row(norm_ffn1), w_ffn1_in=w_ffn1_in.astype(BF16), w_ffn1_out=w_ffn1_out.astype(BF16),
        norm_mix=row(norm_mix), w_in=w_in.astype(BF16), group_mean=group_mean,
        att_qn=row(jnp.tile(att_q_norm, (1, SEG // ATT_DK))), att_kn=row(jnp.tile(att_k_norm, (1, SEG // ATT_DK))),
        lq1=lambda_q1.astype(F32), lk1=lambda_k1.astype(F32), lq2=lambda_q2.astype(F32), lk2=lambda_k2.astype(F32),
        att_subln=att_subln,
        lb=row(lbs), pool_w=pool_w.astype(BF16), pool_scale=row(pool_scale), hg_outn=row(hgrn_out_norm),
        w_up_pool=w_up_pool.astype(BF16), w_up_att=w_up_att.astype(BF16),
        w_up_hgrn=w_up_hgrn.astype(BF16), w_out=w_out.astype(BF16),
        norm_cross=row(norm_cross), w_cq=w_cq.astype(BF16), cross_qn=row(cross_q_norm), w_co=w_co.astype(BF16),
        norm_ffn2=row(norm_ffn2), w_ffn2_in=w_ffn2_in.astype(BF16), w_ffn2_out=w_ffn2_out.astype(BF16),
    )
    norm_mem_r, w_ckv16, cross_kn_r = row(norm_mem), w_ckv.astype(BF16), row(cross_k_norm)

    def rope_for(p0, t, tm):
        tabs = _rope_tables(p0, t)
        if t < tm:
            tabs = tuple(jnp.concatenate([a] * (tm // t), axis=0) for a in tabs)
        return tabs

    rope_prompt = rope_for(0, x_prompt.shape[1], PROMPT_TILES['proj'])
    rope_sample = rope_for(p0_sample, x_sample.shape[1], SAMPLE_TILES['proj'])

    y = x_prompt
    pkv = None
    pmk, pmv, ppool, phg = [], [], [], []
    for l in range(depth):
        mk, mv, mk16, mv16 = _memory_kv(mem_prompt, norm_mem_r, w_ckv16, cross_kn_r, layer=l)
        pool0 = jnp.zeros((bp, POOL_HIST, SEG), F32)
        hg0 = jnp.zeros((bp, HG_HEADS, HG_DK, HG_DV), F32)
        y, pkv, pn, hn = _layer(y, l, depth, 0, rope_prompt, pkv, None, pool0, hg0,
                                mk16, mv16, weights, tiles=PROMPT_TILES)
        pmk.append(mk.reshape(bp, -1, X_HEADS, X_HD)); pmv.append(mv.reshape(bp, -1, X_HEADS, X_HD))
        ppool.append(pn); phg.append(hn)
    y_prompt = y
    kv_shape = (depth, bp, x_prompt.shape[1], ATT_HEADS, ATT_DV)
    pk, pv = pkv[0].reshape(kv_shape), pkv[1].reshape(kv_shape)

    y = x_sample
    skv = None
    spool, shg = [], []
    for l in range(depth):
        past = (cache_attn_k[l].reshape(bs, p0_sample, ATT_HEADS * 2 * ATT_DK).astype(BF16),
                cache_attn_v[l].reshape(bs, p0_sample, ATT_HEADS * ATT_DV).astype(BF16))
        pool0 = jnp.pad(state_pool[l], ((0, 0), (POOL_HIST - POOL_STATE, 0), (0, 0)))
        mk16 = cache_mem_k[l].reshape(bs, -1, D_MODEL).astype(BF16)
        mv16 = cache_mem_v[l].reshape(bs, -1, D_MODEL).astype(BF16)
        y, skv, pn, hn = _layer(y, l, depth, p0_sample, rope_sample, skv, past, pool0, state_hgrn[l],
                                mk16, mv16, weights, tiles=SAMPLE_TILES)
        spool.append(pn); shg.append(hn)
    y_sample = y
    kv_shape = (depth, bs, x_sample.shape[1], ATT_HEADS, ATT_DV)
    sk, sv = skv[0].reshape(kv_shape), skv[1].reshape(kv_shape)

    return (y_prompt, y_sample,
            pk, pv, jnp.stack(pmk), jnp.stack(pmv),
            jnp.stack(ppool), jnp.stack(phg),
            sk, sv, jnp.stack(spool), jnp.stack(shg))
```

```python
import functools
import math

import jax
import jax.numpy as jnp
from jax import lax
from jax.experimental import pallas as pl
from jax.experimental.pallas import tpu as pltpu

F32 = jnp.float32
BF16 = jnp.bfloat16

D_MODEL = 1024
CHUNK = 64
EPS = 1e-6
MASK_VALUE = -1e30
POOL_WINDOWS = (2, 4, 8, 16)
POOL_STATE = 15
POOL_HIST = 16
ATT_HEADS = 4
ATT_DK = 64
ATT_DV = 128
ROT_DIMS = 16
ROPE_THETA = 500000.0
HG_HEADS = 4
HG_DK = 64
HG_DV = 128
X_HEADS = 4
X_HD = 256
D_FF = 2816
SEG = 512
N_SEG_A = 7
N_SEG_G = 6
LANES = 128
VMEM_LIMIT = 56 * 1024 * 1024
SCORE_BOUND_MAX = 20.0
KEY_TILES_PER_ITER = 8
KEY_TILES_TAIL = 4


def _cparams(sem):
    return pltpu.CompilerParams(dimension_semantics=sem, vmem_limit_bytes=VMEM_LIMIT)


def _rms(x, g):
    ms = jnp.mean(x * x, axis=-1, keepdims=True)
    return x * lax.rsqrt(ms + EPS) * g


def _sigmoid(x):
    return 0.5 * jnp.tanh(0.5 * x) + 0.5


def _silu(x):
    return x * _sigmoid(x)


def _dot(a, b):
    return jnp.dot(a, b, preferred_element_type=F32)


def _dot_nt(a, b):
    return lax.dot_general(a, b, (((1,), (1,)), ((), ())), preferred_element_type=F32)


def _dot_tn(a, b):
    return lax.dot_general(a, b, (((0,), (0,)), ((), ())), preferred_element_type=F32)


def _shr(x, pow2):
    return lax.shift_right_logical(x, jnp.int32(int(math.log2(pow2))))


def _ffn_kernel(x_ref, g_ref, wi_ref, wo_ref, o_ref, *, tf):
    x = x_ref[...]
    h = _rms(x, g_ref[...]).astype(BF16)
    acc = None
    for c in range(D_FF // tf):
        a = _dot(h, wi_ref[:, pl.ds(c * tf, tf)])
        b = _dot(h, wi_ref[:, pl.ds(D_FF + c * tf, tf)])
        part = _dot((_silu(a) * b).astype(BF16), wo_ref[pl.ds(c * tf, tf), :])
        acc = part if acc is None else acc + part
    o_ref[...] = x + 0.5 * acc


def _resident(shape, layer=None):
    if layer is None:
        return pl.BlockSpec(shape, lambda *_: (0,) * len(shape), pipeline_mode=pl.Buffered(1))
    return pl.BlockSpec((None,) + tuple(shape), lambda *_: (layer,) + (0,) * len(shape),
                        pipeline_mode=pl.Buffered(1))


def _ffn_half(x, g, w_i, w_o, *, layer, tm, tf):
    n = x.shape[0]
    return pl.pallas_call(
        functools.partial(_ffn_kernel, tf=tf),
        grid=(n // tm,),
        in_specs=[
            pl.BlockSpec((tm, D_MODEL), lambda i: (i, 0)),
            _resident((1, D_MODEL), layer),
            _resident((D_MODEL, 2 * D_FF), layer),
            _resident((D_FF, D_MODEL), layer),
        ],
        out_specs=pl.BlockSpec((tm, D_MODEL), lambda i: (i, 0)),
        out_shape=jax.ShapeDtypeStruct((n, D_MODEL), F32),
        compiler_params=_cparams(("parallel",)),
        name="ffn_half",
    )(x, g, w_i, w_o)


def _inproj_kernel(x_ref, g_ref, w_ref, gm_ref, qn_ref, kn_ref, cos_ref, sa_ref, sb_ref, *rest, tq, chained):
    rest = rest[2:] if chained else rest
    pa_ref, pb_ref, kf_ref, vf_ref, pg_ref, qkv_ref = rest[:6]
    qt_ref, vt_ref = rest[6:8] if tq is not None else (None, None)

    def store_head_rows(ref, y):
        for h in range(ATT_HEADS):
            ref[pl.ds(h, y.shape[0], stride=ATT_HEADS), :] = y[:, h * ATT_DV:(h + 1) * ATT_DV]

    def store_transposed(ref, y):
        if ref is None:
            return
        yt = y.T.astype(BF16)
        for h in range(ATT_HEADS):
            for c in range(y.shape[0] // tq):
                ref[h, c] = yt[h * ATT_DV:(h + 1) * ATT_DV, c * tq:(c + 1) * tq]

    h = _rms(x_ref[...], g_ref[...]).astype(BF16)

    def proj(seg):
        return _dot(h, w_ref[:, pl.ds(seg * SEG, SEG)])

    rep = SEG // LANES
    cos = jnp.concatenate([cos_ref[...]] * rep, axis=1)
    sa = jnp.concatenate([sa_ref[...]] * rep, axis=1)
    sb = jnp.concatenate([sb_ref[...]] * rep, axis=1)

    def norm_rope(y, gain):
        ms = _dot((y * y).astype(BF16), gm_ref[...])
        yn = y * lax.rsqrt(ms + EPS) * gain
        half = ROT_DIMS // 2
        return yn * cos + pltpu.roll(yn, half, 1) * sa + pltpu.roll(yn, SEG - half, 1) * sb

    pa_ref[:, pl.ds(0, SEG)] = proj(0)

    q = norm_rope(proj(1), qn_ref[...]) * (ATT_DK ** -0.5)
    qkv_ref[:, pl.ds(0, SEG)] = q.astype(BF16)
    store_transposed(qt_ref, q)

    k = norm_rope(proj(2), kn_ref[...])
    store_head_rows(kf_ref, k)
    qkv_ref[:, pl.ds(SEG, SEG)] = k.astype(BF16)

    v = proj(3)
    store_head_rows(vf_ref, v)
    qkv_ref[:, pl.ds(2 * SEG, SEG)] = v.astype(BF16)
    store_transposed(vt_ref, v)

    pa_ref[:, pl.ds(SEG, SEG)] = proj(4)
    pb_ref[:, pl.ds(0, SEG)] = proj(5).astype(BF16)
    pb_ref[:, pl.ds(SEG, SEG)] = proj(6).astype(BF16)
    for s in range(N_SEG_G):
        pg_ref[:, pl.ds(s * SEG, SEG)] = proj(N_SEG_A + s).astype(BF16)


def _in_proj(x, g, w, gm, qn, kn, cos, sa, sb, kv_rows, *, tm, seq_len, tq, layer, depth):
    n = x.shape[0]
    tab_blocks = cos.shape[0] // tm
    nseg = N_SEG_A + N_SEG_G
    tab_spec = pl.BlockSpec((tm, LANES), lambda i: (i % tab_blocks, 0))
    kv_spec = pl.BlockSpec((tm * ATT_HEADS, ATT_DV), lambda i: (layer * (n // tm) + i, 0))
    kv_shape = jax.ShapeDtypeStruct((depth * n * ATT_HEADS, ATT_DV), F32)

    def rows(width):
        return pl.BlockSpec((tm, width), lambda i: (i, 0))

    out_specs = [rows(2 * SEG), rows(2 * SEG), kv_spec, kv_spec, rows(N_SEG_G * SEG), rows(3 * SEG)]
    out_shape = [
        jax.ShapeDtypeStruct((n, 2 * SEG), F32),
        jax.ShapeDtypeStruct((n, 2 * SEG), BF16),
        kv_shape, kv_shape,
        jax.ShapeDtypeStruct((n, N_SEG_G * SEG), BF16),
        jax.ShapeDtypeStruct((n, 3 * SEG), BF16),
    ]
    chained = kv_rows is not None
    n_in = 9
    any_spec = pl.BlockSpec(memory_space=pl.ANY)
    extra_in = list(kv_rows) if chained else []
    aliases = {n_in: 2, n_in + 1: 3} if chained else {}
    if tq is not None:
        per_seq = seq_len // tm
        t_spec = pl.BlockSpec((None, ATT_HEADS, tm // tq, ATT_DV, tq),
                              lambda i: (i // per_seq, 0, i % per_seq, 0, 0))
        t_shape = jax.ShapeDtypeStruct((n // seq_len, ATT_HEADS, seq_len // tq, ATT_DV, tq), BF16)
        out_specs += [t_spec, t_spec]
        out_shape += [t_shape, t_shape]
    return pl.pallas_call(
        functools.partial(_inproj_kernel, tq=tq, chained=chained),
        grid=(n // tm,),
        in_specs=[
            rows(D_MODEL),
            _resident((1, D_MODEL), layer),
            _resident((D_MODEL, nseg * SEG), layer),
            _resident((SEG, SEG)),
            _resident((1, SEG), layer), _resident((1, SEG), layer), tab_spec, tab_spec, tab_spec,
        ] + [any_spec] * len(extra_in),
        out_specs=out_specs,
        out_shape=out_shape,
        input_output_aliases=aliases,
        compiler_params=_cparams(("parallel",)),
        name="in_proj",
    )(x, g, w, gm, qn, kn, cos, sa, sb, *extra_in)


def _rope_tables(p0, t):
    half = ROT_DIMS // 2
    inv = ROPE_THETA ** (-jnp.arange(half, dtype=F32) / half)
    in_head = jnp.arange(LANES) % ATT_DK
    pos = p0 + jnp.arange(t, dtype=jnp.int32)
    ang = pos.astype(F32)[:, None] * inv[in_head % half][None, :]
    cos, sin = jnp.cos(ang), jnp.sin(ang)
    lower = (in_head < half)[None, :]
    upper = jnp.logical_and(in_head >= half, in_head < ROT_DIMS)[None, :]
    return (jnp.where(in_head[None, :] < ROT_DIMS, cos, 1.0),
            jnp.where(upper, sin, 0.0),
            jnp.where(lower, -sin, 0.0))


def _stack_maps(q):
    lane = lax.broadcasted_iota(jnp.int32, q.shape, 1)
    zero = jnp.zeros_like(q)
    return jnp.concatenate([jnp.where(lane < ATT_DK, q, zero), jnp.where(lane >= ATT_DK, q, zero)], axis=0)


def _attn_finish(acc, l, lam, gsub, post_scale, tq):
    o = acc[:tq] / l[:tq] - lam * (acc[tq:] / l[tq:])
    return _rms(o, gsub) * post_scale


def _attn_prompt_kernel(lam_ref, gsub_ref, q_ref, k_ref, v_ref, o_ref, acc_sc, m_sc, l_sc, *, tq, post_scale):
    i = pl.program_id(2)
    qbd = _stack_maps(q_ref[...])
    m_sc[...] = jnp.full_like(m_sc, MASK_VALUE)
    l_sc[...] = jnp.zeros_like(l_sc)
    acc_sc[...] = jnp.zeros_like(acc_sc)

    def step(j, masked):
        off = pl.multiple_of(j * tq, tq)
        kj = k_ref[pl.ds(off, tq), :]
        vj = v_ref[pl.ds(off, tq), :]
        s = _dot_nt(qbd, kj)
        if masked:
            qi = lax.broadcasted_iota(jnp.int32, s.shape, 0) & (tq - 1)
            ki = lax.broadcasted_iota(jnp.int32, s.shape, 1)
            s = jnp.where(_shr(ki, CHUNK) <= _shr(qi, CHUNK), s, MASK_VALUE)
        m_prev = m_sc[...]
        m_new = jnp.maximum(m_prev, jnp.max(s, axis=-1, keepdims=True))
        alpha = jnp.exp(m_prev - m_new)
        p = jnp.exp(s - m_new)
        l_sc[...] = alpha * l_sc[...] + jnp.sum(p, axis=-1, keepdims=True)
        acc_sc[...] = alpha * acc_sc[...] + _dot(p.astype(BF16), vj)
        m_sc[...] = m_new

    def body(j, carry):
        step(j, False)
        return carry

    lax.fori_loop(0, i, body, 0)
    step(i, True)
    o_ref[...] = _attn_finish(acc_sc[...], l_sc[...], lam_ref[...], gsub_ref[...], post_scale, tq)


def _attn_prompt(qkv, lam, gsub, post_scale, *, tq):
    b, t, _ = qkv.shape
    kern = functools.partial(_attn_prompt_kernel, tq=tq, post_scale=post_scale)
    vec_spec = pl.BlockSpec((1, ATT_DV), lambda b_, h, i: (0, 0))
    return pl.pallas_call(
        kern,
        grid=(b, ATT_HEADS, t // tq),
        in_specs=[
            vec_spec, vec_spec,
            pl.BlockSpec((None, tq, ATT_DV), lambda b_, h, i: (b_, i, h)),
            pl.BlockSpec((None, t, ATT_DV), lambda b_, h, i: (b_, 0, ATT_HEADS + h)),
            pl.BlockSpec((None, t, ATT_DV), lambda b_, h, i: (b_, 0, 2 * ATT_HEADS + h)),
        ],
        out_specs=pl.BlockSpec((None, tq, ATT_DV), lambda b_, h, i: (b_, i, h)),
        out_shape=jax.ShapeDtypeStruct((b, t, ATT_HEADS * ATT_DV), F32),
        scratch_shapes=[pltpu.VMEM((2 * tq, ATT_DV), F32), pltpu.VMEM((2 * tq, 1), F32),
                        pltpu.VMEM((2 * tq, 1), F32)],
        compiler_params=_cparams(("parallel", "parallel", "arbitrary")),
        name="diff_attn_prompt",
    )(lam, gsub, qkv, qkv, qkv)


def _attn_bounded_kernel(lam_ref, gsub_ref, qt_ref, k_ref, vt_ref, o_ref, acc_sc, l_sc, *, tq, post_scale):
    i = pl.program_id(2)
    qt = qt_ref[...].astype(F32)
    row = lax.broadcasted_iota(jnp.int32, qt.shape, 0)
    qbd = jnp.concatenate([jnp.where(row < ATT_DK, qt, 0.0), jnp.where(row >= ATT_DK, qt, 0.0)],
                          axis=1).astype(BF16)
    acc_sc[...] = jnp.zeros_like(acc_sc)
    l_sc[...] = jnp.zeros_like(l_sc)

    def probs(j, masked):
        off = pl.multiple_of(j * tq, tq)
        s = _dot(k_ref[pl.ds(off, tq), :], qbd)
        if masked:
            ki = lax.broadcasted_iota(jnp.int32, s.shape, 0)
            qi = lax.broadcasted_iota(jnp.int32, s.shape, 1) & (tq - 1)
            s = jnp.where(_shr(ki, CHUNK) <= _shr(qi, CHUNK), s, MASK_VALUE)
        p = jnp.exp(s)
        return jnp.sum(p.reshape(tq // 8, 8, 2 * tq), axis=0), _dot(vt_ref[j], p.astype(BF16))

    def accumulate(first, count, diagonal_last=False):
        parts = [probs(first + c, diagonal_last and c == count - 1) for c in range(count)]
        l_sc[...] += functools.reduce(lambda x, y: x + y, [p_[0] for p_ in parts])
        acc_sc[...] += functools.reduce(lambda x, y: x + y, [p_[1] for p_ in parts])

    def group(jj, carry):
        accumulate(KEY_TILES_PER_ITER * jj, KEY_TILES_PER_ITER)
        return carry

    lax.fori_loop(0, i // KEY_TILES_PER_ITER, group, 0)
    left = i % KEY_TILES_PER_ITER

    @pl.when(left >= KEY_TILES_TAIL)
    def _():
        accumulate(i - left, KEY_TILES_TAIL)

    rem = left % KEY_TILES_TAIL
    for r in range(KEY_TILES_TAIL):
        @pl.when(rem == r)
        def _(r=r):
            accumulate(i - r, r + 1, diagonal_last=True)
    l = jnp.sum(l_sc[...], axis=0, keepdims=True)
    acc = acc_sc[...]
    ot = acc[:, :tq] / l[:, :tq] - lam_ref[...] * (acc[:, tq:] / l[:, tq:])
    ms = jnp.mean(ot * ot, axis=0, keepdims=True)
    yt = ot * lax.rsqrt(ms + EPS) * (gsub_ref[...] * post_scale)
    o_ref[...] = yt.T


def _attn_bounded(qkv, q5, v5, lam, gsub, post_scale):
    b, t, _ = qkv.shape
    nt, tq = q5.shape[2], q5.shape[4]
    kern = functools.partial(_attn_bounded_kernel, tq=tq, post_scale=post_scale)
    return pl.pallas_call(
        kern,
        grid=(b, ATT_HEADS, nt),
        in_specs=[
            pl.BlockSpec((1, 1), lambda b_, h, i: (0, 0)),
            pl.BlockSpec((ATT_DV, 1), lambda b_, h, i: (0, 0)),
            pl.BlockSpec((None, None, None, ATT_DV, tq), lambda b_, h, i: (b_, h, i, 0, 0)),
            pl.BlockSpec((None, t, ATT_DV), lambda b_, h, i: (b_, 0, ATT_HEADS + h)),
            pl.BlockSpec((None, None, nt, ATT_DV, tq), lambda b_, h, i: (b_, h, 0, 0, 0)),
        ],
        out_specs=pl.BlockSpec((None, tq, ATT_DV), lambda b_, h, i: (b_, i, h)),
        out_shape=jax.ShapeDtypeStruct((b, t, ATT_HEADS * ATT_DV), F32),
        scratch_shapes=[pltpu.VMEM((ATT_DV, 2 * tq), F32), pltpu.VMEM((8, 2 * tq), F32)],
        compiler_params=_cparams(("parallel", "parallel", "arbitrary")),
        name="diff_attn_bounded",
    )(lam[:, :1], gsub.reshape(ATT_DV, 1), q5, qkv, v5)


def _attn_sample_kernel(lam_ref, gsub_ref, q_ref, kn_ref, vn_ref, kp_ref, vp_ref, o_ref, *, tq, tp, post_scale):
    for h in range(ATT_HEADS):
        cols = pl.ds(h * ATT_DV, ATT_DV)
        head_rows = pl.ds(h, tp, stride=ATT_HEADS)
        qbd = _stack_maps(q_ref[:, cols])
        sp = _dot_nt(qbd, kp_ref[head_rows, :].astype(BF16))
        sn = _dot_nt(qbd, kn_ref[:, cols])
        m = jnp.maximum(jnp.max(sp, axis=-1, keepdims=True), jnp.max(sn, axis=-1, keepdims=True))
        pp = jnp.exp(sp - m)
        pn = jnp.exp(sn - m)
        l = jnp.sum(pp, axis=-1, keepdims=True) + jnp.sum(pn, axis=-1, keepdims=True)
        acc = (_dot(pp.astype(BF16), vp_ref[head_rows, :].astype(BF16))
               + _dot(pn.astype(BF16), vn_ref[:, cols]))
        o_ref[:, cols] = _attn_finish(acc, l, lam_ref[...], gsub_ref[...], post_scale, tq)


def _attn_sample(qkv, kp, vp, lam, gsub, post_scale, *, layer):
    b, t, _ = qkv.shape
    tp = kp.shape[2] // ATT_HEADS
    kern = functools.partial(_attn_sample_kernel, tq=t, tp=tp, post_scale=post_scale)
    vec_spec = pl.BlockSpec((1, ATT_DV), lambda b_: (0, 0))
    past_spec = pl.BlockSpec((None, None, tp * ATT_HEADS, ATT_DV), lambda b_: (layer, b_, 0, 0))
    width = ATT_HEADS * ATT_DV
    return pl.pallas_call(
        kern,
        grid=(b,),
        in_specs=[
            vec_spec, vec_spec,
            pl.BlockSpec((None, t, width), lambda b_: (b_, 0, 0)),
            pl.BlockSpec((None, t, width), lambda b_: (b_, 0, 1)),
            pl.BlockSpec((None, t, width), lambda b_: (b_, 0, 2)),
            past_spec, past_spec,
        ],
        out_specs=pl.BlockSpec((None, t, width), lambda b_: (b_, 0, 0)),
        out_shape=jax.ShapeDtypeStruct((b, t, width), F32),
        compiler_params=_cparams(("parallel",)),
        name="diff_attn_sample",
    )(lam, gsub, qkv, qkv, qkv, kp, vp)


HG_W = HG_HEADS * HG_DK
HG_V = HG_HEADS * HG_DV
B_PAD = 8


def _level_reference(b_sc, w):
    if w >= 4:
        pieces = []
        for p in range(CHUNK // (2 * w)):
            row = b_sc[pl.ds(B_PAD + p * 2 * w + w, 1), :]
            pieces.append(jnp.broadcast_to(row, (2 * w, HG_W)))
        return jnp.concatenate(pieces, axis=0)
    t = lax.broadcasted_iota(jnp.int32, (CHUNK, HG_W), 0)
    phase = t & (2 * w - 1)
    r = None
    for ph in range(2 * w):
        shifted = b_sc[pl.ds(B_PAD + w - ph, CHUNK), :]
        r = shifted if r is None else jnp.where(phase == ph, shifted, r)
    return r


def _hgrn_chunk(hq, hf, hi, lb, st_sc, b_sc, masks):
    tril, bd_k, bd_v, bd_s = masks
    f = lb + (1.0 - lb) * jax.nn.sigmoid(hf)
    g = jnp.log(f)
    kk = 1.0 - f
    q = _silu(hq) * (HG_DK ** -0.5)
    g0 = g.astype(BF16)
    r1 = g - g0.astype(F32)
    g1 = r1.astype(BF16)
    g2 = (r1 - g1.astype(F32)).astype(BF16)
    b = _dot(tril, g0) + _dot(tril, g1) + _dot(tril, g2)
    b_sc[pl.ds(B_PAD, CHUNK), :] = b

    t_idx = lax.broadcasted_iota(jnp.int32, (CHUNK, HG_W), 0)
    s_idx = lax.broadcasted_iota(jnp.int32, (CHUNK, HG_W), 1) & (CHUNK - 1)
    zero = jnp.zeros((CHUNK, HG_W), F32)

    def block_diag_k(x):
        return jnp.where(bd_k, jnp.concatenate([x] * HG_HEADS, axis=0), jnp.zeros((), F32)).astype(BF16)

    a = jnp.where(t_idx == s_idx, _dot_nt(q.astype(BF16), block_diag_k(kk)), zero)
    w = CHUNK // 2
    while w >= 1:
        r = _level_reference(b_sc, w)
        upper = (t_idx & w) != 0
        e = jnp.exp(jnp.where(upper, b - r, r - b))
        ql = jnp.where(upper, q * e, zero)
        kl = jnp.where(upper, zero, kk * e)
        same_pair = _shr(t_idx, 2 * w) == _shr(s_idx, 2 * w)
        a = a + jnp.where(same_pair, _dot_nt(ql.astype(BF16), block_diag_k(kl)), zero)
        w //= 2

    b_last = jnp.broadcast_to(b_sc[pl.ds(B_PAD + CHUNK - 1, 1), :], (CHUNK, HG_W))
    qb = q * jnp.exp(b)
    kd = kk * jnp.exp(b_last - b)
    v16 = hi
    vbd = jnp.where(bd_v, jnp.concatenate([hi.astype(F32)] * HG_HEADS, axis=0), jnp.zeros((), F32)).astype(BF16)
    st = st_sc[...]
    o = _dot(a.astype(BF16), vbd) + _dot_nt(qb.astype(BF16), st.astype(BF16))
    upd = _dot_tn(v16, kd.astype(BF16))
    decay = jnp.exp(b_sc[pl.ds(B_PAD + CHUNK - 1, 1), :])
    st_sc[...] = st * decay + jnp.where(bd_s, upd, jnp.zeros((), F32))
    return o


def _mix_kernel(x_ref, u_ref, hqf_ref, hi_ref, hg_ref, g_ref, ya_ref, st0_ref, pool0_ref,
                lb_ref, pw_ref, ps_ref, hgn_ref, wup_ref, wua_ref, wuh_ref, wo_ref,
                xo_ref, stn_ref, pooln_ref,
                st_sc, b_sc, ext_sc, yh_sc, *, tm, p0):
    it = pl.program_id(1)

    @pl.when(it == 0)
    def _():
        st_sc[...] = st0_ref[...]
        ext_sc[pl.ds(0, POOL_HIST), :] = pool0_ref[...]
        b_sc[...] = jnp.zeros_like(b_sc)

    u = u_ref[...]
    ext_sc[pl.ds(POOL_HIST, tm), :] = u
    row = lax.broadcasted_iota(jnp.int32, (tm, LANES), 0)
    seen = (p0 + 1 + it * tm + row).astype(F32)
    cols = []
    for gi, w in enumerate(POOL_WINDOWS):
        sl = pl.ds(gi * LANES, LANES)
        win = u[:, gi * LANES:(gi + 1) * LANES]
        for d in range(1, w):
            win = win + ext_sc[pl.ds(POOL_HIST - d, tm), sl]
        dlt = win / jnp.minimum(seen, float(w)) - u[:, gi * LANES:(gi + 1) * LANES]
        cols.append(_dot(dlt.astype(BF16), pw_ref[gi]))
    y_pool = jnp.concatenate(cols, axis=1) * ps_ref[...]
    ext_sc[pl.ds(0, POOL_HIST), :] = ext_sc[pl.ds(tm, POOL_HIST), :]

    tri_r = lax.broadcasted_iota(jnp.int32, (CHUNK, CHUNK), 0)
    tri_c = lax.broadcasted_iota(jnp.int32, (CHUNK, CHUNK), 1)
    tril = (tri_c <= tri_r).astype(BF16)
    rk = lax.broadcasted_iota(jnp.int32, (HG_W, HG_W), 0)
    ck = lax.broadcasted_iota(jnp.int32, (HG_W, HG_W), 1)
    bd_k = _shr(rk, CHUNK) == _shr(ck, HG_DK)
    rv = lax.broadcasted_iota(jnp.int32, (HG_W, HG_V), 0)
    cv = lax.broadcasted_iota(jnp.int32, (HG_W, HG_V), 1)
    bd_v = _shr(rv, CHUNK) == _shr(cv, HG_DV)
    rs = lax.broadcasted_iota(jnp.int32, (HG_V, HG_W), 0)
    cs = lax.broadcasted_iota(jnp.int32, (HG_V, HG_W), 1)
    bd_s = _shr(rs, HG_DV) == _shr(cs, HG_DK)
    masks = (tril, bd_k, bd_v, bd_s)
    lb = lb_ref[...]
    for c in range(tm // CHUNK):
        rows = pl.ds(c * CHUNK, CHUNK)
        o = _hgrn_chunk(hqf_ref[rows, pl.ds(0, HG_W)], hqf_ref[rows, pl.ds(HG_W, HG_W)],
                        hi_ref[rows, :], lb, st_sc, b_sc, masks)
        yh_sc[rows, :] = o
    oh = yh_sc[...]
    hg = hg_ref[...].astype(F32)
    heads = []
    for h in range(HG_HEADS):
        sl = slice(h * HG_DV, (h + 1) * HG_DV)
        heads.append(_rms(oh[:, sl], hgn_ref[...]) * _silu(hg[:, sl]))
    y_hg = jnp.concatenate(heads, axis=1)

    gts = g_ref[...].astype(F32)
    merged = (_sigmoid(gts[:, :D_MODEL]) * _dot(y_pool.astype(BF16), wup_ref[...])
              + _sigmoid(gts[:, D_MODEL:2 * D_MODEL]) * _dot(ya_ref[...].astype(BF16), wua_ref[...])
              + _sigmoid(gts[:, 2 * D_MODEL:]) * _dot(y_hg.astype(BF16), wuh_ref[...]))
    xo_ref[...] = x_ref[...] + _dot(merged.astype(BF16), wo_ref[...])

    @pl.when(it == pl.num_programs(1) - 1)
    def _():
        stn_ref[...] = st_sc[...]
        pooln_ref[...] = ext_sc[pl.ds(0, POOL_HIST), :]


def _mix_out(x, proj_a, proj_b, gates, y_att, st0, pool0, lb, pool_w, pool_scale, hg_outn,
             w_up_pool, w_up_att, w_up_hgrn, w_out, *, layer, tm, p0):
    b, t, _ = x.shape
    kern = functools.partial(_mix_kernel, tm=tm, p0=p0)

    def rows(width, col):
        return pl.BlockSpec((None, tm, width), lambda b_, i: (b_, i, col))

    def const(shape):
        return _resident(shape, layer)

    def per_batch(shape):
        return pl.BlockSpec((None,) + shape, lambda b_, i: (b_, 0, 0))

    return pl.pallas_call(
        kern,
        grid=(b, t // tm),
        in_specs=[
            rows(D_MODEL, 0),
            rows(SEG, 0),
            rows(SEG, 1),
            rows(SEG, 0),
            rows(SEG, 1),
            rows(3 * D_MODEL, 0),
            rows(SEG, 0),
            per_batch((HG_V, HG_W)),
            per_batch((POOL_HIST, SEG)),
            const((1, HG_W)), const((4, LANES, LANES)), const((1, SEG)), const((1, HG_DV)),
            const((SEG, D_MODEL)), const((SEG, D_MODEL)), const((SEG, D_MODEL)), const((D_MODEL, D_MODEL)),
        ],
        out_specs=[
            rows(D_MODEL, 0),
            per_batch((HG_V, HG_W)),
            per_batch((POOL_HIST, SEG)),
        ],
        out_shape=[
            jax.ShapeDtypeStruct((b, t, D_MODEL), F32),
            jax.ShapeDtypeStruct((b, HG_V, HG_W), F32),
            jax.ShapeDtypeStruct((b, POOL_HIST, SEG), F32),
        ],
        scratch_shapes=[
            pltpu.VMEM((HG_V, HG_W), F32),
            pltpu.VMEM((CHUNK + 2 * B_PAD, HG_W), F32),
            pltpu.VMEM((POOL_HIST + tm, SEG), F32),
            pltpu.VMEM((tm, HG_V), F32),
        ],
        compiler_params=_cparams(("parallel", "arbitrary")),
        name="mix_out",
    )(x, proj_a, proj_a, proj_b, proj_b, gates, y_att, st0, pool0,
      lb, pool_w, pool_scale, hg_outn, w_up_pool, w_up_att, w_up_hgrn, w_out)


def _state_to_block_diag(s):
    b = s.shape[0]
    st = jnp.swapaxes(s, 2, 3)
    eye = jnp.eye(HG_HEADS, dtype=s.dtype)
    return jnp.einsum('bhed,hg->bhegd', st, eye).reshape(b, HG_V, HG_W)


def _block_diag_to_state(st):
    b = st.shape[0]
    s5 = st.reshape(b, HG_HEADS, HG_DV, HG_HEADS, HG_DK)
    diag = jnp.stack([s5[:, h, :, h, :] for h in range(HG_HEADS)], axis=1)
    return jnp.swapaxes(diag, 2, 3)


def _head_rms(x, gain):
    return jnp.concatenate(
        [_rms(x[:, h * X_HD:(h + 1) * X_HD], gain) for h in range(X_HEADS)], axis=1)


def _memkv_kernel(m_ref, g_ref, w_ref, kn_ref, k_ref, v_ref, k16_ref, v16_ref):
    h = _rms(m_ref[...], g_ref[...]).astype(BF16)
    kv = _dot(h, w_ref[...])
    mk = _head_rms(kv[:, :D_MODEL], kn_ref[...])
    mv = kv[:, D_MODEL:]
    k_ref[...] = mk
    v_ref[...] = mv
    k16_ref[...] = mk.astype(BF16)
    v16_ref[...] = mv.astype(BF16)


def _memory_kv(mem, g, w_ckv, kn, *, layer):
    b, n, _ = mem.shape
    blk = pl.BlockSpec((None, n, D_MODEL), lambda b_: (b_, 0, 0))
    return pl.pallas_call(
        _memkv_kernel,
        grid=(b,),
        in_specs=[
            blk,
            _resident((1, D_MODEL), layer),
            _resident((D_MODEL, 2 * D_MODEL), layer),
            _resident((1, X_HD), layer),
        ],
        out_specs=[blk, blk, blk, blk],
        out_shape=[jax.ShapeDtypeStruct((b, n, D_MODEL), F32)] * 2
        + [jax.ShapeDtypeStruct((b, n, D_MODEL), BF16)] * 2,
        compiler_params=_cparams(("parallel",)),
        name="memory_kv",
    )(mem, g, w_ckv, kn)


def _cross_kernel(x_ref, g_ref, wq_ref, qn_ref, mk_ref, mv_ref, wo_ref, o_ref, *, cache_rows):
    x = x_ref[...]
    q = _head_rms(_dot(_rms(x, g_ref[...]).astype(BF16), wq_ref[...]), qn_ref[...])
    q = (q * (X_HD ** -0.5)).astype(BF16)
    outs = []
    for h in range(X_HEADS):
        sl = slice(h * X_HD, (h + 1) * X_HD)
        if cache_rows is None:
            mk, mv = mk_ref[:, sl], mv_ref[:, sl]
        else:
            parts = X_HD // LANES

            def head_of(ref):
                halves = [ref[pl.ds(parts * h + c, cache_rows, stride=parts * X_HEADS), :] for c in range(parts)]
                return jnp.concatenate(halves, axis=1).astype(BF16)

            mk, mv = head_of(mk_ref), head_of(mv_ref)
        s = _dot_nt(q[:, sl], mk)
        p = jnp.exp(s - jnp.max(s, axis=-1, keepdims=True))
        p = p / jnp.sum(p, axis=-1, keepdims=True)
        outs.append(_dot(p.astype(BF16), mv))
    o = jnp.concatenate(outs, axis=1)
    o_ref[...] = x + _dot(o.astype(BF16), wo_ref[...])


def _cross_attend(x, mk, mv, g, w_cq, qn, w_co, *, layer, tm):
    b, t, _ = x.shape
    if mk.ndim == 3:
        cache_rows = None
        mem = pl.BlockSpec((None, mk.shape[1], D_MODEL), lambda b_, i: (b_, 0, 0))
    else:
        cache_rows = mk.shape[2] * LANES // D_MODEL
        mem = pl.BlockSpec((None, None, mk.shape[2], LANES), lambda b_, i: (layer, b_, 0, 0))
    rows = pl.BlockSpec((None, tm, D_MODEL), lambda b_, i: (b_, i, 0))
    wsq = _resident((D_MODEL, D_MODEL), layer)
    return pl.pallas_call(
        functools.partial(_cross_kernel, cache_rows=cache_rows),
        grid=(b, t // tm),
        in_specs=[
            rows,
            _resident((1, D_MODEL), layer),
            wsq,
            _resident((1, X_HD), layer),
            mem, mem, wsq,
        ],
        out_specs=rows,
        out_shape=jax.ShapeDtypeStruct((b, t, D_MODEL), F32),
        compiler_params=_cparams(("parallel", "parallel")),
        name="cross_attn",
    )(x, g, w_cq, qn, mk, mv, w_co)


def _layer(x, l, depth, p0, rope, kv_rows, past_kv, pool_prev, hg_state, mk16, mv16, w, *, tiles):
    b, t, _ = x.shape
    n = b * t
    x2 = _ffn_half(x.reshape(n, D_MODEL), w['norm_ffn1'], w['w_ffn1_in'], w['w_ffn1_out'],
                   layer=l, tm=tiles['ffn'], tf=tiles['tf'])

    proj_a, proj_b, kf, vf, gates, qkv, *tiles_t = _in_proj(
        x2, w['norm_mix'], w['w_in'], w['group_mean'], w['att_qn'], w['att_kn'], *rope, kv_rows,
        tm=tiles['proj'], seq_len=t, tq=None if past_kv is not None else tiles['attn_bounded'],
        layer=l, depth=depth)

    lam_init = 0.8 - 0.6 * math.exp(-0.3 * l)
    lam = (jnp.exp(jnp.sum(w['lq1'][l] * w['lk1'][l])) - jnp.exp(jnp.sum(w['lq2'][l] * w['lk2'][l])) + lam_init)
    lam = jnp.full((1, ATT_DV), lam, F32)
    gsub = w['att_subln'][l].reshape(1, ATT_DV)
    qkv3 = qkv.reshape(b, t, 3 * SEG)
    if past_kv is None:
        bound = (ATT_DK ** 0.5) * jnp.max(jnp.abs(w['att_qn'][l])) * jnp.max(jnp.abs(w['att_kn'][l]))
        y_att = lax.cond(
            bound <= SCORE_BOUND_MAX,
            lambda a, q5, v5: _attn_bounded(a, q5, v5, lam, gsub, 1.0 - lam_init),
            lambda a, q5, v5: _attn_prompt(a, lam, gsub, 1.0 - lam_init, tq=tiles['attn']),
            qkv3, *tiles_t)
    else:
        y_att = _attn_sample(qkv3, past_kv[0], past_kv[1], lam, gsub, 1.0 - lam_init, layer=l)

    x3, st_new, pool_new = _mix_out(
        x2.reshape(b, t, D_MODEL), proj_a.reshape(b, t, 2 * SEG), proj_b.reshape(b, t, 2 * SEG),
        gates.reshape(b, t, N_SEG_G * SEG),
        y_att, _state_to_block_diag(hg_state), pool_prev, w['lb'],
        w['pool_w'], w['pool_scale'], w['hg_outn'], w['w_up_pool'], w['w_up_att'], w['w_up_hgrn'], w['w_out'],
        layer=l, tm=tiles['mix'], p0=p0)

    x4 = _cross_attend(x3, mk16, mv16, w['norm_cross'], w['w_cq'], w['cross_qn'], w['w_co'],
                       layer=l, tm=tiles['cross'])
    x5 = _ffn_half(x4.reshape(n, D_MODEL), w['norm_ffn2'], w['w_ffn2_in'], w['w_ffn2_out'],
                   layer=l, tm=tiles['ffn'], tf=tiles['tf'])
    return (x5.reshape(b, t, D_MODEL), (kf, vf), pool_new[:, POOL_HIST - POOL_STATE:],
            _block_diag_to_state(st_new))


PROMPT_TILES = dict(ffn=1024, tf=256, proj=512, attn=256, attn_bounded=512, mix=512, cross=1024)
SAMPLE_TILES = dict(ffn=512, tf=256, proj=512, attn=64, mix=64, cross=64)


def kernel(x_prompt, x_sample, cache_attn_k, cache_attn_v, cache_mem_k, cache_mem_v, state_pool, state_hgrn, mem_prompt, norm_ffn1, w_ffn1_in, w_ffn1_out, norm_mix, w_in, pool_w, pool_scale, att_q_norm, att_k_norm, lambda_q1, lambda_k1, lambda_q2, lambda_k2, att_subln, hgrn_lower, hgrn_out_norm, w_up_pool, w_up_att, w_up_hgrn, w_out, norm_cross, norm_mem, w_cq, w_ckv, cross_q_norm, cross_k_norm, w_co, norm_ffn2, w_ffn2_in, w_ffn2_out):
    depth = w_in.shape[0]
    bp = x_prompt.shape[0]
    bs = x_sample.shape[0]
    p0_sample = cache_attn_k.shape[2]

    lp = jax.nn.softmax(hgrn_lower.astype(F32), axis=0)
    lbs = jnp.cumsum(lp, axis=0) - lp[0:1]

    gidx = jnp.arange(SEG) // ATT_DK
    group_mean = ((gidx[:, None] == gidx[None, :]).astype(F32) / ATT_DK).astype(BF16)

    def row(p):
        return p.reshape(depth, 1, p.shape[-1])

    weights = dict(
        norm_ffn1=row(norm_ffn1), w_ffn1_in=w_ffn1_in.astype(BF16), w_ffn1_out=w_ffn1_out.astype(BF16),
        norm_mix=row(norm_mix), w_in=w_in.astype(BF16), group_mean=group_mean,
        att_qn=row(jnp.tile(att_q_norm, (1, SEG // ATT_DK))), att_kn=row(jnp.tile(att_k_norm, (1, SEG // ATT_DK))),
        lq1=lambda_q1.astype(F32), lk1=lambda_k1.astype(F32), lq2=lambda_q2.astype(F32), lk2=lambda_k2.astype(F32),
        att_subln=att_subln,
        lb=row(lbs), pool_w=pool_w.astype(BF16), pool_scale=row(pool_scale), hg_outn=row(hgrn_out_norm),
        w_up_pool=w_up_pool.astype(BF16), w_up_att=w_up_att.astype(BF16),
        w_up_hgrn=w_up_hgrn.astype(BF16), w_out=w_out.astype(BF16),
        norm_cross=row(norm_cross), w_cq=w_cq.astype(BF16), cross_qn=row(cross_q_norm), w_co=w_co.astype(BF16),
        norm_ffn2=row(norm_ffn2), w_ffn2_in=w_ffn2_in.astype(BF16), w_ffn2_out=w_ffn2_out.astype(BF16),
    )
    norm_mem_r, w_ckv16, cross_kn_r = row(norm_mem), w_ckv.astype(BF16), row(cross_k_norm)

    def rope_for(p0, t, tm):
        tabs = _rope_tables(p0, t)
        if t < tm:
            tabs = tuple(jnp.concatenate([a] * (tm // t), axis=0) for a in tabs)
        return tabs

    rope_prompt = rope_for(0, x_prompt.shape[1], PROMPT_TILES['proj'])
    rope_sample = rope_for(p0_sample, x_sample.shape[1], SAMPLE_TILES['proj'])

    y = x_prompt
    pkv = None
    pmk, pmv, ppool, phg = [], [], [], []
    for l in range(depth):
        mk, mv, mk16, mv16 = _memory_kv(mem_prompt, norm_mem_r, w_ckv16, cross_kn_r, layer=l)
        pool0 = jnp.zeros((bp, POOL_HIST, SEG), F32)
        hg0 = jnp.zeros((bp, HG_HEADS, HG_DK, HG_DV), F32)
        y, pkv, pn, hn = _layer(y, l, depth, 0, rope_prompt, pkv, None, pool0, hg0,
                                mk16, mv16, weights, tiles=PROMPT_TILES)
        pmk.append(mk.reshape(bp, -1, X_HEADS, X_HD)); pmv.append(mv.reshape(bp, -1, X_HEADS, X_HD))
        ppool.append(pn); phg.append(hn)
    y_prompt = y
    kv_shape = (depth, bp, x_prompt.shape[1], ATT_HEADS, ATT_DV)
    pk, pv = pkv[0].reshape(kv_shape), pkv[1].reshape(kv_shape)

    y = x_sample
    skv = None
    spool, shg = [], []
    past = (cache_attn_k.reshape(depth, bs, p0_sample * ATT_HEADS, ATT_DV),
            cache_attn_v.reshape(depth, bs, p0_sample * ATT_HEADS, ATT_DV))
    mem_slots = cache_mem_k.shape[2]
    mem_k = cache_mem_k.reshape(depth, bs, mem_slots * D_MODEL // LANES, LANES)
    mem_v = cache_mem_v.reshape(depth, bs, mem_slots * D_MODEL // LANES, LANES)
    for l in range(depth):
        pool0 = jnp.pad(state_pool[l], ((0, 0), (POOL_HIST - POOL_STATE, 0), (0, 0)))
        y, skv, pn, hn = _layer(y, l, depth, p0_sample, rope_sample, skv, past, pool0, state_hgrn[l],
                                mem_k, mem_v, weights, tiles=SAMPLE_TILES)
        spool.append(pn); shg.append(hn)
    y_sample = y
    kv_shape = (depth, bs, x_sample.shape[1], ATT_HEADS, ATT_DV)
    sk, sv = skv[0].reshape(kv_shape), skv[1].reshape(kv_shape)

    return (y_prompt, y_sample,
            pk, pv, jnp.stack(pmk), jnp.stack(pmv),
            jnp.stack(ppool), jnp.stack(phg),
            sk, sv, jnp.stack(spool), jnp.stack(shg))
```

```python
import functools
import math

import jax
import jax.numpy as jnp
from jax import lax
from jax.experimental import pallas as pl
from jax.experimental.pallas import tpu as pltpu

F32 = jnp.float32
BF16 = jnp.bfloat16

D_MODEL = 1024
CHUNK = 64
EPS = 1e-6
MASK_VALUE = -1e30
POOL_WINDOWS = (2, 4, 8, 16)
POOL_STATE = 15
POOL_HIST = 16
POOL_PAD = 32
SUBLANES = 8
ATT_HEADS = 4
ATT_DK = 64
ATT_DV = 128
ROT_DIMS = 16
ROPE_THETA = 500000.0
HG_HEADS = 4
HG_DK = 64
HG_DV = 128
X_HEADS = 4
X_HD = 256
D_FF = 2816
SEG = 512
N_SEG_A = 7
N_SEG_G = 6
LANES = 128
VMEM_LIMIT = 56 * 1024 * 1024
SCORE_BOUND_MAX = 20.0
KEY_TILES_PER_ITER = 4


def _cparams(sem):
    return pltpu.CompilerParams(dimension_semantics=sem, vmem_limit_bytes=VMEM_LIMIT)


def _rms(x, g):
    ms = jnp.mean(x * x, axis=-1, keepdims=True)
    return x * lax.rsqrt(ms + EPS) * g


def _sigmoid(x):
    return 0.5 * jnp.tanh(0.5 * x) + 0.5


def _silu(x):
    return x * _sigmoid(x)


def _dot(a, b):
    return jnp.dot(a, b, preferred_element_type=F32)


def _dot_nt(a, b):
    return lax.dot_general(a, b, (((1,), (1,)), ((), ())), preferred_element_type=F32)


def _dot_tn(a, b):
    return lax.dot_general(a, b, (((0,), (0,)), ((), ())), preferred_element_type=F32)


def _shr(x, pow2):
    return lax.shift_right_logical(x, jnp.int32(int(math.log2(pow2))))


def _ffn_kernel(x_ref, g_ref, wi_ref, wo_ref, o_ref, *, tf):
    x = x_ref[...]
    h = _rms(x, g_ref[...]).astype(BF16)
    acc = None
    for c in range(D_FF // tf):
        a = _dot(h, wi_ref[:, pl.ds(c * tf, tf)])
        b = _dot(h, wi_ref[:, pl.ds(D_FF + c * tf, tf)])
        part = _dot((_silu(a) * b).astype(BF16), wo_ref[pl.ds(c * tf, tf), :])
        acc = part if acc is None else acc + part
    o_ref[...] = x + 0.5 * acc


def _resident(shape, layer=None):
    if layer is None:
        return pl.BlockSpec(shape, lambda *_: (0,) * len(shape), pipeline_mode=pl.Buffered(1))
    return pl.BlockSpec((None,) + tuple(shape), lambda *_: (layer,) + (0,) * len(shape),
                        pipeline_mode=pl.Buffered(1))


def _ffn_half(x, g, w_i, w_o, *, layer, tm, tf):
    n = x.shape[0]
    return pl.pallas_call(
        functools.partial(_ffn_kernel, tf=tf),
        grid=(n // tm,),
        in_specs=[
            pl.BlockSpec((tm, D_MODEL), lambda i: (i, 0)),
            _resident((1, D_MODEL), layer),
            _resident((D_MODEL, 2 * D_FF), layer),
            _resident((D_FF, D_MODEL), layer),
        ],
        out_specs=pl.BlockSpec((tm, D_MODEL), lambda i: (i, 0)),
        out_shape=jax.ShapeDtypeStruct((n, D_MODEL), F32),
        compiler_params=_cparams(("parallel",)),
        name="ffn_half",
    )(x, g, w_i, w_o)


def _inproj_kernel(x_ref, g_ref, w_ref, gm_ref, qn_ref, kn_ref, cos_ref, sa_ref, sb_ref, *rest, tq, chained):
    rest = rest[2:] if chained else rest
    pa_ref, pb_ref, kf_ref, vf_ref, pg_ref, qkv_ref = rest[:6]
    qt_ref, vt_ref = rest[6:8] if tq is not None else (None, None)

    def store_head_rows(ref, y):
        for h in range(ATT_HEADS):
            ref[pl.ds(h, y.shape[0], stride=ATT_HEADS), :] = y[:, h * ATT_DV:(h + 1) * ATT_DV]

    def store_transposed(ref, y):
        if ref is None:
            return
        yt = y.T.astype(BF16)
        for h in range(ATT_HEADS):
            for c in range(y.shape[0] // tq):
                ref[h, c] = yt[h * ATT_DV:(h + 1) * ATT_DV, c * tq:(c + 1) * tq]

    h = _rms(x_ref[...], g_ref[...]).astype(BF16)

    def proj(seg):
        return _dot(h, w_ref[:, pl.ds(seg * SEG, SEG)])

    rep = SEG // LANES
    cos = jnp.concatenate([cos_ref[...]] * rep, axis=1)
    sa = jnp.concatenate([sa_ref[...]] * rep, axis=1)
    sb = jnp.concatenate([sb_ref[...]] * rep, axis=1)

    def norm_rope(y, gain):
        ms = _dot((y * y).astype(BF16), gm_ref[...])
        yn = y * lax.rsqrt(ms + EPS) * gain
        half = ROT_DIMS // 2
        return yn * cos + pltpu.roll(yn, half, 1) * sa + pltpu.roll(yn, SEG - half, 1) * sb

    pa_ref[:, pl.ds(0, SEG)] = proj(0)

    q = norm_rope(proj(1), qn_ref[...]) * (ATT_DK ** -0.5)
    qkv_ref[:, pl.ds(0, SEG)] = q.astype(BF16)
    store_transposed(qt_ref, q)

    k = norm_rope(proj(2), kn_ref[...])
    store_head_rows(kf_ref, k)
    qkv_ref[:, pl.ds(SEG, SEG)] = k.astype(BF16)

    v = proj(3)
    store_head_rows(vf_ref, v)
    qkv_ref[:, pl.ds(2 * SEG, SEG)] = v.astype(BF16)
    store_transposed(vt_ref, v)

    pa_ref[:, pl.ds(SEG, SEG)] = proj(4)
    pb_ref[:, pl.ds(0, SEG)] = proj(5).astype(BF16)
    pb_ref[:, pl.ds(SEG, SEG)] = proj(6).astype(BF16)
    for s in range(N_SEG_G):
        pg_ref[:, pl.ds(s * SEG, SEG)] = (0.5 * proj(N_SEG_A + s)).astype(BF16)


def _in_proj(x, g, w, gm, qn, kn, cos, sa, sb, kv_rows, *, tm, seq_len, tq, layer, depth):
    n = x.shape[0]
    tab_blocks = cos.shape[0] // tm
    nseg = N_SEG_A + N_SEG_G
    tab_spec = pl.BlockSpec((tm, LANES), lambda i: (i % tab_blocks, 0))
    kv_spec = pl.BlockSpec((tm * ATT_HEADS, ATT_DV), lambda i: (layer * (n // tm) + i, 0))
    kv_shape = jax.ShapeDtypeStruct((depth * n * ATT_HEADS, ATT_DV), F32)

    def rows(width):
        return pl.BlockSpec((tm, width), lambda i: (i, 0))

    out_specs = [rows(2 * SEG), rows(2 * SEG), kv_spec, kv_spec, rows(N_SEG_G * SEG), rows(3 * SEG)]
    out_shape = [
        jax.ShapeDtypeStruct((n, 2 * SEG), F32),
        jax.ShapeDtypeStruct((n, 2 * SEG), BF16),
        kv_shape, kv_shape,
        jax.ShapeDtypeStruct((n, N_SEG_G * SEG), BF16),
        jax.ShapeDtypeStruct((n, 3 * SEG), BF16),
    ]
    chained = kv_rows is not None
    n_in = 9
    any_spec = pl.BlockSpec(memory_space=pl.ANY)
    extra_in = list(kv_rows) if chained else []
    aliases = {n_in: 2, n_in + 1: 3} if chained else {}
    if tq is not None:
        per_seq = seq_len // tm
        t_spec = pl.BlockSpec((None, ATT_HEADS, tm // tq, ATT_DV, tq),
                              lambda i: (i // per_seq, 0, i % per_seq, 0, 0))
        t_shape = jax.ShapeDtypeStruct((n // seq_len, ATT_HEADS, seq_len // tq, ATT_DV, tq), BF16)
        out_specs += [t_spec, t_spec]
        out_shape += [t_shape, t_shape]
    return pl.pallas_call(
        functools.partial(_inproj_kernel, tq=tq, chained=chained),
        grid=(n // tm,),
        in_specs=[
            rows(D_MODEL),
            _resident((1, D_MODEL), layer),
            _resident((D_MODEL, nseg * SEG), layer),
            _resident((SEG, SEG)),
            _resident((1, SEG), layer), _resident((1, SEG), layer), tab_spec, tab_spec, tab_spec,
        ] + [any_spec] * len(extra_in),
        out_specs=out_specs,
        out_shape=out_shape,
        input_output_aliases=aliases,
        compiler_params=_cparams(("parallel",)),
        name="in_proj",
    )(x, g, w, gm, qn, kn, cos, sa, sb, *extra_in)


def _rope_tables(p0, t):
    half = ROT_DIMS // 2
    inv = ROPE_THETA ** (-jnp.arange(half, dtype=F32) / half)
    in_head = jnp.arange(LANES) % ATT_DK
    pos = p0 + jnp.arange(t, dtype=jnp.int32)
    ang = pos.astype(F32)[:, None] * inv[in_head % half][None, :]
    cos, sin = jnp.cos(ang), jnp.sin(ang)
    lower = (in_head < half)[None, :]
    upper = jnp.logical_and(in_head >= half, in_head < ROT_DIMS)[None, :]
    return (jnp.where(in_head[None, :] < ROT_DIMS, cos, 1.0),
            jnp.where(upper, sin, 0.0),
            jnp.where(lower, -sin, 0.0))


def _stack_maps(q):
    lane = lax.broadcasted_iota(jnp.int32, q.shape, 1)
    zero = jnp.zeros_like(q)
    return jnp.concatenate([jnp.where(lane < ATT_DK, q, zero), jnp.where(lane >= ATT_DK, q, zero)], axis=0)


def _attn_finish(acc, l, lam, gsub, post_scale, tq):
    o = acc[:tq] / l[:tq] - lam * (acc[tq:] / l[tq:])
    return _rms(o, gsub) * post_scale


def _attn_prompt_kernel(lam_ref, gsub_ref, q_ref, k_ref, v_ref, o_ref, acc_sc, m_sc, l_sc, *, tq, post_scale):
    i = pl.program_id(2)
    qbd = _stack_maps(q_ref[...])
    m_sc[...] = jnp.full_like(m_sc, MASK_VALUE)
    l_sc[...] = jnp.zeros_like(l_sc)
    acc_sc[...] = jnp.zeros_like(acc_sc)

    def step(j, masked):
        off = pl.multiple_of(j * tq, tq)
        kj = k_ref[pl.ds(off, tq), :]
        vj = v_ref[pl.ds(off, tq), :]
        s = _dot_nt(qbd, kj)
        if masked:
            qi = lax.broadcasted_iota(jnp.int32, s.shape, 0) & (tq - 1)
            ki = lax.broadcasted_iota(jnp.int32, s.shape, 1)
            s = jnp.where(_shr(ki, CHUNK) <= _shr(qi, CHUNK), s, MASK_VALUE)
        m_prev = m_sc[...]
        m_new = jnp.maximum(m_prev, jnp.max(s, axis=-1, keepdims=True))
        alpha = jnp.exp(m_prev - m_new)
        p = jnp.exp(s - m_new)
        l_sc[...] = alpha * l_sc[...] + jnp.sum(p, axis=-1, keepdims=True)
        acc_sc[...] = alpha * acc_sc[...] + _dot(p.astype(BF16), vj)
        m_sc[...] = m_new

    def body(j, carry):
        step(j, False)
        return carry

    lax.fori_loop(0, i, body, 0)
    step(i, True)
    o_ref[...] = _attn_finish(acc_sc[...], l_sc[...], lam_ref[...], gsub_ref[...], post_scale, tq)


def _attn_prompt(qkv, lam, gsub, post_scale, *, tq):
    b, t, _ = qkv.shape
    kern = functools.partial(_attn_prompt_kernel, tq=tq, post_scale=post_scale)
    vec_spec = pl.BlockSpec((1, ATT_DV), lambda b_, h, i: (0, 0))
    return pl.pallas_call(
        kern,
        grid=(b, ATT_HEADS, t // tq),
        in_specs=[
            vec_spec, vec_spec,
            pl.BlockSpec((None, tq, ATT_DV), lambda b_, h, i: (b_, i, h)),
            pl.BlockSpec((None, t, ATT_DV), lambda b_, h, i: (b_, 0, ATT_HEADS + h)),
            pl.BlockSpec((None, t, ATT_DV), lambda b_, h, i: (b_, 0, 2 * ATT_HEADS + h)),
        ],
        out_specs=pl.BlockSpec((None, tq, ATT_DV), lambda b_, h, i: (b_, i, h)),
        out_shape=jax.ShapeDtypeStruct((b, t, ATT_HEADS * ATT_DV), F32),
        scratch_shapes=[pltpu.VMEM((2 * tq, ATT_DV), F32), pltpu.VMEM((2 * tq, 1), F32),
                        pltpu.VMEM((2 * tq, 1), F32)],
        compiler_params=_cparams(("parallel", "parallel", "arbitrary")),
        name="diff_attn_prompt",
    )(lam, gsub, qkv, qkv, qkv)


def _attn_bounded_kernel(lam_ref, gsub_ref, qt_ref, k_ref, vt_ref, o_ref, acc_sc, l_sc, p_sc, *, tq, post_scale):
    i = pl.program_id(2)
    qt = qt_ref[...].astype(F32)
    row = lax.broadcasted_iota(jnp.int32, qt.shape, 0)
    qbd = jnp.concatenate([jnp.where(row < ATT_DK, qt, 0.0), jnp.where(row >= ATT_DK, qt, 0.0)],
                          axis=1).astype(BF16)
    acc_sc[...] = jnp.zeros_like(acc_sc)
    l_sc[...] = jnp.zeros_like(l_sc)

    def weights_of(j, masked):
        off = pl.multiple_of(j * tq, tq)
        s = _dot(k_ref[pl.ds(off, tq), :], qbd)
        if masked:
            ki = lax.broadcasted_iota(jnp.int32, s.shape, 0)
            qi = lax.broadcasted_iota(jnp.int32, s.shape, 1) & (tq - 1)
            s = jnp.where(_shr(ki, CHUNK) <= _shr(qi, CHUNK), s, MASK_VALUE)
        p = jnp.exp(s)
        return jnp.sum(p.reshape(tq // SUBLANES, SUBLANES, 2 * tq), axis=0), p.astype(BF16)

    def total(xs):
        return functools.reduce(lambda x, y: x + y, xs)

    group = KEY_TILES_PER_ITER
    n_groups = i // group

    def stage_scores(g, slot):
        parts = []
        for c in range(group):
            lsum, p16 = weights_of(g * group + c, False)
            p_sc[slot, c] = p16
            parts.append(lsum)
        l_sc[...] += total(parts)

    def stage_values(g, slot):
        acc_sc[...] += total([_dot(vt_ref[g * group + c], p_sc[slot, c]) for c in range(group)])

    @pl.when(n_groups > 0)
    def _():
        stage_scores(0, 0)

    def pipelined(n, carry):
        slot = n & 1
        stage_values(n - 1, 1 - slot)
        stage_scores(n, slot)
        return carry

    lax.fori_loop(1, n_groups, pipelined, 0)

    @pl.when(n_groups > 0)
    def _():
        stage_values(n_groups - 1, (n_groups - 1) & 1)

    rem = i % group
    for r in range(group):
        @pl.when(rem == r)
        def _(r=r):
            parts = [weights_of(i - r + c, c == r) for c in range(r + 1)]
            l_sc[...] += total([p_[0] for p_ in parts])
            acc_sc[...] += total([_dot(vt_ref[i - r + c], parts[c][1]) for c in range(r + 1)])
    l = jnp.sum(l_sc[...], axis=0, keepdims=True)
    acc = acc_sc[...]
    ot = acc[:, :tq] / l[:, :tq] - lam_ref[...] * (acc[:, tq:] / l[:, tq:])
    ms = jnp.mean(ot * ot, axis=0, keepdims=True)
    yt = ot * lax.rsqrt(ms + EPS) * (gsub_ref[...] * post_scale)
    o_ref[...] = yt.T


def _attn_bounded(qkv, q5, v5, lam, gsub, post_scale):
    b, t, _ = qkv.shape
    nt, tq = q5.shape[2], q5.shape[4]
    kern = functools.partial(_attn_bounded_kernel, tq=tq, post_scale=post_scale)
    return pl.pallas_call(
        kern,
        grid=(b, ATT_HEADS, nt),
        in_specs=[
            pl.BlockSpec((1, 1), lambda b_, h, i: (0, 0)),
            pl.BlockSpec((ATT_DV, 1), lambda b_, h, i: (0, 0)),
            pl.BlockSpec((None, None, None, ATT_DV, tq), lambda b_, h, i: (b_, h, i, 0, 0)),
            pl.BlockSpec((None, t, ATT_DV), lambda b_, h, i: (b_, 0, ATT_HEADS + h)),
            pl.BlockSpec((None, None, nt, ATT_DV, tq), lambda b_, h, i: (b_, h, 0, 0, 0)),
        ],
        out_specs=pl.BlockSpec((None, tq, ATT_DV), lambda b_, h, i: (b_, i, h)),
        out_shape=jax.ShapeDtypeStruct((b, t, ATT_HEADS * ATT_DV), F32),
        scratch_shapes=[pltpu.VMEM((ATT_DV, 2 * tq), F32), pltpu.VMEM((SUBLANES, 2 * tq), F32),
                        pltpu.VMEM((2, KEY_TILES_PER_ITER, tq, 2 * tq), BF16)],
        compiler_params=_cparams(("parallel", "parallel", "arbitrary")),
        name="diff_attn_bounded",
    )(lam[:, :1], gsub.reshape(ATT_DV, 1), q5, qkv, v5)


def _attn_sample_kernel(lam_ref, gsub_ref, q_ref, kn_ref, vn_ref, kp_ref, vp_ref, o_ref, *, tq, tp, post_scale):
    for h in range(ATT_HEADS):
        cols = pl.ds(h * ATT_DV, ATT_DV)
        head_rows = pl.ds(h, tp, stride=ATT_HEADS)
        qbd = _stack_maps(q_ref[:, cols])
        sp = _dot_nt(qbd, kp_ref[head_rows, :].astype(BF16))
        sn = _dot_nt(qbd, kn_ref[:, cols])
        m = jnp.maximum(jnp.max(sp, axis=-1, keepdims=True), jnp.max(sn, axis=-1, keepdims=True))
        pp = jnp.exp(sp - m)
        pn = jnp.exp(sn - m)
        l = jnp.sum(pp, axis=-1, keepdims=True) + jnp.sum(pn, axis=-1, keepdims=True)
        acc = (_dot(pp.astype(BF16), vp_ref[head_rows, :].astype(BF16))
               + _dot(pn.astype(BF16), vn_ref[:, cols]))
        o_ref[:, cols] = _attn_finish(acc, l, lam_ref[...], gsub_ref[...], post_scale, tq)


def _attn_sample(qkv, kp, vp, lam, gsub, post_scale, *, layer):
    b, t, _ = qkv.shape
    tp = kp.shape[2] // ATT_HEADS
    kern = functools.partial(_attn_sample_kernel, tq=t, tp=tp, post_scale=post_scale)
    vec_spec = pl.BlockSpec((1, ATT_DV), lambda b_: (0, 0))
    past_spec = pl.BlockSpec((None, None, tp * ATT_HEADS, ATT_DV), lambda b_: (layer, b_, 0, 0))
    width = ATT_HEADS * ATT_DV
    return pl.pallas_call(
        kern,
        grid=(b,),
        in_specs=[
            vec_spec, vec_spec,
            pl.BlockSpec((None, t, width), lambda b_: (b_, 0, 0)),
            pl.BlockSpec((None, t, width), lambda b_: (b_, 0, 1)),
            pl.BlockSpec((None, t, width), lambda b_: (b_, 0, 2)),
            past_spec, past_spec,
        ],
        out_specs=pl.BlockSpec((None, t, width), lambda b_: (b_, 0, 0)),
        out_shape=jax.ShapeDtypeStruct((b, t, width), F32),
        compiler_params=_cparams(("parallel",)),
        name="diff_attn_sample",
    )(lam, gsub, qkv, qkv, qkv, kp, vp)


HG_W = HG_HEADS * HG_DK
HG_V = HG_HEADS * HG_DV
B_PAD = 8


def _level_reference(b_sc, w):
    if w >= 4:
        pieces = []
        for p in range(CHUNK // (2 * w)):
            row = b_sc[pl.ds(B_PAD + p * 2 * w + w, 1), :]
            pieces.append(jnp.broadcast_to(row, (2 * w, HG_W)))
        return jnp.concatenate(pieces, axis=0)
    t = lax.broadcasted_iota(jnp.int32, (CHUNK, HG_W), 0)
    phase = t & (2 * w - 1)
    r = None
    for ph in range(2 * w):
        shifted = b_sc[pl.ds(B_PAD + w - ph, CHUNK), :]
        r = shifted if r is None else jnp.where(phase == ph, shifted, r)
    return r


def _hgrn_chunk(hq, hf, hi, lb, st_sc, b_sc, masks):
    tril, bd_k, bd_v, bd_s = masks
    f = lb + (1.0 - lb) * jax.nn.sigmoid(hf)
    g = jnp.log(f)
    kk = 1.0 - f
    q = _silu(hq) * (HG_DK ** -0.5)
    g0 = g.astype(BF16)
    r1 = g - g0.astype(F32)
    g1 = r1.astype(BF16)
    g2 = (r1 - g1.astype(F32)).astype(BF16)
    b = _dot(tril, g0) + _dot(tril, g1) + _dot(tril, g2)
    b_sc[pl.ds(B_PAD, CHUNK), :] = b

    t_idx = lax.broadcasted_iota(jnp.int32, (CHUNK, HG_W), 0)
    s_idx = lax.broadcasted_iota(jnp.int32, (CHUNK, HG_W), 1) & (CHUNK - 1)
    zero = jnp.zeros((CHUNK, HG_W), F32)

    def block_diag_k(x):
        return jnp.where(bd_k, jnp.concatenate([x] * HG_HEADS, axis=0), jnp.zeros((), F32)).astype(BF16)

    a = jnp.where(t_idx == s_idx, _dot_nt(q.astype(BF16), block_diag_k(kk)), zero)
    w = CHUNK // 2
    while w >= 1:
        r = _level_reference(b_sc, w)
        upper = (t_idx & w) != 0
        e = jnp.exp(jnp.where(upper, b - r, r - b))
        ql = jnp.where(upper, q * e, zero)
        kl = jnp.where(upper, zero, kk * e)
        same_pair = _shr(t_idx, 2 * w) == _shr(s_idx, 2 * w)
        a = a + jnp.where(same_pair, _dot_nt(ql.astype(BF16), block_diag_k(kl)), zero)
        w //= 2

    b_last = jnp.broadcast_to(b_sc[pl.ds(B_PAD + CHUNK - 1, 1), :], (CHUNK, HG_W))
    qb = q * jnp.exp(b)
    kd = kk * jnp.exp(b_last - b)
    v16 = hi
    vbd = jnp.where(bd_v, jnp.concatenate([hi.astype(F32)] * HG_HEADS, axis=0), jnp.zeros((), F32)).astype(BF16)
    st = st_sc[...]
    o = _dot(a.astype(BF16), vbd) + _dot_nt(qb.astype(BF16), st.astype(BF16))
    upd = _dot_tn(v16, kd.astype(BF16))
    decay = jnp.exp(b_sc[pl.ds(B_PAD + CHUNK - 1, 1), :])
    st_sc[...] = st * decay + jnp.where(bd_s, upd, jnp.zeros((), F32))
    return o


def _mix_kernel(x_ref, u_ref, hqf_ref, hi_ref, hg_ref, g_ref, ya_ref, st0_ref, pool0_ref,
                lb_ref, pw_ref, ps_ref, hgn_ref, wup_ref, wua_ref, wuh_ref, wo_ref,
                xo_ref, stn_ref, pooln_ref,
                st_sc, b_sc, ext_sc, lvl_sc, yh_sc, *, tm, p0):
    it = pl.program_id(1)
    top = POOL_PAD + tm

    @pl.when(it == 0)
    def _():
        st_sc[...] = st0_ref[...]
        ext_sc[pl.ds(0, POOL_PAD - POOL_HIST), :] = jnp.zeros((POOL_PAD - POOL_HIST, SEG), F32)
        ext_sc[pl.ds(POOL_PAD - POOL_HIST, POOL_HIST), :] = pool0_ref[...]
        b_sc[...] = jnp.zeros_like(b_sc)

    ext_sc[pl.ds(POOL_PAD, tm), :] = u_ref[...]
    row = lax.broadcasted_iota(jnp.int32, (tm, LANES), 0)
    seen = (p0 + 1 + it * tm + row).astype(F32)
    cols = []
    for gi, w in enumerate(POOL_WINDOWS):
        sl = pl.ds(gi * LANES, LANES)
        stages = int(math.log2(w))

        def read(lo, n, level):
            if level == 0:
                return ext_sc[pl.ds(lo, n), sl]
            return lvl_sc[(level - 1) % 2, pl.ds(lo, n), :]

        win = None
        for k in range(1, stages + 1):
            lo = POOL_PAD if k == stages else SUBLANES * k
            win = read(lo, top - lo, k - 1) + read(lo - (1 << (k - 1)), top - lo, k - 1)
            if k < stages:
                lvl_sc[(k - 1) % 2, pl.ds(lo, top - lo), :] = win
        dlt = win / jnp.minimum(seen, float(w)) - u_ref[:, sl]
        cols.append(_dot(dlt.astype(BF16), pw_ref[gi]))
    y_pool = jnp.concatenate(cols, axis=1) * ps_ref[...]
    ext_sc[pl.ds(POOL_PAD - POOL_HIST, POOL_HIST), :] = ext_sc[pl.ds(top - POOL_HIST, POOL_HIST), :]

    tri_r = lax.broadcasted_iota(jnp.int32, (CHUNK, CHUNK), 0)
    tri_c = lax.broadcasted_iota(jnp.int32, (CHUNK, CHUNK), 1)
    tril = (tri_c <= tri_r).astype(BF16)
    rk = lax.broadcasted_iota(jnp.int32, (HG_W, HG_W), 0)
    ck = lax.broadcasted_iota(jnp.int32, (HG_W, HG_W), 1)
    bd_k = _shr(rk, CHUNK) == _shr(ck, HG_DK)
    rv = lax.broadcasted_iota(jnp.int32, (HG_W, HG_V), 0)
    cv = lax.broadcasted_iota(jnp.int32, (HG_W, HG_V), 1)
    bd_v = _shr(rv, CHUNK) == _shr(cv, HG_DV)
    rs = lax.broadcasted_iota(jnp.int32, (HG_V, HG_W), 0)
    cs = lax.broadcasted_iota(jnp.int32, (HG_V, HG_W), 1)
    bd_s = _shr(rs, HG_DV) == _shr(cs, HG_DK)
    masks = (tril, bd_k, bd_v, bd_s)
    lb = lb_ref[...]
    for c in range(tm // CHUNK):
        rows = pl.ds(c * CHUNK, CHUNK)
        o = _hgrn_chunk(hqf_ref[rows, pl.ds(0, HG_W)], hqf_ref[rows, pl.ds(HG_W, HG_W)],
                        hi_ref[rows, :], lb, st_sc, b_sc, masks)
        yh_sc[rows, :] = o
    oh = yh_sc[...]
    hg = hg_ref[...].astype(F32)
    heads = []
    for h in range(HG_HEADS):
        sl = slice(h * HG_DV, (h + 1) * HG_DV)
        heads.append(_rms(oh[:, sl], hgn_ref[...]) * _silu(hg[:, sl]))
    y_hg = jnp.concatenate(heads, axis=1)

    th = jnp.tanh(g_ref[...].astype(F32))
    merged = 0.5 * ((th[:, :D_MODEL] + 1.0) * _dot(y_pool.astype(BF16), wup_ref[...])
                    + (th[:, D_MODEL:2 * D_MODEL] + 1.0) * _dot(ya_ref[...].astype(BF16), wua_ref[...])
                    + (th[:, 2 * D_MODEL:] + 1.0) * _dot(y_hg.astype(BF16), wuh_ref[...]))
    xo_ref[...] = x_ref[...] + _dot(merged.astype(BF16), wo_ref[...])

    @pl.when(it == pl.num_programs(1) - 1)
    def _():
        stn_ref[...] = st_sc[...]
        pooln_ref[...] = ext_sc[pl.ds(POOL_PAD - POOL_HIST, POOL_HIST), :]


def _mix_out(x, proj_a, proj_b, gates, y_att, st0, pool0, lb, pool_w, pool_scale, hg_outn,
             w_up_pool, w_up_att, w_up_hgrn, w_out, *, layer, tm, p0):
    b, t, _ = x.shape
    kern = functools.partial(_mix_kernel, tm=tm, p0=p0)

    def rows(width, col):
        return pl.BlockSpec((None, tm, width), lambda b_, i: (b_, i, col))

    def const(shape):
        return _resident(shape, layer)

    def per_batch(shape):
        return pl.BlockSpec((None,) + shape, lambda b_, i: (b_, 0, 0))

    return pl.pallas_call(
        kern,
        grid=(b, t // tm),
        in_specs=[
            rows(D_MODEL, 0),
            rows(SEG, 0),
            rows(SEG, 1),
            rows(SEG, 0),
            rows(SEG, 1),
            rows(3 * D_MODEL, 0),
            rows(SEG, 0),
            per_batch((HG_V, HG_W)),
            per_batch((POOL_HIST, SEG)),
            const((1, HG_W)), const((4, LANES, LANES)), const((1, SEG)), const((1, HG_DV)),
            const((SEG, D_MODEL)), const((SEG, D_MODEL)), const((SEG, D_MODEL)), const((D_MODEL, D_MODEL)),
        ],
        out_specs=[
            rows(D_MODEL, 0),
            per_batch((HG_V, HG_W)),
            per_batch((POOL_HIST, SEG)),
        ],
        out_shape=[
            jax.ShapeDtypeStruct((b, t, D_MODEL), F32),
            jax.ShapeDtypeStruct((b, HG_V, HG_W), F32),
            jax.ShapeDtypeStruct((b, POOL_HIST, SEG), F32),
        ],
        scratch_shapes=[
            pltpu.VMEM((HG_V, HG_W), F32),
            pltpu.VMEM((CHUNK + 2 * B_PAD, HG_W), F32),
            pltpu.VMEM((POOL_PAD + tm, SEG), F32),
            pltpu.VMEM((2, POOL_PAD + tm, LANES), F32),
            pltpu.VMEM((tm, HG_V), F32),
        ],
        compiler_params=_cparams(("parallel", "arbitrary")),
        name="mix_out",
    )(x, proj_a, proj_a, proj_b, proj_b, gates, y_att, st0, pool0,
      lb, pool_w, pool_scale, hg_outn, w_up_pool, w_up_att, w_up_hgrn, w_out)


def _state_to_block_diag(s):
    b = s.shape[0]
    st = jnp.swapaxes(s, 2, 3)
    eye = jnp.eye(HG_HEADS, dtype=s.dtype)
    return jnp.einsum('bhed,hg->bhegd', st, eye).reshape(b, HG_V, HG_W)


def _block_diag_to_state(st):
    b = st.shape[0]
    s5 = st.reshape(b, HG_HEADS, HG_DV, HG_HEADS, HG_DK)
    diag = jnp.stack([s5[:, h, :, h, :] for h in range(HG_HEADS)], axis=1)
    return jnp.swapaxes(diag, 2, 3)


def _head_rms(x, gain):
    return jnp.concatenate(
        [_rms(x[:, h * X_HD:(h + 1) * X_HD], gain) for h in range(X_HEADS)], axis=1)


def _memkv_kernel(m_ref, g_ref, w_ref, kn_ref, k_ref, v_ref, k16_ref, v16_ref):
    h = _rms(m_ref[...], g_ref[...]).astype(BF16)
    kv = _dot(h, w_ref[...])
    mk = _head_rms(kv[:, :D_MODEL], kn_ref[...])
    mv = kv[:, D_MODEL:]
    k_ref[...] = mk
    v_ref[...] = mv
    k16_ref[...] = mk.astype(BF16)
    v16_ref[...] = mv.astype(BF16)


def _memory_kv(mem, g, w_ckv, kn, *, layer):
    b, n, _ = mem.shape
    blk = pl.BlockSpec((None, n, D_MODEL), lambda b_: (b_, 0, 0))
    return pl.pallas_call(
        _memkv_kernel,
        grid=(b,),
        in_specs=[
            blk,
            _resident((1, D_MODEL), layer),
            _resident((D_MODEL, 2 * D_MODEL), layer),
            _resident((1, X_HD), layer),
        ],
        out_specs=[blk, blk, blk, blk],
        out_shape=[jax.ShapeDtypeStruct((b, n, D_MODEL), F32)] * 2
        + [jax.ShapeDtypeStruct((b, n, D_MODEL), BF16)] * 2,
        compiler_params=_cparams(("parallel",)),
        name="memory_kv",
    )(mem, g, w_ckv, kn)


def _cross_kernel(x_ref, g_ref, wq_ref, qn_ref, mk_ref, mv_ref, wo_ref, o_ref, *, cache_rows):
    x = x_ref[...]
    q = _head_rms(_dot(_rms(x, g_ref[...]).astype(BF16), wq_ref[...]), qn_ref[...])
    q = (q * (X_HD ** -0.5)).astype(BF16)
    outs = []
    for h in range(X_HEADS):
        sl = slice(h * X_HD, (h + 1) * X_HD)
        if cache_rows is None:
            mk, mv = mk_ref[:, sl], mv_ref[:, sl]
        else:
            parts = X_HD // LANES

            def head_of(ref):
                halves = [ref[pl.ds(parts * h + c, cache_rows, stride=parts * X_HEADS), :] for c in range(parts)]
                return jnp.concatenate(halves, axis=1).astype(BF16)

            mk, mv = head_of(mk_ref), head_of(mv_ref)
        s = _dot_nt(q[:, sl], mk)
        p = jnp.exp(s - jnp.max(s, axis=-1, keepdims=True))
        p = p / jnp.sum(p, axis=-1, keepdims=True)
        outs.append(_dot(p.astype(BF16), mv))
    o = jnp.concatenate(outs, axis=1)
    o_ref[...] = x + _dot(o.astype(BF16), wo_ref[...])


def _cross_attend(x, mk, mv, g, w_cq, qn, w_co, *, layer, tm):
    b, t, _ = x.shape
    if mk.ndim == 3:
        cache_rows = None
        mem = pl.BlockSpec((None, mk.shape[1], D_MODEL), lambda b_, i: (b_, 0, 0))
    else:
        cache_rows = mk.shape[2] * LANES // D_MODEL
        mem = pl.BlockSpec((None, None, mk.shape[2], LANES), lambda b_, i: (layer, b_, 0, 0))
    rows = pl.BlockSpec((None, tm, D_MODEL), lambda b_, i: (b_, i, 0))
    wsq = _resident((D_MODEL, D_MODEL), layer)
    return pl.pallas_call(
        functools.partial(_cross_kernel, cache_rows=cache_rows),
        grid=(b, t // tm),
        in_specs=[
            rows,
            _resident((1, D_MODEL), layer),
            wsq,
            _resident((1, X_HD), layer),
            mem, mem, wsq,
        ],
        out_specs=rows,
        out_shape=jax.ShapeDtypeStruct((b, t, D_MODEL), F32),
        compiler_params=_cparams(("parallel", "parallel")),
        name="cross_attn",
    )(x, g, w_cq, qn, mk, mv, w_co)


def _layer(x, l, depth, p0, rope, kv_rows, past_kv, pool_prev, hg_state, mk16, mv16, w, *, tiles):
    b, t, _ = x.shape
    n = b * t
    x2 = _ffn_half(x.reshape(n, D_MODEL), w['norm_ffn1'], w['w_ffn1_in'], w['w_ffn1_out'],
                   layer=l, tm=tiles['ffn'], tf=tiles['tf'])

    proj_a, proj_b, kf, vf, gates, qkv, *tiles_t = _in_proj(
        x2, w['norm_mix'], w['w_in'], w['group_mean'], w['att_qn'], w['att_kn'], *rope, kv_rows,
        tm=tiles['proj'], seq_len=t, tq=None if past_kv is not None else tiles['attn_bounded'],
        layer=l, depth=depth)

    lam_init = 0.8 - 0.6 * math.exp(-0.3 * l)
    lam = (jnp.exp(jnp.sum(w['lq1'][l] * w['lk1'][l])) - jnp.exp(jnp.sum(w['lq2'][l] * w['lk2'][l])) + lam_init)
    lam = jnp.full((1, ATT_DV), lam, F32)
    gsub = w['att_subln'][l].reshape(1, ATT_DV)
    qkv3 = qkv.reshape(b, t, 3 * SEG)
    if past_kv is None:
        bound = (ATT_DK ** 0.5) * jnp.max(jnp.abs(w['att_qn'][l])) * jnp.max(jnp.abs(w['att_kn'][l]))
        y_att = lax.cond(
            bound <= SCORE_BOUND_MAX,
            lambda a, q5, v5: _attn_bounded(a, q5, v5, lam, gsub, 1.0 - lam_init),
            lambda a, q5, v5: _attn_prompt(a, lam, gsub, 1.0 - lam_init, tq=tiles['attn']),
            qkv3, *tiles_t)
    else:
        y_att = _attn_sample(qkv3, past_kv[0], past_kv[1], lam, gsub, 1.0 - lam_init, layer=l)

    x3, st_new, pool_new = _mix_out(
        x2.reshape(b, t, D_MODEL), proj_a.reshape(b, t, 2 * SEG), proj_b.reshape(b, t, 2 * SEG),
        gates.reshape(b, t, N_SEG_G * SEG),
        y_att, _state_to_block_diag(hg_state), pool_prev, w['lb'],
        w['pool_w'], w['pool_scale'], w['hg_outn'], w['w_up_pool'], w['w_up_att'], w['w_up_hgrn'], w['w_out'],
        layer=l, tm=tiles['mix'], p0=p0)

    x4 = _cross_attend(x3, mk16, mv16, w['norm_cross'], w['w_cq'], w['cross_qn'], w['w_co'],
                       layer=l, tm=tiles['cross'])
    x5 = _ffn_half(x4.reshape(n, D_MODEL), w['norm_ffn2'], w['w_ffn2_in'], w['w_ffn2_out'],
                   layer=l, tm=tiles['ffn'], tf=tiles['tf'])
    return (x5.reshape(b, t, D_MODEL), (kf, vf), pool_new[:, POOL_HIST - POOL_STATE:],
            _block_diag_to_state(st_new))


PROMPT_TILES = dict(ffn=1024, tf=256, proj=512, attn=256, attn_bounded=512, mix=512, cross=1024)
SAMPLE_TILES = dict(ffn=512, tf=256, proj=512, attn=64, mix=64, cross=64)


def kernel(x_prompt, x_sample, cache_attn_k, cache_attn_v, cache_mem_k, cache_mem_v, state_pool, state_hgrn, mem_prompt, norm_ffn1, w_ffn1_in, w_ffn1_out, norm_mix, w_in, pool_w, pool_scale, att_q_norm, att_k_norm, lambda_q1, lambda_k1, lambda_q2, lambda_k2, att_subln, hgrn_lower, hgrn_out_norm, w_up_pool, w_up_att, w_up_hgrn, w_out, norm_cross, norm_mem, w_cq, w_ckv, cross_q_norm, cross_k_norm, w_co, norm_ffn2, w_ffn2_in, w_ffn2_out):
    depth = w_in.shape[0]
    bp = x_prompt.shape[0]
    bs = x_sample.shape[0]
    p0_sample = cache_attn_k.shape[2]

    lp = jax.nn.softmax(hgrn_lower.astype(F32), axis=0)
    lbs = jnp.cumsum(lp, axis=0) - lp[0:1]

    gidx = jnp.arange(SEG) // ATT_DK
    group_mean = ((gidx[:, None] == gidx[None, :]).astype(F32) / ATT_DK).astype(BF16)

    def row(p):
        return p.reshape(depth, 1, p.shape[-1])

    weights = dict(
        norm_ffn1=row(norm_ffn1), w_ffn1_in=w_ffn1_in.astype(BF16), w_ffn1_out=w_ffn1_out.astype(BF16),
        norm_mix=row(norm_mix), w_in=w_in.astype(BF16), group_mean=group_mean,
        att_qn=row(jnp.tile(att_q_norm, (1, SEG // ATT_DK))), att_kn=row(jnp.tile(att_k_norm, (1, SEG // ATT_DK))),
        lq1=lambda_q1.astype(F32), lk1=lambda_k1.astype(F32), lq2=lambda_q2.astype(F32), lk2=lambda_k2.astype(F32),
        att_subln=att_subln,
        lb=row(lbs), pool_w=pool_w.astype(BF16), pool_scale=row(pool_scale), hg_outn=row(hgrn_out_norm),
        w_up_pool=w_up_pool.astype(BF16), w_up_att=w_up_att.astype(BF16),
        w_up_hgrn=w_up_hgrn.astype(BF16), w_out=w_out.astype(BF16),
        norm_cross=row(norm_cross), w_cq=w_cq.astype(BF16), cross_qn=row(cross_q_norm), w_co=w_co.astype(BF16),
        norm_ffn2=row(norm_ffn2), w_ffn2_in=w_ffn2_in.astype(BF16), w_ffn2_out=w_ffn2_out.astype(BF16),
    )
    norm_mem_r, w_ckv16, cross_kn_r = row(norm_mem), w_ckv.astype(BF16), row(cross_k_norm)

    def rope_for(p0, t, tm):
        tabs = _rope_tables(p0, t)
        if t < tm:
            tabs = tuple(jnp.concatenate([a] * (tm // t), axis=0) for a in tabs)
        return tabs

    rope_prompt = rope_for(0, x_prompt.shape[1], PROMPT_TILES['proj'])
    rope_sample = rope_for(p0_sample, x_sample.shape[1], SAMPLE_TILES['proj'])

    y = x_prompt
    pkv = None
    pmk, pmv, ppool, phg = [], [], [], []
    for l in range(depth):
        mk, mv, mk16, mv16 = _memory_kv(mem_prompt, norm_mem_r, w_ckv16, cross_kn_r, layer=l)
        pool0 = jnp.zeros((bp, POOL_HIST, SEG), F32)
        hg0 = jnp.zeros((bp, HG_HEADS, HG_DK, HG_DV), F32)
        y, pkv, pn, hn = _layer(y, l, depth, 0, rope_prompt, pkv, None, pool0, hg0,
                                mk16, mv16, weights, tiles=PROMPT_TILES)
        pmk.append(mk.reshape(bp, -1, X_HEADS, X_HD)); pmv.append(mv.reshape(bp, -1, X_HEADS, X_HD))
        ppool.append(pn); phg.append(hn)
    y_prompt = y
    kv_shape = (depth, bp, x_prompt.shape[1], ATT_HEADS, ATT_DV)
    pk, pv = pkv[0].reshape(kv_shape), pkv[1].reshape(kv_shape)

    y = x_sample
    skv = None
    spool, shg = [], []
    past = (cache_attn_k.reshape(depth, bs, p0_sample * ATT_HEADS, ATT_DV),
            cache_attn_v.reshape(depth, bs, p0_sample * ATT_HEADS, ATT_DV))
    mem_slots = cache_mem_k.shape[2]
    mem_k = cache_mem_k.reshape(depth, bs, mem_slots * D_MODEL // LANES, LANES)
    mem_v = cache_mem_v.reshape(depth, bs, mem_slots * D_MODEL // LANES, LANES)
    for l in range(depth):
        pool0 = jnp.pad(state_pool[l], ((0, 0), (POOL_HIST - POOL_STATE, 0), (0, 0)))
        y, skv, pn, hn = _layer(y, l, depth, p0_sample, rope_sample, skv, past, pool0, state_hgrn[l],
                                mem_k, mem_v, weights, tiles=SAMPLE_TILES)
        spool.append(pn); shg.append(hn)
    y_sample = y
    kv_shape = (depth, bs, x_sample.shape[1], ATT_HEADS, ATT_DV)
    sk, sv = skv[0].reshape(kv_shape), skv[1].reshape(kv_shape)

    return (y_prompt, y_sample,
            pk, pv, jnp.stack(pmk), jnp.stack(pmv),
            jnp.stack(ppool), jnp.stack(phg),
            sk, sv, jnp.stack(spool), jnp.stack(shg))
```

```python
import functools
import math

import jax
import jax.numpy as jnp
from jax import lax
from jax.experimental import pallas as pl
from jax.experimental.pallas import tpu as pltpu

F32 = jnp.float32
BF16 = jnp.bfloat16

D_MODEL = 1024
CHUNK = 64
EPS = 1e-6
MASK_VALUE = -1e30
POOL_WINDOWS = (2, 4, 8, 16)
POOL_STATE = 15
POOL_HIST = 16
POOL_PAD = 32
SUBLANES = 8
ATT_HEADS = 4
ATT_DK = 64
ATT_DV = 128
ROT_DIMS = 16
ROPE_THETA = 500000.0
HG_HEADS = 4
HG_DK = 64
HG_DV = 128
X_HEADS = 4
X_HD = 256
D_FF = 2816
SEG = 512
N_SEG_A = 7
N_SEG_G = 6
LANES = 128
VMEM_LIMIT = 56 * 1024 * 1024
SCORE_BOUND_MAX = 20.0
KEY_TILES_PER_ITER = 4


def _cparams(sem):
    return pltpu.CompilerParams(dimension_semantics=sem, vmem_limit_bytes=VMEM_LIMIT)


def _rms(x, g):
    ms = jnp.mean(x * x, axis=-1, keepdims=True)
    return x * lax.rsqrt(ms + EPS) * g


def _sigmoid(x):
    return 0.5 * jnp.tanh(0.5 * x) + 0.5


def _silu(x):
    return x * _sigmoid(x)


def _dot(a, b):
    return jnp.dot(a, b, preferred_element_type=F32)


def _dot_nt(a, b):
    return lax.dot_general(a, b, (((1,), (1,)), ((), ())), preferred_element_type=F32)


def _dot_tn(a, b):
    return lax.dot_general(a, b, (((0,), (0,)), ((), ())), preferred_element_type=F32)


def _shr(x, pow2):
    return lax.shift_right_logical(x, jnp.int32(int(math.log2(pow2))))


def _ffn_kernel(x_ref, g_ref, wi_ref, wo_ref, o_ref, *, tf):
    x = x_ref[...]
    h = _rms(x, g_ref[...]).astype(BF16)
    acc = None
    for c in range(D_FF // tf):
        a = _dot(h, wi_ref[:, pl.ds(c * tf, tf)])
        b = _dot(h, wi_ref[:, pl.ds(D_FF + c * tf, tf)])
        part = _dot((_silu(a) * b).astype(BF16), wo_ref[pl.ds(c * tf, tf), :])
        acc = part if acc is None else acc + part
    o_ref[...] = x + 0.5 * acc


def _resident(shape, layer=None):
    if layer is None:
        return pl.BlockSpec(shape, lambda *_: (0,) * len(shape), pipeline_mode=pl.Buffered(1))
    return pl.BlockSpec((None,) + tuple(shape), lambda *_: (layer,) + (0,) * len(shape),
                        pipeline_mode=pl.Buffered(1))


def _ffn_half(x, g, w_i, w_o, *, layer, tm, tf):
    n = x.shape[0]
    return pl.pallas_call(
        functools.partial(_ffn_kernel, tf=tf),
        grid=(n // tm,),
        in_specs=[
            pl.BlockSpec((tm, D_MODEL), lambda i: (i, 0)),
            _resident((1, D_MODEL), layer),
            _resident((D_MODEL, 2 * D_FF), layer),
            _resident((D_FF, D_MODEL), layer),
        ],
        out_specs=pl.BlockSpec((tm, D_MODEL), lambda i: (i, 0)),
        out_shape=jax.ShapeDtypeStruct((n, D_MODEL), F32),
        compiler_params=_cparams(("parallel",)),
        name="ffn_half",
    )(x, g, w_i, w_o)


def _inproj_kernel(x_ref, g_ref, w_ref, gm_ref, qn_ref, kn_ref, cos_ref, sa_ref, sb_ref, *rest, tq, chained):
    rest = rest[2:] if chained else rest
    pa_ref, pb_ref, kf_ref, vf_ref, pg_ref, qkv_ref = rest[:6]
    qt_ref, vt_ref = rest[6:8] if tq is not None else (None, None)

    def store_head_rows(ref, y):
        for h in range(ATT_HEADS):
            ref[pl.ds(h, y.shape[0], stride=ATT_HEADS), :] = y[:, h * ATT_DV:(h + 1) * ATT_DV]

    def store_transposed(ref, y):
        if ref is None:
            return
        yt = y.T.astype(BF16)
        for h in range(ATT_HEADS):
            for c in range(y.shape[0] // tq):
                ref[h, c] = yt[h * ATT_DV:(h + 1) * ATT_DV, c * tq:(c + 1) * tq]

    h = _rms(x_ref[...], g_ref[...]).astype(BF16)

    def proj(seg):
        return _dot(h, w_ref[:, pl.ds(seg * SEG, SEG)])

    rep = SEG // LANES
    cos = jnp.concatenate([cos_ref[...]] * rep, axis=1)
    sa = jnp.concatenate([sa_ref[...]] * rep, axis=1)
    sb = jnp.concatenate([sb_ref[...]] * rep, axis=1)

    def norm_rope(y, gain):
        ms = _dot((y * y).astype(BF16), gm_ref[...])
        yn = y * lax.rsqrt(ms + EPS) * gain
        half = ROT_DIMS // 2
        return yn * cos + pltpu.roll(yn, half, 1) * sa + pltpu.roll(yn, SEG - half, 1) * sb

    pa_ref[:, pl.ds(0, SEG)] = proj(0)

    q = norm_rope(proj(1), qn_ref[...]) * (ATT_DK ** -0.5)
    qkv_ref[:, pl.ds(0, SEG)] = q.astype(BF16)
    store_transposed(qt_ref, q)

    k = norm_rope(proj(2), kn_ref[...])
    store_head_rows(kf_ref, k)
    qkv_ref[:, pl.ds(SEG, SEG)] = k.astype(BF16)

    v = proj(3)
    store_head_rows(vf_ref, v)
    qkv_ref[:, pl.ds(2 * SEG, SEG)] = v.astype(BF16)
    store_transposed(vt_ref, v)

    pa_ref[:, pl.ds(SEG, SEG)] = proj(4)
    pb_ref[:, pl.ds(0, SEG)] = proj(5).astype(BF16)
    pb_ref[:, pl.ds(SEG, SEG)] = proj(6).astype(BF16)
    for s in range(N_SEG_G):
        pg_ref[:, pl.ds(s * SEG, SEG)] = (0.5 * proj(N_SEG_A + s)).astype(BF16)


def _in_proj(x, g, w, gm, qn, kn, cos, sa, sb, kv_rows, *, tm, seq_len, tq, layer, depth):
    n = x.shape[0]
    tab_blocks = cos.shape[0] // tm
    nseg = N_SEG_A + N_SEG_G
    tab_spec = pl.BlockSpec((tm, LANES), lambda i: (i % tab_blocks, 0))
    kv_spec = pl.BlockSpec((tm * ATT_HEADS, ATT_DV), lambda i: (layer * (n // tm) + i, 0))
    kv_shape = jax.ShapeDtypeStruct((depth * n * ATT_HEADS, ATT_DV), F32)

    def rows(width):
        return pl.BlockSpec((tm, width), lambda i: (i, 0))

    out_specs = [rows(2 * SEG), rows(2 * SEG), kv_spec, kv_spec, rows(N_SEG_G * SEG), rows(3 * SEG)]
    out_shape = [
        jax.ShapeDtypeStruct((n, 2 * SEG), F32),
        jax.ShapeDtypeStruct((n, 2 * SEG), BF16),
        kv_shape, kv_shape,
        jax.ShapeDtypeStruct((n, N_SEG_G * SEG), BF16),
        jax.ShapeDtypeStruct((n, 3 * SEG), BF16),
    ]
    chained = kv_rows is not None
    n_in = 9
    any_spec = pl.BlockSpec(memory_space=pl.ANY)
    extra_in = list(kv_rows) if chained else []
    aliases = {n_in: 2, n_in + 1: 3} if chained else {}
    if tq is not None:
        per_seq = seq_len // tm
        t_spec = pl.BlockSpec((None, ATT_HEADS, tm // tq, ATT_DV, tq),
                              lambda i: (i // per_seq, 0, i % per_seq, 0, 0))
        t_shape = jax.ShapeDtypeStruct((n // seq_len, ATT_HEADS, seq_len // tq, ATT_DV, tq), BF16)
        out_specs += [t_spec, t_spec]
        out_shape += [t_shape, t_shape]
    return pl.pallas_call(
        functools.partial(_inproj_kernel, tq=tq, chained=chained),
        grid=(n // tm,),
        in_specs=[
            rows(D_MODEL),
            _resident((1, D_MODEL), layer),
            _resident((D_MODEL, nseg * SEG), layer),
            _resident((SEG, SEG)),
            _resident((1, SEG), layer), _resident((1, SEG), layer), tab_spec, tab_spec, tab_spec,
        ] + [any_spec] * len(extra_in),
        out_specs=out_specs,
        out_shape=out_shape,
        input_output_aliases=aliases,
        compiler_params=_cparams(("parallel",)),
        name="in_proj",
    )(x, g, w, gm, qn, kn, cos, sa, sb, *extra_in)


def _rope_tables(p0, t):
    half = ROT_DIMS // 2
    inv = ROPE_THETA ** (-jnp.arange(half, dtype=F32) / half)
    in_head = jnp.arange(LANES) % ATT_DK
    pos = p0 + jnp.arange(t, dtype=jnp.int32)
    ang = pos.astype(F32)[:, None] * inv[in_head % half][None, :]
    cos, sin = jnp.cos(ang), jnp.sin(ang)
    lower = (in_head < half)[None, :]
    upper = jnp.logical_and(in_head >= half, in_head < ROT_DIMS)[None, :]
    return (jnp.where(in_head[None, :] < ROT_DIMS, cos, 1.0),
            jnp.where(upper, sin, 0.0),
            jnp.where(lower, -sin, 0.0))


def _stack_maps(q):
    lane = lax.broadcasted_iota(jnp.int32, q.shape, 1)
    zero = jnp.zeros_like(q)
    return jnp.concatenate([jnp.where(lane < ATT_DK, q, zero), jnp.where(lane >= ATT_DK, q, zero)], axis=0)


def _attn_finish(acc, l, lam, gsub, post_scale, tq):
    o = acc[:tq] / l[:tq] - lam * (acc[tq:] / l[tq:])
    return _rms(o, gsub) * post_scale


def _attn_prompt_kernel(lam_ref, gsub_ref, q_ref, k_ref, v_ref, o_ref, acc_sc, m_sc, l_sc, *, tq, post_scale):
    i = pl.program_id(2)
    qbd = _stack_maps(q_ref[...])
    m_sc[...] = jnp.full_like(m_sc, MASK_VALUE)
    l_sc[...] = jnp.zeros_like(l_sc)
    acc_sc[...] = jnp.zeros_like(acc_sc)

    def step(j, masked):
        off = pl.multiple_of(j * tq, tq)
        kj = k_ref[pl.ds(off, tq), :]
        vj = v_ref[pl.ds(off, tq), :]
        s = _dot_nt(qbd, kj)
        if masked:
            qi = lax.broadcasted_iota(jnp.int32, s.shape, 0) & (tq - 1)
            ki = lax.broadcasted_iota(jnp.int32, s.shape, 1)
            s = jnp.where(_shr(ki, CHUNK) <= _shr(qi, CHUNK), s, MASK_VALUE)
        m_prev = m_sc[...]
        m_new = jnp.maximum(m_prev, jnp.max(s, axis=-1, keepdims=True))
        alpha = jnp.exp(m_prev - m_new)
        p = jnp.exp(s - m_new)
        l_sc[...] = alpha * l_sc[...] + jnp.sum(p, axis=-1, keepdims=True)
        acc_sc[...] = alpha * acc_sc[...] + _dot(p.astype(BF16), vj)
        m_sc[...] = m_new

    def body(j, carry):
        step(j, False)
        return carry

    lax.fori_loop(0, i, body, 0)
    step(i, True)
    o_ref[...] = _attn_finish(acc_sc[...], l_sc[...], lam_ref[...], gsub_ref[...], post_scale, tq)


def _attn_prompt(qkv, lam, gsub, post_scale, *, tq):
    b, t, _ = qkv.shape
    kern = functools.partial(_attn_prompt_kernel, tq=tq, post_scale=post_scale)
    vec_spec = pl.BlockSpec((1, ATT_DV), lambda b_, h, i: (0, 0))
    return pl.pallas_call(
        kern,
        grid=(b, ATT_HEADS, t // tq),
        in_specs=[
            vec_spec, vec_spec,
            pl.BlockSpec((None, tq, ATT_DV), lambda b_, h, i: (b_, i, h)),
            pl.BlockSpec((None, t, ATT_DV), lambda b_, h, i: (b_, 0, ATT_HEADS + h)),
            pl.BlockSpec((None, t, ATT_DV), lambda b_, h, i: (b_, 0, 2 * ATT_HEADS + h)),
        ],
        out_specs=pl.BlockSpec((None, tq, ATT_DV), lambda b_, h, i: (b_, i, h)),
        out_shape=jax.ShapeDtypeStruct((b, t, ATT_HEADS * ATT_DV), F32),
        scratch_shapes=[pltpu.VMEM((2 * tq, ATT_DV), F32), pltpu.VMEM((2 * tq, 1), F32),
                        pltpu.VMEM((2 * tq, 1), F32)],
        compiler_params=_cparams(("parallel", "parallel", "arbitrary")),
        name="diff_attn_prompt",
    )(lam, gsub, qkv, qkv, qkv)


def _attn_bounded_kernel(lam_ref, gsub_ref, qt_ref, k_ref, vt_ref, o_ref, acc_sc, l_sc, p_sc, *, tq, post_scale):
    i = pl.program_id(2)
    qt = qt_ref[...].astype(F32)
    row = lax.broadcasted_iota(jnp.int32, qt.shape, 0)
    qbd = jnp.concatenate([jnp.where(row < ATT_DK, qt, 0.0), jnp.where(row >= ATT_DK, qt, 0.0)],
                          axis=1).astype(BF16)
    acc_sc[...] = jnp.zeros_like(acc_sc)
    l_sc[...] = jnp.zeros_like(l_sc)

    def weights_of(j, masked):
        off = pl.multiple_of(j * tq, tq)
        s = _dot(k_ref[pl.ds(off, tq), :], qbd)
        if masked:
            ki = lax.broadcasted_iota(jnp.int32, s.shape, 0)
            qi = lax.broadcasted_iota(jnp.int32, s.shape, 1) & (tq - 1)
            s = jnp.where(_shr(ki, CHUNK) <= _shr(qi, CHUNK), s, MASK_VALUE)
        p = jnp.exp(s)
        return jnp.sum(p.reshape(tq // SUBLANES, SUBLANES, 2 * tq), axis=0), p.astype(BF16)

    def total(xs):
        return functools.reduce(lambda x, y: x + y, xs)

    group = KEY_TILES_PER_ITER
    n_groups = i // group

    def stage_scores(g, slot):
        parts = []
        for c in range(group):
            lsum, p16 = weights_of(g * group + c, False)
            p_sc[slot, c] = p16
            parts.append(lsum)
        l_sc[...] += total(parts)

    def stage_values(g, slot):
        acc_sc[...] += total([_dot(vt_ref[g * group + c], p_sc[slot, c]) for c in range(group)])

    @pl.when(n_groups > 0)
    def _():
        stage_scores(0, 0)

    def pipelined(n, carry):
        slot = n & 1
        stage_values(n - 1, 1 - slot)
        stage_scores(n, slot)
        return carry

    lax.fori_loop(1, n_groups, pipelined, 0)

    rem = i % group
    for r in range(group):
        for drain in (False, True):
            @pl.when(jnp.logical_and(rem == r, (n_groups > 0) == drain))
            def _(r=r, drain=drain):
                parts = [weights_of(i - r + c, c == r) for c in range(r + 1)]
                if drain:
                    stage_values(n_groups - 1, (n_groups - 1) & 1)
                l_sc[...] += total([p_[0] for p_ in parts])
                acc_sc[...] += total([_dot(vt_ref[i - r + c], parts[c][1]) for c in range(r + 1)])
    l = jnp.sum(l_sc[...], axis=0, keepdims=True)
    acc = acc_sc[...]
    ot = acc[:, :tq] / l[:, :tq] - lam_ref[...] * (acc[:, tq:] / l[:, tq:])
    ms = jnp.mean(ot * ot, axis=0, keepdims=True)
    yt = ot * lax.rsqrt(ms + EPS) * (gsub_ref[...] * post_scale)
    o_ref[...] = yt.T


def _attn_bounded(qkv, q5, v5, lam, gsub, post_scale):
    b, t, _ = qkv.shape
    nt, tq = q5.shape[2], q5.shape[4]
    kern = functools.partial(_attn_bounded_kernel, tq=tq, post_scale=post_scale)
    return pl.pallas_call(
        kern,
        grid=(b, ATT_HEADS, nt),
        in_specs=[
            pl.BlockSpec((1, 1), lambda b_, h, i: (0, 0)),
            pl.BlockSpec((ATT_DV, 1), lambda b_, h, i: (0, 0)),
            pl.BlockSpec((None, None, None, ATT_DV, tq), lambda b_, h, i: (b_, h, i, 0, 0)),
            pl.BlockSpec((None, t, ATT_DV), lambda b_, h, i: (b_, 0, ATT_HEADS + h)),
            pl.BlockSpec((None, None, nt, ATT_DV, tq), lambda b_, h, i: (b_, h, 0, 0, 0)),
        ],
        out_specs=pl.BlockSpec((None, tq, ATT_DV), lambda b_, h, i: (b_, i, h)),
        out_shape=jax.ShapeDtypeStruct((b, t, ATT_HEADS * ATT_DV), F32),
        scratch_shapes=[pltpu.VMEM((ATT_DV, 2 * tq), F32), pltpu.VMEM((SUBLANES, 2 * tq), F32),
                        pltpu.VMEM((2, KEY_TILES_PER_ITER, tq, 2 * tq), BF16)],
        compiler_params=_cparams(("parallel", "parallel", "arbitrary")),
        name="diff_attn_bounded",
    )(lam[:, :1], gsub.reshape(ATT_DV, 1), q5, qkv, v5)


def _attn_sample_kernel(lam_ref, gsub_ref, q_ref, kn_ref, vn_ref, kp_ref, vp_ref, o_ref, *, tq, tp, post_scale):
    for h in range(ATT_HEADS):
        cols = pl.ds(h * ATT_DV, ATT_DV)
        head_rows = pl.ds(h, tp, stride=ATT_HEADS)
        qbd = _stack_maps(q_ref[:, cols])
        sp = _dot_nt(qbd, kp_ref[head_rows, :].astype(BF16))
        sn = _dot_nt(qbd, kn_ref[:, cols])
        m = jnp.maximum(jnp.max(sp, axis=-1, keepdims=True), jnp.max(sn, axis=-1, keepdims=True))
        pp = jnp.exp(sp - m)
        pn = jnp.exp(sn - m)
        l = jnp.sum(pp, axis=-1, keepdims=True) + jnp.sum(pn, axis=-1, keepdims=True)
        acc = (_dot(pp.astype(BF16), vp_ref[head_rows, :].astype(BF16))
               + _dot(pn.astype(BF16), vn_ref[:, cols]))
        o_ref[:, cols] = _attn_finish(acc, l, lam_ref[...], gsub_ref[...], post_scale, tq)


def _attn_sample(qkv, kp, vp, lam, gsub, post_scale, *, layer):
    b, t, _ = qkv.shape
    tp = kp.shape[2] // ATT_HEADS
    kern = functools.partial(_attn_sample_kernel, tq=t, tp=tp, post_scale=post_scale)
    vec_spec = pl.BlockSpec((1, ATT_DV), lambda b_: (0, 0))
    past_spec = pl.BlockSpec((None, None, tp * ATT_HEADS, ATT_DV), lambda b_: (layer, b_, 0, 0))
    width = ATT_HEADS * ATT_DV
    return pl.pallas_call(
        kern,
        grid=(b,),
        in_specs=[
            vec_spec, vec_spec,
            pl.BlockSpec((None, t, width), lambda b_: (b_, 0, 0)),
            pl.BlockSpec((None, t, width), lambda b_: (b_, 0, 1)),
            pl.BlockSpec((None, t, width), lambda b_: (b_, 0, 2)),
            past_spec, past_spec,
        ],
        out_specs=pl.BlockSpec((None, t, width), lambda b_: (b_, 0, 0)),
        out_shape=jax.ShapeDtypeStruct((b, t, width), F32),
        compiler_params=_cparams(("parallel",)),
        name="diff_attn_sample",
    )(lam, gsub, qkv, qkv, qkv, kp, vp)


HG_W = HG_HEADS * HG_DK
HG_V = HG_HEADS * HG_DV
B_PAD = 8


def _level_reference(b_sc, w):
    if w >= 4:
        pieces = []
        for p in range(CHUNK // (2 * w)):
            row = b_sc[pl.ds(B_PAD + p * 2 * w + w, 1), :]
            pieces.append(jnp.broadcast_to(row, (2 * w, HG_W)))
        return jnp.concatenate(pieces, axis=0)
    t = lax.broadcasted_iota(jnp.int32, (CHUNK, HG_W), 0)
    phase = t & (2 * w - 1)
    r = None
    for ph in range(2 * w):
        shifted = b_sc[pl.ds(B_PAD + w - ph, CHUNK), :]
        r = shifted if r is None else jnp.where(phase == ph, shifted, r)
    return r


def _hgrn_chunk(hq, hf, hi, lb, st_sc, b_sc, masks):
    tril, bd_k, bd_v, bd_s = masks
    f = lb + (1.0 - lb) * jax.nn.sigmoid(hf)
    g = jnp.log(f)
    kk = 1.0 - f
    q = _silu(hq) * (HG_DK ** -0.5)
    g0 = g.astype(BF16)
    r1 = g - g0.astype(F32)
    g1 = r1.astype(BF16)
    g2 = (r1 - g1.astype(F32)).astype(BF16)
    b = _dot(tril, g0) + _dot(tril, g1) + _dot(tril, g2)
    b_sc[pl.ds(B_PAD, CHUNK), :] = b

    t_idx = lax.broadcasted_iota(jnp.int32, (CHUNK, HG_W), 0)
    s_idx = lax.broadcasted_iota(jnp.int32, (CHUNK, HG_W), 1) & (CHUNK - 1)
    zero = jnp.zeros((CHUNK, HG_W), F32)

    def block_diag_k(x):
        return jnp.where(bd_k, jnp.concatenate([x] * HG_HEADS, axis=0), jnp.zeros((), F32)).astype(BF16)

    a = jnp.where(t_idx == s_idx, _dot_nt(q.astype(BF16), block_diag_k(kk)), zero)
    w = CHUNK // 2
    while w >= 1:
        r = _level_reference(b_sc, w)
        upper = (t_idx & w) != 0
        e = jnp.exp(jnp.where(upper, b - r, r - b))
        ql = jnp.where(upper, q * e, zero)
        kl = jnp.where(upper, zero, kk * e)
        same_pair = _shr(t_idx, 2 * w) == _shr(s_idx, 2 * w)
        a = a + jnp.where(same_pair, _dot_nt(ql.astype(BF16), block_diag_k(kl)), zero)
        w //= 2

    b_last = jnp.broadcast_to(b_sc[pl.ds(B_PAD + CHUNK - 1, 1), :], (CHUNK, HG_W))
    qb = q * jnp.exp(b)
    kd = kk * jnp.exp(b_last - b)
    v16 = hi
    vbd = jnp.where(bd_v, jnp.concatenate([hi.astype(F32)] * HG_HEADS, axis=0), jnp.zeros((), F32)).astype(BF16)
    st = st_sc[...]
    o = _dot(a.astype(BF16), vbd) + _dot_nt(qb.astype(BF16), st.astype(BF16))
    upd = _dot_tn(v16, kd.astype(BF16))
    decay = jnp.exp(b_sc[pl.ds(B_PAD + CHUNK - 1, 1), :])
    st_sc[...] = st * decay + jnp.where(bd_s, upd, jnp.zeros((), F32))
    return o


def _mix_kernel(x_ref, u_ref, hqf_ref, hi_ref, hg_ref, g_ref, ya_ref, st0_ref, pool0_ref,
                lb_ref, pw_ref, ps_ref, hgn_ref, wup_ref, wua_ref, wuh_ref, wo_ref,
                xo_ref, stn_ref, pooln_ref,
                st_sc, b_sc, ext_sc, lvl_sc, yh_sc, *, tm, p0):
    it = pl.program_id(1)
    top = POOL_PAD + tm

    @pl.when(it == 0)
    def _():
        st_sc[...] = st0_ref[...]
        ext_sc[pl.ds(0, POOL_PAD - POOL_HIST), :] = jnp.zeros((POOL_PAD - POOL_HIST, SEG), F32)
        ext_sc[pl.ds(POOL_PAD - POOL_HIST, POOL_HIST), :] = pool0_ref[...]
        b_sc[...] = jnp.zeros_like(b_sc)

    ext_sc[pl.ds(POOL_PAD, tm), :] = u_ref[...]
    row = lax.broadcasted_iota(jnp.int32, (tm, LANES), 0)
    seen = (p0 + 1 + it * tm + row).astype(F32)
    cols = []
    for gi, w in enumerate(POOL_WINDOWS):
        sl = pl.ds(gi * LANES, LANES)
        stages = int(math.log2(w))

        def read(lo, n, level):
            if level == 0:
                return ext_sc[pl.ds(lo, n), sl]
            return lvl_sc[(level - 1) % 2, pl.ds(lo, n), :]

        win = None
        for k in range(1, stages + 1):
            lo = POOL_PAD if k == stages else SUBLANES * k
            win = read(lo, top - lo, k - 1) + read(lo - (1 << (k - 1)), top - lo, k - 1)
            if k < stages:
                lvl_sc[(k - 1) % 2, pl.ds(lo, top - lo), :] = win
        dlt = win / jnp.minimum(seen, float(w)) - u_ref[:, sl]
        cols.append(_dot(dlt.astype(BF16), pw_ref[gi]))
    y_pool = jnp.concatenate(cols, axis=1) * ps_ref[...]
    ext_sc[pl.ds(POOL_PAD - POOL_HIST, POOL_HIST), :] = ext_sc[pl.ds(top - POOL_HIST, POOL_HIST), :]

    tri_r = lax.broadcasted_iota(jnp.int32, (CHUNK, CHUNK), 0)
    tri_c = lax.broadcasted_iota(jnp.int32, (CHUNK, CHUNK), 1)
    tril = (tri_c <= tri_r).astype(BF16)
    rk = lax.broadcasted_iota(jnp.int32, (HG_W, HG_W), 0)
    ck = lax.broadcasted_iota(jnp.int32, (HG_W, HG_W), 1)
    bd_k = _shr(rk, CHUNK) == _shr(ck, HG_DK)
    rv = lax.broadcasted_iota(jnp.int32, (HG_W, HG_V), 0)
    cv = lax.broadcasted_iota(jnp.int32, (HG_W, HG_V), 1)
    bd_v = _shr(rv, CHUNK) == _shr(cv, HG_DV)
    rs = lax.broadcasted_iota(jnp.int32, (HG_V, HG_W), 0)
    cs = lax.broadcasted_iota(jnp.int32, (HG_V, HG_W), 1)
    bd_s = _shr(rs, HG_DV) == _shr(cs, HG_DK)
    masks = (tril, bd_k, bd_v, bd_s)
    lb = lb_ref[...]
    for c in range(tm // CHUNK):
        rows = pl.ds(c * CHUNK, CHUNK)
        o = _hgrn_chunk(hqf_ref[rows, pl.ds(0, HG_W)], hqf_ref[rows, pl.ds(HG_W, HG_W)],
                        hi_ref[rows, :], lb, st_sc, b_sc, masks)
        yh_sc[rows, :] = o
    oh = yh_sc[...]
    hg = hg_ref[...].astype(F32)
    heads = []
    for h in range(HG_HEADS):
        sl = slice(h * HG_DV, (h + 1) * HG_DV)
        heads.append(_rms(oh[:, sl], hgn_ref[...]) * _silu(hg[:, sl]))
    y_hg = jnp.concatenate(heads, axis=1)

    th = jnp.tanh(g_ref[...].astype(F32))
    merged = 0.5 * ((th[:, :D_MODEL] + 1.0) * _dot(y_pool.astype(BF16), wup_ref[...])
                    + (th[:, D_MODEL:2 * D_MODEL] + 1.0) * _dot(ya_ref[...].astype(BF16), wua_ref[...])
                    + (th[:, 2 * D_MODEL:] + 1.0) * _dot(y_hg.astype(BF16), wuh_ref[...]))
    xo_ref[...] = x_ref[...] + _dot(merged.astype(BF16), wo_ref[...])

    @pl.when(it == pl.num_programs(1) - 1)
    def _():
        stn_ref[...] = st_sc[...]
        pooln_ref[...] = ext_sc[pl.ds(POOL_PAD - POOL_HIST, POOL_HIST), :]


def _mix_out(x, proj_a, proj_b, gates, y_att, st0, pool0, lb, pool_w, pool_scale, hg_outn,
             w_up_pool, w_up_att, w_up_hgrn, w_out, *, layer, tm, p0):
    b, t, _ = x.shape
    kern = functools.partial(_mix_kernel, tm=tm, p0=p0)

    def rows(width, col):
        return pl.BlockSpec((None, tm, width), lambda b_, i: (b_, i, col))

    def const(shape):
        return _resident(shape, layer)

    def per_batch(shape):
        return pl.BlockSpec((None,) + shape, lambda b_, i: (b_, 0, 0))

    return pl.pallas_call(
        kern,
        grid=(b, t // tm),
        in_specs=[
            rows(D_MODEL, 0),
            rows(SEG, 0),
            rows(SEG, 1),
            rows(SEG, 0),
            rows(SEG, 1),
            rows(3 * D_MODEL, 0),
            rows(SEG, 0),
            per_batch((HG_V, HG_W)),
            per_batch((POOL_HIST, SEG)),
            const((1, HG_W)), const((4, LANES, LANES)), const((1, SEG)), const((1, HG_DV)),
            const((SEG, D_MODEL)), const((SEG, D_MODEL)), const((SEG, D_MODEL)), const((D_MODEL, D_MODEL)),
        ],
        out_specs=[
            rows(D_MODEL, 0),
            per_batch((HG_V, HG_W)),
            per_batch((POOL_HIST, SEG)),
        ],
        out_shape=[
            jax.ShapeDtypeStruct((b, t, D_MODEL), F32),
            jax.ShapeDtypeStruct((b, HG_V, HG_W), F32),
            jax.ShapeDtypeStruct((b, POOL_HIST, SEG), F32),
        ],
        scratch_shapes=[
            pltpu.VMEM((HG_V, HG_W), F32),
            pltpu.VMEM((CHUNK + 2 * B_PAD, HG_W), F32),
            pltpu.VMEM((POOL_PAD + tm, SEG), F32),
            pltpu.VMEM((2, POOL_PAD + tm, LANES), F32),
            pltpu.VMEM((tm, HG_V), F32),
        ],
        compiler_params=_cparams(("parallel", "arbitrary")),
        name="mix_out",
    )(x, proj_a, proj_a, proj_b, proj_b, gates, y_att, st0, pool0,
      lb, pool_w, pool_scale, hg_outn, w_up_pool, w_up_att, w_up_hgrn, w_out)


def _state_to_block_diag(s):
    b = s.shape[0]
    st = jnp.swapaxes(s, 2, 3)
    eye = jnp.eye(HG_HEADS, dtype=s.dtype)
    return jnp.einsum('bhed,hg->bhegd', st, eye).reshape(b, HG_V, HG_W)


def _block_diag_to_state(st):
    b = st.shape[0]
    s5 = st.reshape(b, HG_HEADS, HG_DV, HG_HEADS, HG_DK)
    diag = jnp.stack([s5[:, h, :, h, :] for h in range(HG_HEADS)], axis=1)
    return jnp.swapaxes(diag, 2, 3)


def _head_rms(x, gain):
    return jnp.concatenate(
        [_rms(x[:, h * X_HD:(h + 1) * X_HD], gain) for h in range(X_HEADS)], axis=1)


def _memkv_kernel(m_ref, g_ref, w_ref, kn_ref, k_ref, v_ref, k16_ref, v16_ref):
    h = _rms(m_ref[...], g_ref[...]).astype(BF16)
    kv = _dot(h, w_ref[...])
    mk = _head_rms(kv[:, :D_MODEL], kn_ref[...])
    mv = kv[:, D_MODEL:]
    k_ref[...] = mk
    v_ref[...] = mv
    k16_ref[...] = mk.astype(BF16)
    v16_ref[...] = mv.astype(BF16)


def _memory_kv(mem, g, w_ckv, kn, *, layer):
    b, n, _ = mem.shape
    blk = pl.BlockSpec((None, n, D_MODEL), lambda b_: (b_, 0, 0))
    return pl.pallas_call(
        _memkv_kernel,
        grid=(b,),
        in_specs=[
            blk,
            _resident((1, D_MODEL), layer),
            _resident((D_MODEL, 2 * D_MODEL), layer),
            _resident((1, X_HD), layer),
        ],
        out_specs=[blk, blk, blk, blk],
        out_shape=[jax.ShapeDtypeStruct((b, n, D_MODEL), F32)] * 2
        + [jax.ShapeDtypeStruct((b, n, D_MODEL), BF16)] * 2,
        compiler_params=_cparams(("parallel",)),
        name="memory_kv",
    )(mem, g, w_ckv, kn)


def _cross_kernel(x_ref, g_ref, wq_ref, qn_ref, mk_ref, mv_ref, wo_ref, o_ref, *, cache_rows):
    x = x_ref[...]
    q = _head_rms(_dot(_rms(x, g_ref[...]).astype(BF16), wq_ref[...]), qn_ref[...])
    q = (q * (X_HD ** -0.5)).astype(BF16)
    outs = []
    for h in range(X_HEADS):
        sl = slice(h * X_HD, (h + 1) * X_HD)
        if cache_rows is None:
            mk, mv = mk_ref[:, sl], mv_ref[:, sl]
        else:
            parts = X_HD // LANES

            def head_of(ref):
                halves = [ref[pl.ds(parts * h + c, cache_rows, stride=parts * X_HEADS), :] for c in range(parts)]
                return jnp.concatenate(halves, axis=1).astype(BF16)

            mk, mv = head_of(mk_ref), head_of(mv_ref)
        s = _dot_nt(q[:, sl], mk)
        p = jnp.exp(s - jnp.max(s, axis=-1, keepdims=True))
        p = p / jnp.sum(p, axis=-1, keepdims=True)
        outs.append(_dot(p.astype(BF16), mv))
    o = jnp.concatenate(outs, axis=1)
    o_ref[...] = x + _dot(o.astype(BF16), wo_ref[...])


def _cross_attend(x, mk, mv, g, w_cq, qn, w_co, *, layer, tm):
    b, t, _ = x.shape
    if mk.ndim == 3:
        cache_rows = None
        mem = pl.BlockSpec((None, mk.shape[1], D_MODEL), lambda b_, i: (b_, 0, 0))
    else:
        cache_rows = mk.shape[2] * LANES // D_MODEL
        mem = pl.BlockSpec((None, None, mk.shape[2], LANES), lambda b_, i: (layer, b_, 0, 0))
    rows = pl.BlockSpec((None, tm, D_MODEL), lambda b_, i: (b_, i, 0))
    wsq = _resident((D_MODEL, D_MODEL), layer)
    return pl.pallas_call(
        functools.partial(_cross_kernel, cache_rows=cache_rows),
        grid=(b, t // tm),
        in_specs=[
            rows,
            _resident((1, D_MODEL), layer),
            wsq,
            _resident((1, X_HD), layer),
            mem, mem, wsq,
        ],
        out_specs=rows,
        out_shape=jax.ShapeDtypeStruct((b, t, D_MODEL), F32),
        compiler_params=_cparams(("parallel", "parallel")),
        name="cross_attn",
    )(x, g, w_cq, qn, mk, mv, w_co)


def _layer(x, l, depth, p0, rope, kv_rows, past_kv, pool_prev, hg_state, mk16, mv16, w, *, tiles):
    b, t, _ = x.shape
    n = b * t
    x2 = _ffn_half(x.reshape(n, D_MODEL), w['norm_ffn1'], w['w_ffn1_in'], w['w_ffn1_out'],
                   layer=l, tm=tiles['ffn'], tf=tiles['tf'])

    proj_a, proj_b, kf, vf, gates, qkv, *tiles_t = _in_proj(
        x2, w['norm_mix'], w['w_in'], w['group_mean'], w['att_qn'], w['att_kn'], *rope, kv_rows,
        tm=tiles['proj'], seq_len=t, tq=None if past_kv is not None else tiles['attn_bounded'],
        layer=l, depth=depth)

    lam_init = 0.8 - 0.6 * math.exp(-0.3 * l)
    lam = (jnp.exp(jnp.sum(w['lq1'][l] * w['lk1'][l])) - jnp.exp(jnp.sum(w['lq2'][l] * w['lk2'][l])) + lam_init)
    lam = jnp.full((1, ATT_DV), lam, F32)
    gsub = w['att_subln'][l].reshape(1, ATT_DV)
    qkv3 = qkv.reshape(b, t, 3 * SEG)
    if past_kv is None:
        bound = (ATT_DK ** 0.5) * jnp.max(jnp.abs(w['att_qn'][l])) * jnp.max(jnp.abs(w['att_kn'][l]))
        y_att = lax.cond(
            bound <= SCORE_BOUND_MAX,
            lambda a, q5, v5: _attn_bounded(a, q5, v5, lam, gsub, 1.0 - lam_init),
            lambda a, q5, v5: _attn_prompt(a, lam, gsub, 1.0 - lam_init, tq=tiles['attn']),
            qkv3, *tiles_t)
    else:
        y_att = _attn_sample(qkv3, past_kv[0], past_kv[1], lam, gsub, 1.0 - lam_init, layer=l)

    x3, st_new, pool_new = _mix_out(
        x2.reshape(b, t, D_MODEL), proj_a.reshape(b, t, 2 * SEG), proj_b.reshape(b, t, 2 * SEG),
        gates.reshape(b, t, N_SEG_G * SEG),
        y_att, _state_to_block_diag(hg_state), pool_prev, w['lb'],
        w['pool_w'], w['pool_scale'], w['hg_outn'], w['w_up_pool'], w['w_up_att'], w['w_up_hgrn'], w['w_out'],
        layer=l, tm=tiles['mix'], p0=p0)

    x4 = _cross_attend(x3, mk16, mv16, w['norm_cross'], w['w_cq'], w['cross_qn'], w['w_co'],
                       layer=l, tm=tiles['cross'])
    x5 = _ffn_half(x4.reshape(n, D_MODEL), w['norm_ffn2'], w['w_ffn2_in'], w['w_ffn2_out'],
                   layer=l, tm=tiles['ffn'], tf=tiles['tf'])
    return (x5.reshape(b, t, D_MODEL), (kf, vf), pool_new[:, POOL_HIST - POOL_STATE:],
            _block_diag_to_state(st_new))


PROMPT_TILES = dict(ffn=1024, tf=256, proj=512, attn=256, attn_bounded=512, mix=512, cross=1024)
SAMPLE_TILES = dict(ffn=512, tf=256, proj=512, attn=64, mix=64, cross=64)


def kernel(x_prompt, x_sample, cache_attn_k, cache_attn_v, cache_mem_k, cache_mem_v, state_pool, state_hgrn, mem_prompt, norm_ffn1, w_ffn1_in, w_ffn1_out, norm_mix, w_in, pool_w, pool_scale, att_q_norm, att_k_norm, lambda_q1, lambda_k1, lambda_q2, lambda_k2, att_subln, hgrn_lower, hgrn_out_norm, w_up_pool, w_up_att, w_up_hgrn, w_out, norm_cross, norm_mem, w_cq, w_ckv, cross_q_norm, cross_k_norm, w_co, norm_ffn2, w_ffn2_in, w_ffn2_out):
    depth = w_in.shape[0]
    bp = x_prompt.shape[0]
    bs = x_sample.shape[0]
    p0_sample = cache_attn_k.shape[2]

    lp = jax.nn.softmax(hgrn_lower.astype(F32), axis=0)
    lbs = jnp.cumsum(lp, axis=0) - lp[0:1]

    gidx = jnp.arange(SEG) // ATT_DK
    group_mean = ((gidx[:, None] == gidx[None, :]).astype(F32) / ATT_DK).astype(BF16)

    def row(p):
        return p.reshape(depth, 1, p.shape[-1])

    weights = dict(
        norm_ffn1=row(norm_ffn1), w_ffn1_in=w_ffn1_in.astype(BF16), w_ffn1_out=w_ffn1_out.astype(BF16),
        norm_mix=row(norm_mix), w_in=w_in.astype(BF16), group_mean=group_mean,
        att_qn=row(jnp.tile(att_q_norm, (1, SEG // ATT_DK))), att_kn=row(jnp.tile(att_k_norm, (1, SEG // ATT_DK))),
        lq1=lambda_q1.astype(F32), lk1=lambda_k1.astype(F32), lq2=lambda_q2.astype(F32), lk2=lambda_k2.astype(F32),
        att_subln=att_subln,
        lb=row(lbs), pool_w=pool_w.astype(BF16), pool_scale=row(pool_scale), hg_outn=row(hgrn_out_norm),
        w_up_pool=w_up_pool.astype(BF16), w_up_att=w_up_att.astype(BF16),
        w_up_hgrn=w_up_hgrn.astype(BF16), w_out=w_out.astype(BF16),
        norm_cross=row(norm_cross), w_cq=w_cq.astype(BF16), cross_qn=row(cross_q_norm), w_co=w_co.astype(BF16),
        norm_ffn2=row(norm_ffn2), w_ffn2_in=w_ffn2_in.astype(BF16), w_ffn2_out=w_ffn2_out.astype(BF16),
    )
    norm_mem_r, w_ckv16, cross_kn_r = row(norm_mem), w_ckv.astype(BF16), row(cross_k_norm)

    def rope_for(p0, t, tm):
        tabs = _rope_tables(p0, t)
        if t < tm:
            tabs = tuple(jnp.concatenate([a] * (tm // t), axis=0) for a in tabs)
        return tabs

    rope_prompt = rope_for(0, x_prompt.shape[1], PROMPT_TILES['proj'])
    rope_sample = rope_for(p0_sample, x_sample.shape[1], SAMPLE_TILES['proj'])

    y = x_prompt
    pkv = None
    pmk, pmv, ppool, phg = [], [], [], []
    for l in range(depth):
        mk, mv, mk16, mv16 = _memory_kv(mem_prompt, norm_mem_r, w_ckv16, cross_kn_r, layer=l)
        pool0 = jnp.zeros((bp, POOL_HIST, SEG), F32)
        hg0 = jnp.zeros((bp, HG_HEADS, HG_DK, HG_DV), F32)
        y, pkv, pn, hn = _layer(y, l, depth, 0, rope_prompt, pkv, None, pool0, hg0,
                                mk16, mv16, weights, tiles=PROMPT_TILES)
        pmk.append(mk.reshape(bp, -1, X_HEADS, X_HD)); pmv.append(mv.reshape(bp, -1, X_HEADS, X_HD))
        ppool.append(pn); phg.append(hn)
    y_prompt = y
    kv_shape = (depth, bp, x_prompt.shape[1], ATT_HEADS, ATT_DV)
    pk, pv = pkv[0].reshape(kv_shape), pkv[1].reshape(kv_shape)

    y = x_sample
    skv = None
    spool, shg = [], []
    past = (cache_attn_k.reshape(depth, bs, p0_sample * ATT_HEADS, ATT_DV),
            cache_attn_v.reshape(depth, bs, p0_sample * ATT_HEADS, ATT_DV))
    mem_slots = cache_mem_k.shape[2]
    mem_k = cache_mem_k.reshape(depth, bs, mem_slots * D_MODEL // LANES, LANES)
    mem_v = cache_mem_v.reshape(depth, bs, mem_slots * D_MODEL // LANES, LANES)
    for l in range(depth):
        pool0 = jnp.pad(state_pool[l], ((0, 0), (POOL_HIST - POOL_STATE, 0), (0, 0)))
        y, skv, pn, hn = _layer(y, l, depth, p0_sample, rope_sample, skv, past, pool0, state_hgrn[l],
                                mem_k, mem_v, weights, tiles=SAMPLE_TILES)
        spool.append(pn); shg.append(hn)
    y_sample = y
    kv_shape = (depth, bs, x_sample.shape[1], ATT_HEADS, ATT_DV)
    sk, sv = skv[0].reshape(kv_shape), skv[1].reshape(kv_shape)

    return (y_prompt, y_sample,
            pk, pv, jnp.stack(pmk), jnp.stack(pmv),
            jnp.stack(ppool), jnp.stack(phg),
            sk, sv, jnp.stack(spool), jnp.stack(shg))
```

```python
import functools
import math

import jax
import jax.numpy as jnp
from jax import lax
from jax.experimental import pallas as pl
from jax.experimental.pallas import tpu as pltpu

F32 = jnp.float32
BF16 = jnp.bfloat16

D_MODEL = 1024
CHUNK = 64
EPS = 1e-6
MASK_VALUE = -1e30
POOL_WINDOWS = (2, 4, 8, 16)
POOL_STATE = 15
POOL_HIST = 16
POOL_PAD = 32
SUBLANES = 8
ATT_HEADS = 4
ATT_DK = 64
ATT_DV = 128
ROT_DIMS = 16
ROPE_THETA = 500000.0
HG_HEADS = 4
HG_DK = 64
HG_DV = 128
X_HEADS = 4
X_HD = 256
D_FF = 2816
SEG = 512
N_SEG_A = 7
N_SEG_G = 6
LANES = 128
VMEM_LIMIT = 56 * 1024 * 1024
SCORE_BOUND_MAX = 20.0
KEY_TILES_PER_ITER = 4


def _cparams(sem):
    return pltpu.CompilerParams(dimension_semantics=sem, vmem_limit_bytes=VMEM_LIMIT)


def _rms(x, g):
    ms = jnp.mean(x * x, axis=-1, keepdims=True)
    return x * lax.rsqrt(ms + EPS) * g


def _sigmoid(x):
    return 0.5 * jnp.tanh(0.5 * x) + 0.5


def _silu(x):
    return x * _sigmoid(x)


def _dot(a, b):
    return jnp.dot(a, b, preferred_element_type=F32)


def _dot_nt(a, b):
    return lax.dot_general(a, b, (((1,), (1,)), ((), ())), preferred_element_type=F32)


def _dot_tn(a, b):
    return lax.dot_general(a, b, (((0,), (0,)), ((), ())), preferred_element_type=F32)


def _shr(x, pow2):
    return lax.shift_right_logical(x, jnp.int32(int(math.log2(pow2))))


def _ffn_kernel(x_ref, g_ref, wi_ref, wo_ref, o_ref, *, tf):
    x = x_ref[...]
    h = _rms(x, g_ref[...]).astype(BF16)
    acc = None
    for c in range(D_FF // tf):
        a = _dot(h, wi_ref[:, pl.ds(c * tf, tf)])
        b = _dot(h, wi_ref[:, pl.ds(D_FF + c * tf, tf)])
        part = _dot((_silu(a) * b).astype(BF16), wo_ref[pl.ds(c * tf, tf), :])
        acc = part if acc is None else acc + part
    o_ref[...] = x + 0.5 * acc


def _resident(shape, layer=None):
    if layer is None:
        return pl.BlockSpec(shape, lambda *_: (0,) * len(shape), pipeline_mode=pl.Buffered(1))
    return pl.BlockSpec((None,) + tuple(shape), lambda *_: (layer,) + (0,) * len(shape),
                        pipeline_mode=pl.Buffered(1))


def _ffn_half(x, g, w_i, w_o, *, layer, tm, tf):
    n = x.shape[0]
    return pl.pallas_call(
        functools.partial(_ffn_kernel, tf=tf),
        grid=(n // tm,),
        in_specs=[
            pl.BlockSpec((tm, D_MODEL), lambda i: (i, 0)),
            _resident((1, D_MODEL), layer),
            _resident((D_MODEL, 2 * D_FF), layer),
            _resident((D_FF, D_MODEL), layer),
        ],
        out_specs=pl.BlockSpec((tm, D_MODEL), lambda i: (i, 0)),
        out_shape=jax.ShapeDtypeStruct((n, D_MODEL), F32),
        compiler_params=_cparams(("parallel",)),
        name="ffn_half",
    )(x, g, w_i, w_o)


def _inproj_kernel(x_ref, g_ref, w_ref, gm_ref, qn_ref, kn_ref, cos_ref, sa_ref, sb_ref, *rest, tq, chained):
    rest = rest[2:] if chained else rest
    pa_ref, pb_ref, kf_ref, vf_ref, pg_ref, qkv_ref = rest[:6]
    qt_ref, vt_ref = rest[6:8] if tq is not None else (None, None)

    def store_head_rows(ref, y):
        for h in range(ATT_HEADS):
            ref[pl.ds(h, y.shape[0], stride=ATT_HEADS), :] = y[:, h * ATT_DV:(h + 1) * ATT_DV]

    def store_transposed(ref, y):
        if ref is None:
            return
        yt = y.T.astype(BF16)
        for h in range(ATT_HEADS):
            for c in range(y.shape[0] // tq):
                ref[h, c] = yt[h * ATT_DV:(h + 1) * ATT_DV, c * tq:(c + 1) * tq]

    h = _rms(x_ref[...], g_ref[...]).astype(BF16)

    def proj(seg):
        return _dot(h, w_ref[:, pl.ds(seg * SEG, SEG)])

    rep = SEG // LANES
    cos = jnp.concatenate([cos_ref[...]] * rep, axis=1)
    sa = jnp.concatenate([sa_ref[...]] * rep, axis=1)
    sb = jnp.concatenate([sb_ref[...]] * rep, axis=1)

    def norm_rope(y, gain):
        ms = _dot((y * y).astype(BF16), gm_ref[...])
        yn = y * lax.rsqrt(ms + EPS) * gain
        half = ROT_DIMS // 2
        return yn * cos + pltpu.roll(yn, half, 1) * sa + pltpu.roll(yn, SEG - half, 1) * sb

    pa_ref[:, pl.ds(0, SEG)] = proj(0)

    q = norm_rope(proj(1), qn_ref[...]) * (ATT_DK ** -0.5)
    qkv_ref[:, pl.ds(0, SEG)] = q.astype(BF16)
    store_transposed(qt_ref, q)

    k = norm_rope(proj(2), kn_ref[...])
    store_head_rows(kf_ref, k)
    qkv_ref[:, pl.ds(SEG, SEG)] = k.astype(BF16)

    v = proj(3)
    store_head_rows(vf_ref, v)
    qkv_ref[:, pl.ds(2 * SEG, SEG)] = v.astype(BF16)
    store_transposed(vt_ref, v)

    pa_ref[:, pl.ds(SEG, SEG)] = proj(4)
    pb_ref[:, pl.ds(0, SEG)] = proj(5).astype(BF16)
    pb_ref[:, pl.ds(SEG, SEG)] = proj(6).astype(BF16)
    for s in range(N_SEG_G):
        pg_ref[:, pl.ds(s * SEG, SEG)] = (0.5 * proj(N_SEG_A + s)).astype(BF16)


def _in_proj(x, g, w, gm, qn, kn, cos, sa, sb, kv_rows, *, tm, seq_len, tq, layer, depth):
    n = x.shape[0]
    tab_blocks = cos.shape[0] // tm
    nseg = N_SEG_A + N_SEG_G
    tab_spec = pl.BlockSpec((tm, LANES), lambda i: (i % tab_blocks, 0))
    kv_spec = pl.BlockSpec((tm * ATT_HEADS, ATT_DV), lambda i: (layer * (n // tm) + i, 0))
    kv_shape = jax.ShapeDtypeStruct((depth * n * ATT_HEADS, ATT_DV), F32)

    def rows(width):
        return pl.BlockSpec((tm, width), lambda i: (i, 0))

    out_specs = [rows(2 * SEG), rows(2 * SEG), kv_spec, kv_spec, rows(N_SEG_G * SEG), rows(3 * SEG)]
    out_shape = [
        jax.ShapeDtypeStruct((n, 2 * SEG), F32),
        jax.ShapeDtypeStruct((n, 2 * SEG), BF16),
        kv_shape, kv_shape,
        jax.ShapeDtypeStruct((n, N_SEG_G * SEG), BF16),
        jax.ShapeDtypeStruct((n, 3 * SEG), BF16),
    ]
    chained = kv_rows is not None
    n_in = 9
    any_spec = pl.BlockSpec(memory_space=pl.ANY)
    extra_in = list(kv_rows) if chained else []
    aliases = {n_in: 2, n_in + 1: 3} if chained else {}
    if tq is not None:
        per_seq = seq_len // tm
        t_spec = pl.BlockSpec((None, ATT_HEADS, tm // tq, ATT_DV, tq),
                              lambda i: (i // per_seq, 0, i % per_seq, 0, 0))
        t_shape = jax.ShapeDtypeStruct((n // seq_len, ATT_HEADS, seq_len // tq, ATT_DV, tq), BF16)
        out_specs += [t_spec, t_spec]
        out_shape += [t_shape, t_shape]
    return pl.pallas_call(
        functools.partial(_inproj_kernel, tq=tq, chained=chained),
        grid=(n // tm,),
        in_specs=[
            rows(D_MODEL),
            _resident((1, D_MODEL), layer),
            _resident((D_MODEL, nseg * SEG), layer),
            _resident((SEG, SEG)),
            _resident((1, SEG), layer), _resident((1, SEG), layer), tab_spec, tab_spec, tab_spec,
        ] + [any_spec] * len(extra_in),
        out_specs=out_specs,
        out_shape=out_shape,
        input_output_aliases=aliases,
        compiler_params=_cparams(("parallel",)),
        name="in_proj",
    )(x, g, w, gm, qn, kn, cos, sa, sb, *extra_in)


def _rope_tables(p0, t):
    half = ROT_DIMS // 2
    inv = ROPE_THETA ** (-jnp.arange(half, dtype=F32) / half)
    in_head = jnp.arange(LANES) % ATT_DK
    pos = p0 + jnp.arange(t, dtype=jnp.int32)
    ang = pos.astype(F32)[:, None] * inv[in_head % half][None, :]
    cos, sin = jnp.cos(ang), jnp.sin(ang)
    lower = (in_head < half)[None, :]
    upper = jnp.logical_and(in_head >= half, in_head < ROT_DIMS)[None, :]
    return (jnp.where(in_head[None, :] < ROT_DIMS, cos, 1.0),
            jnp.where(upper, sin, 0.0),
            jnp.where(lower, -sin, 0.0))


def _stack_maps(q):
    lane = lax.broadcasted_iota(jnp.int32, q.shape, 1)
    zero = jnp.zeros_like(q)
    return jnp.concatenate([jnp.where(lane < ATT_DK, q, zero), jnp.where(lane >= ATT_DK, q, zero)], axis=0)


def _attn_finish(acc, l, lam, gsub, post_scale, tq):
    o = acc[:tq] / l[:tq] - lam * (acc[tq:] / l[tq:])
    return _rms(o, gsub) * post_scale


def _attn_prompt_kernel(lam_ref, gsub_ref, q_ref, k_ref, v_ref, o_ref, acc_sc, m_sc, l_sc, *, tq, post_scale):
    i = pl.program_id(2)
    qbd = _stack_maps(q_ref[...])
    m_sc[...] = jnp.full_like(m_sc, MASK_VALUE)
    l_sc[...] = jnp.zeros_like(l_sc)
    acc_sc[...] = jnp.zeros_like(acc_sc)

    def step(j, masked):
        off = pl.multiple_of(j * tq, tq)
        kj = k_ref[pl.ds(off, tq), :]
        vj = v_ref[pl.ds(off, tq), :]
        s = _dot_nt(qbd, kj)
        if masked:
            qi = lax.broadcasted_iota(jnp.int32, s.shape, 0) & (tq - 1)
            ki = lax.broadcasted_iota(jnp.int32, s.shape, 1)
            s = jnp.where(_shr(ki, CHUNK) <= _shr(qi, CHUNK), s, MASK_VALUE)
        m_prev = m_sc[...]
        m_new = jnp.maximum(m_prev, jnp.max(s, axis=-1, keepdims=True))
        alpha = jnp.exp(m_prev - m_new)
        p = jnp.exp(s - m_new)
        l_sc[...] = alpha * l_sc[...] + jnp.sum(p, axis=-1, keepdims=True)
        acc_sc[...] = alpha * acc_sc[...] + _dot(p.astype(BF16), vj)
        m_sc[...] = m_new

    def body(j, carry):
        step(j, False)
        return carry

    lax.fori_loop(0, i, body, 0)
    step(i, True)
    o_ref[...] = _attn_finish(acc_sc[...], l_sc[...], lam_ref[...], gsub_ref[...], post_scale, tq)


def _attn_prompt(qkv, lam, gsub, post_scale, *, tq):
    b, t, _ = qkv.shape
    kern = functools.partial(_attn_prompt_kernel, tq=tq, post_scale=post_scale)
    vec_spec = pl.BlockSpec((1, ATT_DV), lambda b_, h, i: (0, 0))
    return pl.pallas_call(
        kern,
        grid=(b, ATT_HEADS, t // tq),
        in_specs=[
            vec_spec, vec_spec,
            pl.BlockSpec((None, tq, ATT_DV), lambda b_, h, i: (b_, i, h)),
            pl.BlockSpec((None, t, ATT_DV), lambda b_, h, i: (b_, 0, ATT_HEADS + h)),
            pl.BlockSpec((None, t, ATT_DV), lambda b_, h, i: (b_, 0, 2 * ATT_HEADS + h)),
        ],
        out_specs=pl.BlockSpec((None, tq, ATT_DV), lambda b_, h, i: (b_, i, h)),
        out_shape=jax.ShapeDtypeStruct((b, t, ATT_HEADS * ATT_DV), F32),
        scratch_shapes=[pltpu.VMEM((2 * tq, ATT_DV), F32), pltpu.VMEM((2 * tq, 1), F32),
                        pltpu.VMEM((2 * tq, 1), F32)],
        compiler_params=_cparams(("parallel", "parallel", "arbitrary")),
        name="diff_attn_prompt",
    )(lam, gsub, qkv, qkv, qkv)


def _attn_bounded_kernel(lam_ref, gsub_ref, qt_ref, k_ref, vt_ref, o_ref, acc_sc, l_sc, p_sc, *, tq, post_scale):
    i = pl.program_id(2)
    qt = qt_ref[...].astype(F32)
    row = lax.broadcasted_iota(jnp.int32, qt.shape, 0)
    qbd = jnp.concatenate([jnp.where(row < ATT_DK, qt, 0.0), jnp.where(row >= ATT_DK, qt, 0.0)],
                          axis=1).astype(BF16)
    acc_sc[...] = jnp.zeros_like(acc_sc)
    l_sc[...] = jnp.zeros_like(l_sc)

    def weights_of(j, masked):
        off = pl.multiple_of(j * tq, tq)
        s = _dot(k_ref[pl.ds(off, tq), :], qbd)
        if masked:
            ki = lax.broadcasted_iota(jnp.int32, s.shape, 0)
            qi = lax.broadcasted_iota(jnp.int32, s.shape, 1) & (tq - 1)
            s = jnp.where(_shr(ki, CHUNK) <= _shr(qi, CHUNK), s, MASK_VALUE)
        p = jnp.exp(s)
        return jnp.sum(p.reshape(tq // SUBLANES, SUBLANES, 2 * tq), axis=0), p.astype(BF16)

    def total(xs):
        return functools.reduce(lambda x, y: x + y, xs)

    group = KEY_TILES_PER_ITER
    n_groups = i // group

    def stage_scores(g, slot):
        off = pl.multiple_of(g * (group * tq), group * tq)
        p = jnp.exp(_dot(k_ref[pl.ds(off, group * tq), :], qbd))
        for c in range(group):
            p_sc[slot, c] = p[c * tq:(c + 1) * tq].astype(BF16)
        l_sc[...] += jnp.sum(p.reshape(group * tq // SUBLANES, SUBLANES, 2 * tq), axis=0)

    def stage_values(g, slot):
        acc_sc[...] += total([_dot(vt_ref[g * group + c], p_sc[slot, c]) for c in range(group)])

    @pl.when(n_groups > 0)
    def _():
        stage_scores(0, 0)

    steps = jnp.maximum(n_groups - 1, 0)

    def two_steps(t, carry):
        stage_values(2 * t, 0)
        stage_scores(2 * t + 1, 1)
        stage_values(2 * t + 1, 1)
        stage_scores(2 * t + 2, 0)
        return carry

    lax.fori_loop(0, steps // 2, two_steps, 0)

    @pl.when(steps % 2 == 1)
    def _():
        stage_values(n_groups - 2, 0)
        stage_scores(n_groups - 1, 1)

    rem = i % group
    for r in range(group):
        for drain in (False, True):
            @pl.when(jnp.logical_and(rem == r, (n_groups > 0) == drain))
            def _(r=r, drain=drain):
                parts = [weights_of(i - r + c, c == r) for c in range(r + 1)]
                if drain:
                    stage_values(n_groups - 1, (n_groups - 1) & 1)
                l_sc[...] += total([p_[0] for p_ in parts])
                acc_sc[...] += total([_dot(vt_ref[i - r + c], parts[c][1]) for c in range(r + 1)])
    l = jnp.sum(l_sc[...], axis=0, keepdims=True)
    acc = acc_sc[...]
    ot = acc[:, :tq] / l[:, :tq] - lam_ref[...] * (acc[:, tq:] / l[:, tq:])
    ms = jnp.mean(ot * ot, axis=0, keepdims=True)
    yt = ot * lax.rsqrt(ms + EPS) * (gsub_ref[...] * post_scale)
    o_ref[...] = yt.T


def _attn_bounded(qkv, q5, v5, lam, gsub, post_scale):
    b, t, _ = qkv.shape
    nt, tq = q5.shape[2], q5.shape[4]
    kern = functools.partial(_attn_bounded_kernel, tq=tq, post_scale=post_scale)
    return pl.pallas_call(
        kern,
        grid=(b, ATT_HEADS, nt),
        in_specs=[
            pl.BlockSpec((1, 1), lambda b_, h, i: (0, 0)),
            pl.BlockSpec((ATT_DV, 1), lambda b_, h, i: (0, 0)),
            pl.BlockSpec((None, None, None, ATT_DV, tq), lambda b_, h, i: (b_, h, i, 0, 0)),
            pl.BlockSpec((None, t, ATT_DV), lambda b_, h, i: (b_, 0, ATT_HEADS + h)),
            pl.BlockSpec((None, None, nt, ATT_DV, tq), lambda b_, h, i: (b_, h, 0, 0, 0)),
        ],
        out_specs=pl.BlockSpec((None, tq, ATT_DV), lambda b_, h, i: (b_, i, h)),
        out_shape=jax.ShapeDtypeStruct((b, t, ATT_HEADS * ATT_DV), F32),
        scratch_shapes=[pltpu.VMEM((ATT_DV, 2 * tq), F32), pltpu.VMEM((SUBLANES, 2 * tq), F32),
                        pltpu.VMEM((2, KEY_TILES_PER_ITER, tq, 2 * tq), BF16)],
        compiler_params=_cparams(("parallel", "parallel", "arbitrary")),
        name="diff_attn_bounded",
    )(lam[:, :1], gsub.reshape(ATT_DV, 1), q5, qkv, v5)


def _attn_sample_kernel(lam_ref, gsub_ref, q_ref, kn_ref, vn_ref, kp_ref, vp_ref, o_ref, *, tq, tp, post_scale):
    for h in range(ATT_HEADS):
        cols = pl.ds(h * ATT_DV, ATT_DV)
        head_rows = pl.ds(h, tp, stride=ATT_HEADS)
        qbd = _stack_maps(q_ref[:, cols])
        sp = _dot_nt(qbd, kp_ref[head_rows, :].astype(BF16))
        sn = _dot_nt(qbd, kn_ref[:, cols])
        m = jnp.maximum(jnp.max(sp, axis=-1, keepdims=True), jnp.max(sn, axis=-1, keepdims=True))
        pp = jnp.exp(sp - m)
        pn = jnp.exp(sn - m)
        l = jnp.sum(pp, axis=-1, keepdims=True) + jnp.sum(pn, axis=-1, keepdims=True)
        acc = (_dot(pp.astype(BF16), vp_ref[head_rows, :].astype(BF16))
               + _dot(pn.astype(BF16), vn_ref[:, cols]))
        o_ref[:, cols] = _attn_finish(acc, l, lam_ref[...], gsub_ref[...], post_scale, tq)


def _attn_sample(qkv, kp, vp, lam, gsub, post_scale, *, layer):
    b, t, _ = qkv.shape
    tp = kp.shape[2] // ATT_HEADS
    kern = functools.partial(_attn_sample_kernel, tq=t, tp=tp, post_scale=post_scale)
    vec_spec = pl.BlockSpec((1, ATT_DV), lambda b_: (0, 0))
    past_spec = pl.BlockSpec((None, None, tp * ATT_HEADS, ATT_DV), lambda b_: (layer, b_, 0, 0))
    width = ATT_HEADS * ATT_DV
    return pl.pallas_call(
        kern,
        grid=(b,),
        in_specs=[
            vec_spec, vec_spec,
            pl.BlockSpec((None, t, width), lambda b_: (b_, 0, 0)),
            pl.BlockSpec((None, t, width), lambda b_: (b_, 0, 1)),
            pl.BlockSpec((None, t, width), lambda b_: (b_, 0, 2)),
            past_spec, past_spec,
        ],
        out_specs=pl.BlockSpec((None, t, width), lambda b_: (b_, 0, 0)),
        out_shape=jax.ShapeDtypeStruct((b, t, width), F32),
        compiler_params=_cparams(("parallel",)),
        name="diff_attn_sample",
    )(lam, gsub, qkv, qkv, qkv, kp, vp)


HG_W = HG_HEADS * HG_DK
HG_V = HG_HEADS * HG_DV
B_PAD = 8


def _level_reference(b_sc, w):
    if w >= 4:
        pieces = []
        for p in range(CHUNK // (2 * w)):
            row = b_sc[pl.ds(B_PAD + p * 2 * w + w, 1), :]
            pieces.append(jnp.broadcast_to(row, (2 * w, HG_W)))
        return jnp.concatenate(pieces, axis=0)
    t = lax.broadcasted_iota(jnp.int32, (CHUNK, HG_W), 0)
    phase = t & (2 * w - 1)
    r = None
    for ph in range(2 * w):
        shifted = b_sc[pl.ds(B_PAD + w - ph, CHUNK), :]
        r = shifted if r is None else jnp.where(phase == ph, shifted, r)
    return r


def _hgrn_chunk(hq, hf, hi, lb, st_sc, b_sc, masks):
    tril, bd_k, bd_v, bd_s = masks
    f = lb + (1.0 - lb) * jax.nn.sigmoid(hf)
    g = jnp.log(f)
    kk = 1.0 - f
    q = _silu(hq) * (HG_DK ** -0.5)
    g0 = g.astype(BF16)
    r1 = g - g0.astype(F32)
    g1 = r1.astype(BF16)
    g2 = (r1 - g1.astype(F32)).astype(BF16)
    b = _dot(tril, g0) + _dot(tril, g1) + _dot(tril, g2)
    b_sc[pl.ds(B_PAD, CHUNK), :] = b

    t_idx = lax.broadcasted_iota(jnp.int32, (CHUNK, HG_W), 0)
    s_idx = lax.broadcasted_iota(jnp.int32, (CHUNK, HG_W), 1) & (CHUNK - 1)
    zero = jnp.zeros((CHUNK, HG_W), F32)

    def block_diag_k(x):
        return jnp.where(bd_k, jnp.concatenate([x] * HG_HEADS, axis=0), jnp.zeros((), F32)).astype(BF16)

    a = jnp.where(t_idx == s_idx, _dot_nt(q.astype(BF16), block_diag_k(kk)), zero)
    w = CHUNK // 2
    while w >= 1:
        r = _level_reference(b_sc, w)
        upper = (t_idx & w) != 0
        e = jnp.exp(jnp.where(upper, b - r, r - b))
        ql = jnp.where(upper, q * e, zero)
        kl = jnp.where(upper, zero, kk * e)
        same_pair = _shr(t_idx, 2 * w) == _shr(s_idx, 2 * w)
        a = a + jnp.where(same_pair, _dot_nt(ql.astype(BF16), block_diag_k(kl)), zero)
        w //= 2

    b_last = jnp.broadcast_to(b_sc[pl.ds(B_PAD + CHUNK - 1, 1), :], (CHUNK, HG_W))
    qb = q * jnp.exp(b)
    kd = kk * jnp.exp(b_last - b)
    v16 = hi
    vbd = jnp.where(bd_v, jnp.concatenate([hi.astype(F32)] * HG_HEADS, axis=0), jnp.zeros((), F32)).astype(BF16)
    st = st_sc[...]
    o = _dot(a.astype(BF16), vbd) + _dot_nt(qb.astype(BF16), st.astype(BF16))
    upd = _dot_tn(v16, kd.astype(BF16))
    decay = jnp.exp(b_sc[pl.ds(B_PAD + CHUNK - 1, 1), :])
    st_sc[...] = st * decay + jnp.where(bd_s, upd, jnp.zeros((), F32))
    return o


def _mix_kernel(x_ref, u_ref, hqf_ref, hi_ref, hg_ref, g_ref, ya_ref, st0_ref, pool0_ref,
                lb_ref, pw_ref, ps_ref, hgn_ref, wup_ref, wua_ref, wuh_ref, wo_ref,
                xo_ref, stn_ref, pooln_ref,
                st_sc, b_sc, ext_sc, lvl_sc, yh_sc, *, tm, p0):
    it = pl.program_id(1)
    top = POOL_PAD + tm

    @pl.when(it == 0)
    def _():
        st_sc[...] = st0_ref[...]
        ext_sc[pl.ds(0, POOL_PAD - POOL_HIST), :] = jnp.zeros((POOL_PAD - POOL_HIST, SEG), F32)
        ext_sc[pl.ds(POOL_PAD - POOL_HIST, POOL_HIST), :] = pool0_ref[...]
        b_sc[...] = jnp.zeros_like(b_sc)

    ext_sc[pl.ds(POOL_PAD, tm), :] = u_ref[...]
    row = lax.broadcasted_iota(jnp.int32, (tm, LANES), 0)
    seen = (p0 + 1 + it * tm + row).astype(F32)
    cols = []
    for gi, w in enumerate(POOL_WINDOWS):
        sl = pl.ds(gi * LANES, LANES)
        stages = int(math.log2(w))

        def read(lo, n, level):
            if level == 0:
                return ext_sc[pl.ds(lo, n), sl]
            return lvl_sc[(level - 1) % 2, pl.ds(lo, n), :]

        win = None
        for k in range(1, stages + 1):
            lo = POOL_PAD if k == stages else SUBLANES * k
            win = read(lo, top - lo, k - 1) + read(lo - (1 << (k - 1)), top - lo, k - 1)
            if k < stages:
                lvl_sc[(k - 1) % 2, pl.ds(lo, top - lo), :] = win
        dlt = win / jnp.minimum(seen, float(w)) - u_ref[:, sl]
        cols.append(_dot(dlt.astype(BF16), pw_ref[gi]))
    y_pool = jnp.concatenate(cols, axis=1) * ps_ref[...]
    ext_sc[pl.ds(POOL_PAD - POOL_HIST, POOL_HIST), :] = ext_sc[pl.ds(top - POOL_HIST, POOL_HIST), :]

    tri_r = lax.broadcasted_iota(jnp.int32, (CHUNK, CHUNK), 0)
    tri_c = lax.broadcasted_iota(jnp.int32, (CHUNK, CHUNK), 1)
    tril = (tri_c <= tri_r).astype(BF16)
    rk = lax.broadcasted_iota(jnp.int32, (HG_W, HG_W), 0)
    ck = lax.broadcasted_iota(jnp.int32, (HG_W, HG_W), 1)
    bd_k = _shr(rk, CHUNK) == _shr(ck, HG_DK)
    rv = lax.broadcasted_iota(jnp.int32, (HG_W, HG_V), 0)
    cv = lax.broadcasted_iota(jnp.int32, (HG_W, HG_V), 1)
    bd_v = _shr(rv, CHUNK) == _shr(cv, HG_DV)
    rs = lax.broadcasted_iota(jnp.int32, (HG_V, HG_W), 0)
    cs = lax.broadcasted_iota(jnp.int32, (HG_V, HG_W), 1)
    bd_s = _shr(rs, HG_DV) == _shr(cs, HG_DK)
    masks = (tril, bd_k, bd_v, bd_s)
    lb = lb_ref[...]
    for c in range(tm // CHUNK):
        rows = pl.ds(c * CHUNK, CHUNK)
        o = _hgrn_chunk(hqf_ref[rows, pl.ds(0, HG_W)], hqf_ref[rows, pl.ds(HG_W, HG_W)],
                        hi_ref[rows, :], lb, st_sc, b_sc, masks)
        yh_sc[rows, :] = o
    oh = yh_sc[...]
    hg = hg_ref[...].astype(F32)
    heads = []
    for h in range(HG_HEADS):
        sl = slice(h * HG_DV, (h + 1) * HG_DV)
        heads.append(_rms(oh[:, sl], hgn_ref[...]) * _silu(hg[:, sl]))
    y_hg = jnp.concatenate(heads, axis=1)

    th = jnp.tanh(g_ref[...].astype(F32))
    merged = 0.5 * ((th[:, :D_MODEL] + 1.0) * _dot(y_pool.astype(BF16), wup_ref[...])
                    + (th[:, D_MODEL:2 * D_MODEL] + 1.0) * _dot(ya_ref[...].astype(BF16), wua_ref[...])
                    + (th[:, 2 * D_MODEL:] + 1.0) * _dot(y_hg.astype(BF16), wuh_ref[...]))
    xo_ref[...] = x_ref[...] + _dot(merged.astype(BF16), wo_ref[...])

    @pl.when(it == pl.num_programs(1) - 1)
    def _():
        stn_ref[...] = st_sc[...]
        pooln_ref[...] = ext_sc[pl.ds(POOL_PAD - POOL_HIST, POOL_HIST), :]


def _mix_out(x, proj_a, proj_b, gates, y_att, st0, pool0, lb, pool_w, pool_scale, hg_outn,
             w_up_pool, w_up_att, w_up_hgrn, w_out, *, layer, tm, p0):
    b, t, _ = x.shape
    kern = functools.partial(_mix_kernel, tm=tm, p0=p0)

    def rows(width, col):
        return pl.BlockSpec((None, tm, width), lambda b_, i: (b_, i, col))

    def const(shape):
        return _resident(shape, layer)

    def per_batch(shape):
        return pl.BlockSpec((None,) + shape, lambda b_, i: (b_, 0, 0))

    return pl.pallas_call(
        kern,
        grid=(b, t // tm),
        in_specs=[
            rows(D_MODEL, 0),
            rows(SEG, 0),
            rows(SEG, 1),
            rows(SEG, 0),
            rows(SEG, 1),
            rows(3 * D_MODEL, 0),
            rows(SEG, 0),
            per_batch((HG_V, HG_W)),
            per_batch((POOL_HIST, SEG)),
            const((1, HG_W)), const((4, LANES, LANES)), const((1, SEG)), const((1, HG_DV)),
            const((SEG, D_MODEL)), const((SEG, D_MODEL)), const((SEG, D_MODEL)), const((D_MODEL, D_MODEL)),
        ],
        out_specs=[
            rows(D_MODEL, 0),
            per_batch((HG_V, HG_W)),
            per_batch((POOL_HIST, SEG)),
        ],
        out_shape=[
            jax.ShapeDtypeStruct((b, t, D_MODEL), F32),
            jax.ShapeDtypeStruct((b, HG_V, HG_W), F32),
            jax.ShapeDtypeStruct((b, POOL_HIST, SEG), F32),
        ],
        scratch_shapes=[
            pltpu.VMEM((HG_V, HG_W), F32),
            pltpu.VMEM((CHUNK + 2 * B_PAD, HG_W), F32),
            pltpu.VMEM((POOL_PAD + tm, SEG), F32),
            pltpu.VMEM((2, POOL_PAD + tm, LANES), F32),
            pltpu.VMEM((tm, HG_V), F32),
        ],
        compiler_params=_cparams(("parallel", "arbitrary")),
        name="mix_out",
    )(x, proj_a, proj_a, proj_b, proj_b, gates, y_att, st0, pool0,
      lb, pool_w, pool_scale, hg_outn, w_up_pool, w_up_att, w_up_hgrn, w_out)


def _state_to_block_diag(s):
    b = s.shape[0]
    st = jnp.swapaxes(s, 2, 3)
    eye = jnp.eye(HG_HEADS, dtype=s.dtype)
    return jnp.einsum('bhed,hg->bhegd', st, eye).reshape(b, HG_V, HG_W)


def _block_diag_to_state(st):
    b = st.shape[0]
    s5 = st.reshape(b, HG_HEADS, HG_DV, HG_HEADS, HG_DK)
    diag = jnp.stack([s5[:, h, :, h, :] for h in range(HG_HEADS)], axis=1)
    return jnp.swapaxes(diag, 2, 3)


def _head_rms(x, gain):
    return jnp.concatenate(
        [_rms(x[:, h * X_HD:(h + 1) * X_HD], gain) for h in range(X_HEADS)], axis=1)


def _memkv_kernel(m_ref, g_ref, w_ref, kn_ref, k_ref, v_ref, k16_ref, v16_ref):
    h = _rms(m_ref[...], g_ref[...]).astype(BF16)
    kv = _dot(h, w_ref[...])
    mk = _head_rms(kv[:, :D_MODEL], kn_ref[...])
    mv = kv[:, D_MODEL:]
    k_ref[...] = mk
    v_ref[...] = mv
    k16_ref[...] = mk.astype(BF16)
    v16_ref[...] = mv.astype(BF16)


def _memory_kv(mem, g, w_ckv, kn, *, layer):
    b, n, _ = mem.shape
    blk = pl.BlockSpec((None, n, D_MODEL), lambda b_: (b_, 0, 0))
    return pl.pallas_call(
        _memkv_kernel,
        grid=(b,),
        in_specs=[
            blk,
            _resident((1, D_MODEL), layer),
            _resident((D_MODEL, 2 * D_MODEL), layer),
            _resident((1, X_HD), layer),
        ],
        out_specs=[blk, blk, blk, blk],
        out_shape=[jax.ShapeDtypeStruct((b, n, D_MODEL), F32)] * 2
        + [jax.ShapeDtypeStruct((b, n, D_MODEL), BF16)] * 2,
        compiler_params=_cparams(("parallel",)),
        name="memory_kv",
    )(mem, g, w_ckv, kn)


def _cross_kernel(x_ref, g_ref, wq_ref, qn_ref, mk_ref, mv_ref, wo_ref, o_ref, *, cache_rows):
    x = x_ref[...]
    q = _head_rms(_dot(_rms(x, g_ref[...]).astype(BF16), wq_ref[...]), qn_ref[...])
    q = (q * (X_HD ** -0.5)).astype(BF16)
    outs = []
    for h in range(X_HEADS):
        sl = slice(h * X_HD, (h + 1) * X_HD)
        if cache_rows is None:
            mk, mv = mk_ref[:, sl], mv_ref[:, sl]
        else:
            parts = X_HD // LANES

            def head_of(ref):
                halves = [ref[pl.ds(parts * h + c, cache_rows, stride=parts * X_HEADS), :] for c in range(parts)]
                return jnp.concatenate(halves, axis=1).astype(BF16)

            mk, mv = head_of(mk_ref), head_of(mv_ref)
        s = _dot_nt(q[:, sl], mk)
        p = jnp.exp(s - jnp.max(s, axis=-1, keepdims=True))
        p = p / jnp.sum(p, axis=-1, keepdims=True)
        outs.append(_dot(p.astype(BF16), mv))
    o = jnp.concatenate(outs, axis=1)
    o_ref[...] = x + _dot(o.astype(BF16), wo_ref[...])


def _cross_attend(x, mk, mv, g, w_cq, qn, w_co, *, layer, tm):
    b, t, _ = x.shape
    if mk.ndim == 3:
        cache_rows = None
        mem = pl.BlockSpec((None, mk.shape[1], D_MODEL), lambda b_, i: (b_, 0, 0))
    else:
        cache_rows = mk.shape[2] * LANES // D_MODEL
        mem = pl.BlockSpec((None, None, mk.shape[2], LANES), lambda b_, i: (layer, b_, 0, 0))
    rows = pl.BlockSpec((None, tm, D_MODEL), lambda b_, i: (b_, i, 0))
    wsq = _resident((D_MODEL, D_MODEL), layer)
    return pl.pallas_call(
        functools.partial(_cross_kernel, cache_rows=cache_rows),
        grid=(b, t // tm),
        in_specs=[
            rows,
            _resident((1, D_MODEL), layer),
            wsq,
            _resident((1, X_HD), layer),
            mem, mem, wsq,
        ],
        out_specs=rows,
        out_shape=jax.ShapeDtypeStruct((b, t, D_MODEL), F32),
        compiler_params=_cparams(("parallel", "parallel")),
        name="cross_attn",
    )(x, g, w_cq, qn, mk, mv, w_co)


def _layer(x, l, depth, p0, rope, kv_rows, past_kv, pool_prev, hg_state, mk16, mv16, w, *, tiles):
    b, t, _ = x.shape
    n = b * t
    x2 = _ffn_half(x.reshape(n, D_MODEL), w['norm_ffn1'], w['w_ffn1_in'], w['w_ffn1_out'],
                   layer=l, tm=tiles['ffn'], tf=tiles['tf'])

    proj_a, proj_b, kf, vf, gates, qkv, *tiles_t = _in_proj(
        x2, w['norm_mix'], w['w_in'], w['group_mean'], w['att_qn'], w['att_kn'], *rope, kv_rows,
        tm=tiles['proj'], seq_len=t, tq=None if past_kv is not None else tiles['attn_bounded'],
        layer=l, depth=depth)

    lam_init = 0.8 - 0.6 * math.exp(-0.3 * l)
    lam = (jnp.exp(jnp.sum(w['lq1'][l] * w['lk1'][l])) - jnp.exp(jnp.sum(w['lq2'][l] * w['lk2'][l])) + lam_init)
    lam = jnp.full((1, ATT_DV), lam, F32)
    gsub = w['att_subln'][l].reshape(1, ATT_DV)
    qkv3 = qkv.reshape(b, t, 3 * SEG)
    if past_kv is None:
        bound = (ATT_DK ** 0.5) * jnp.max(jnp.abs(w['att_qn'][l])) * jnp.max(jnp.abs(w['att_kn'][l]))
        y_att = lax.cond(
            bound <= SCORE_BOUND_MAX,
            lambda a, q5, v5: _attn_bounded(a, q5, v5, lam, gsub, 1.0 - lam_init),
            lambda a, q5, v5: _attn_prompt(a, lam, gsub, 1.0 - lam_init, tq=tiles['attn']),
            qkv3, *tiles_t)
    else:
        y_att = _attn_sample(qkv3, past_kv[0], past_kv[1], lam, gsub, 1.0 - lam_init, layer=l)

    x3, st_new, pool_new = _mix_out(
        x2.reshape(b, t, D_MODEL), proj_a.reshape(b, t, 2 * SEG), proj_b.reshape(b, t, 2 * SEG),
        gates.reshape(b, t, N_SEG_G * SEG),
        y_att, _state_to_block_diag(hg_state), pool_prev, w['lb'],
        w['pool_w'], w['pool_scale'], w['hg_outn'], w['w_up_pool'], w['w_up_att'], w['w_up_hgrn'], w['w_out'],
        layer=l, tm=tiles['mix'], p0=p0)

    x4 = _cross_attend(x3, mk16, mv16, w['norm_cross'], w['w_cq'], w['cross_qn'], w['w_co'],
                       layer=l, tm=tiles['cross'])
    x5 = _ffn_half(x4.reshape(n, D_MODEL), w['norm_ffn2'], w['w_ffn2_in'], w['w_ffn2_out'],
                   layer=l, tm=tiles['ffn'], tf=tiles['tf'])
    return (x5.reshape(b, t, D_MODEL), (kf, vf), pool_new[:, POOL_HIST - POOL_STATE:],
            _block_diag_to_state(st_new))


PROMPT_TILES = dict(ffn=1024, tf=256, proj=512, attn=256, attn_bounded=512, mix=512, cross=1024)
SAMPLE_TILES = dict(ffn=512, tf=256, proj=512, attn=64, mix=64, cross=64)


def kernel(x_prompt, x_sample, cache_attn_k, cache_attn_v, cache_mem_k, cache_mem_v, state_pool, state_hgrn, mem_prompt, norm_ffn1, w_ffn1_in, w_ffn1_out, norm_mix, w_in, pool_w, pool_scale, att_q_norm, att_k_norm, lambda_q1, lambda_k1, lambda_q2, lambda_k2, att_subln, hgrn_lower, hgrn_out_norm, w_up_pool, w_up_att, w_up_hgrn, w_out, norm_cross, norm_mem, w_cq, w_ckv, cross_q_norm, cross_k_norm, w_co, norm_ffn2, w_ffn2_in, w_ffn2_out):
    depth = w_in.shape[0]
    bp = x_prompt.shape[0]
    bs = x_sample.shape[0]
    p0_sample = cache_attn_k.shape[2]

    lp = jax.nn.softmax(hgrn_lower.astype(F32), axis=0)
    lbs = jnp.cumsum(lp, axis=0) - lp[0:1]

    gidx = jnp.arange(SEG) // ATT_DK
    group_mean = ((gidx[:, None] == gidx[None, :]).astype(F32) / ATT_DK).astype(BF16)

    def row(p):
        return p.reshape(depth, 1, p.shape[-1])

    weights = dict(
        norm_ffn1=row(norm_ffn1), w_ffn1_in=w_ffn1_in.astype(BF16), w_ffn1_out=w_ffn1_out.astype(BF16),
        norm_mix=row(norm_mix), w_in=w_in.astype(BF16), group_mean=group_mean,
        att_qn=row(jnp.tile(att_q_norm, (1, SEG // ATT_DK))), att_kn=row(jnp.tile(att_k_norm, (1, SEG // ATT_DK))),
        lq1=lambda_q1.astype(F32), lk1=lambda_k1.astype(F32), lq2=lambda_q2.astype(F32), lk2=lambda_k2.astype(F32),
        att_subln=att_subln,
        lb=row(lbs), pool_w=pool_w.astype(BF16), pool_scale=row(pool_scale), hg_outn=row(hgrn_out_norm),
        w_up_pool=w_up_pool.astype(BF16), w_up_att=w_up_att.astype(BF16),
        w_up_hgrn=w_up_hgrn.astype(BF16), w_out=w_out.astype(BF16),
        norm_cross=row(norm_cross), w_cq=w_cq.astype(BF16), cross_qn=row(cross_q_norm), w_co=w_co.astype(BF16),
        norm_ffn2=row(norm_ffn2), w_ffn2_in=w_ffn2_in.astype(BF16), w_ffn2_out=w_ffn2_out.astype(BF16),
    )
    norm_mem_r, w_ckv16, cross_kn_r = row(norm_mem), w_ckv.astype(BF16), row(cross_k_norm)

    def rope_for(p0, t, tm):
        tabs = _rope_tables(p0, t)
        if t < tm:
            tabs = tuple(jnp.concatenate([a] * (tm // t), axis=0) for a in tabs)
        return tabs

    rope_prompt = rope_for(0, x_prompt.shape[1], PROMPT_TILES['proj'])
    rope_sample = rope_for(p0_sample, x_sample.shape[1], SAMPLE_TILES['proj'])

    y = x_prompt
    pkv = None
    pmk, pmv, ppool, phg = [], [], [], []
    for l in range(depth):
        mk, mv, mk16, mv16 = _memory_kv(mem_prompt, norm_mem_r, w_ckv16, cross_kn_r, layer=l)
        pool0 = jnp.zeros((bp, POOL_HIST, SEG), F32)
        hg0 = jnp.zeros((bp, HG_HEADS, HG_DK, HG_DV), F32)
        y, pkv, pn, hn = _layer(y, l, depth, 0, rope_prompt, pkv, None, pool0, hg0,
                                mk16, mv16, weights, tiles=PROMPT_TILES)
        pmk.append(mk.reshape(bp, -1, X_HEADS, X_HD)); pmv.append(mv.reshape(bp, -1, X_HEADS, X_HD))
        ppool.append(pn); phg.append(hn)
    y_prompt = y
    kv_shape = (depth, bp, x_prompt.shape[1], ATT_HEADS, ATT_DV)
    pk, pv = pkv[0].reshape(kv_shape), pkv[1].reshape(kv_shape)

    y = x_sample
    skv = None
    spool, shg = [], []
    past = (cache_attn_k.reshape(depth, bs, p0_sample * ATT_HEADS, ATT_DV),
            cache_attn_v.reshape(depth, bs, p0_sample * ATT_HEADS, ATT_DV))
    mem_slots = cache_mem_k.shape[2]
    mem_k = cache_mem_k.reshape(depth, bs, mem_slots * D_MODEL // LANES, LANES)
    mem_v = cache_mem_v.reshape(depth, bs, mem_slots * D_MODEL // LANES, LANES)
    for l in range(depth):
        pool0 = jnp.pad(state_pool[l], ((0, 0), (POOL_HIST - POOL_STATE, 0), (0, 0)))
        y, skv, pn, hn = _layer(y, l, depth, p0_sample, rope_sample, skv, past, pool0, state_hgrn[l],
                                mem_k, mem_v, weights, tiles=SAMPLE_TILES)
        spool.append(pn); shg.append(hn)
    y_sample = y
    kv_shape = (depth, bs, x_sample.shape[1], ATT_HEADS, ATT_DV)
    sk, sv = skv[0].reshape(kv_shape), skv[1].reshape(kv_shape)

    return (y_prompt, y_sample,
            pk, pv, jnp.stack(pmk), jnp.stack(pmv),
            jnp.stack(ppool), jnp.stack(phg),
            sk, sv, jnp.stack(spool), jnp.stack(shg))
```
